```python
import numpy as np
import jax
import jax.numpy as jnp
from jax import lax

D_MODEL = 4096
BATCH = 2
SEQ = 8192
DEPTH = 4

HEAD_DIM = 128
ROPE_THETA = 10000.0
NORM_EPS = 1e-6
Q_BLOCK = 128
N_BRANCH = 4
BRANCH_WIDTH = D_MODEL // 4
DSA_HEADS = BRANCH_WIDTH // HEAD_DIM
IDX_HEADS = 16
IDX_DIM = 64
DSA_TOPK_MAX = 256
CONV_WIDTH = BRANCH_WIDTH
CONV_K = 3
FOX_HEADS = BRANCH_WIDTH // HEAD_DIM
HGRN_HEADS = BRANCH_WIDTH // HEAD_DIM
HGRN_DK = 128
HGRN_DV = 128
HGRN_CHUNK = 64
MERGE_BLOCKS = 16
MERGE_BLOCK_DIM = D_MODEL // MERGE_BLOCKS

IN_LAYOUT = (
    ("a_q", DSA_HEADS * HEAD_DIM), ("a_k", HEAD_DIM), ("a_v", HEAD_DIM),
    ("a_iq", IDX_HEADS * IDX_DIM), ("a_ik", IDX_DIM), ("a_iw", IDX_HEADS), ("a_g", BRANCH_WIDTH),
    ("b_b", CONV_WIDTH), ("b_c", CONV_WIDTH), ("b_x", CONV_WIDTH), ("b_g", BRANCH_WIDTH),
    ("c_q", FOX_HEADS * HEAD_DIM), ("c_k", FOX_HEADS * HEAD_DIM), ("c_v", FOX_HEADS * HEAD_DIM),
    ("c_f", FOX_HEADS), ("c_g", BRANCH_WIDTH),
    ("d_q", HGRN_HEADS * HGRN_DK), ("d_f", HGRN_HEADS * HGRN_DK), ("d_i", HGRN_HEADS * HGRN_DV),
    ("d_g", BRANCH_WIDTH),
)
IN_WIDTH = int(sum(w for _, w in IN_LAYOUT))
IN_OFFSETS = tuple(int(o) for o in np.cumsum([w for _, w in IN_LAYOUT])[:-1])

kernel_name = "hybrid_dsa_conv_fox_hgrn2_gated_merge"


def rms_norm(x, w):
    xf = x.astype(jnp.float32)
    y = xf * lax.rsqrt(jnp.mean(xf * xf, axis=-1, keepdims=True) + NORM_EPS)
    return (y * w.astype(jnp.float32)).astype(x.dtype)


def rotary(x, pos):
    half = x.shape[-1] // 2
    inv = ROPE_THETA ** (-jnp.arange(half, dtype=jnp.float32) / half)
    ang = pos.astype(jnp.float32)[:, None] * inv[None, :]
    cos = jnp.cos(ang)[None, :, None, :]
    sin = jnp.sin(ang)[None, :, None, :]
    xf = x.astype(jnp.float32)
    x1, x2 = xf[..., :half], xf[..., half:]
    return jnp.concatenate([x1 * cos - x2 * sin, x1 * sin + x2 * cos], axis=-1).astype(x.dtype)


def to_blocks(t):
    b, s = t.shape[:2]
    return jnp.moveaxis(t.reshape((b, s // Q_BLOCK, Q_BLOCK) + t.shape[2:]), 1, 0)


def from_blocks(t):
    t = jnp.moveaxis(t, 0, 1)
    return t.reshape((t.shape[0], t.shape[1] * t.shape[2]) + t.shape[3:])


def dsa_sparse_attention(q, k, v, iq, ik, iw):
    b, s = q.shape[:2]
    topk = min(DSA_TOPK_MAX, s // 4)
    pos = jnp.arange(s)
    scale = HEAD_DIM ** -0.5
    iw = iw.astype(jnp.float32) * (IDX_HEADS ** -0.5 * IDX_DIM ** -0.5)
    gather = jax.vmap(lambda t, i: t[i])

    def block(args):
        qb, iqb, iwb, tq = args
        idx_logit = jnp.einsum('bqhd,bkd->bqhk', iqb, ik).astype(jnp.float32)
        score = jnp.einsum('bqh,bqhk->bqk', iwb, jax.nn.relu(idx_logit))
        causal = pos[None, :] <= tq[:, None]
        score = jnp.where(causal[None], score, -jnp.inf)
        _, sel = lax.top_k(score, topk)
        valid = sel <= tq[None, :, None]
        kg = gather(k, sel)
        vg = gather(v, sel)
        logit = jnp.einsum('bqhd,bqnd->bqhn', qb, kg).astype(jnp.float32) * scale
        logit = jnp.where(valid[:, :, None, :], logit, -jnp.inf)
        p = jax.nn.softmax(logit, axis=-1).astype(v.dtype)
        return jnp.einsum('bqhn,bqnd->bqhd', p, vg)

    out = lax.map(block, (to_blocks(q), to_blocks(iq), to_blocks(iw), pos.reshape(-1, Q_BLOCK)))
    return from_blocks(out).reshape(b, s, -1)


def short_gated_conv(b_gate, c_gate, x_in, conv_w):
    u = c_gate * x_in
    y = lax.conv_general_dilated(
        u, conv_w[:, None, :].astype(u.dtype), window_strides=(1,),
        padding=[(CONV_K - 1, 0)], dimension_numbers=('NWC', 'WIO', 'NWC'),
        feature_group_count=u.shape[-1])
    return b_gate * y


def forgetting_attention(q, k, v, log_f):
    s = q.shape[1]
    pos = jnp.arange(s)
    scale = HEAD_DIM ** -0.5
    c = jnp.cumsum(log_f, axis=1)
    c_key = jnp.transpose(c, (0, 2, 1))

    def block(args):
        qb, cb, tq = args
        logit = jnp.einsum('bqhd,bkhd->bhqk', qb, k).astype(jnp.float32) * scale
        logit = logit + jnp.transpose(cb, (0, 2, 1))[..., None] - c_key[:, :, None, :]
        causal = pos[None, :] <= tq[:, None]
        logit = jnp.where(causal[None, None], logit, -jnp.inf)
        p = jax.nn.softmax(logit, axis=-1).astype(v.dtype)
        return jnp.einsum('bhqk,bkhd->bqhd', p, v)

    out = lax.map(block, (to_blocks(q), to_blocks(c), pos.reshape(-1, Q_BLOCK)))
    return from_blocks(out)


def hgrn2_recurrence(q, k, v, log_f):
    b, s, h, dk = q.shape
    dv = v.shape[-1]
    nc = s // HGRN_CHUNK

    def chunks(t):
        return t.reshape(b, nc, HGRN_CHUNK, h, t.shape[-1]).transpose(1, 0, 3, 2, 4)

    tri = jnp.tril(jnp.ones((HGRN_CHUNK, HGRN_CHUNK), dtype=bool))

    def step(state, inp):
        qc, kc, vc, gc = inp
        bcum = jnp.cumsum(gc, axis=2)
        o_inter = jnp.einsum('bhtd,bhde->bhte', qc * jnp.exp(bcum), state)
        diff = bcum[:, :, :, None, :] - bcum[:, :, None, :, :]
        decay = jnp.exp(jnp.where(tri[:, :, None], diff, -jnp.inf))
        att = jnp.einsum('bhtd,bhtsd,bhsd->bhts', qc, decay, kc)
        o_intra = jnp.einsum('bhts,bhse->bhte', att, vc)
        b_last = bcum[:, :, -1:, :]
        new_state = (jnp.exp(b_last[:, :, 0, :])[..., None] * state
                     + jnp.einsum('bhsd,bhse->bhde', kc * jnp.exp(b_last - bcum), vc))
        return new_state, o_inter + o_intra

    state0 = jnp.zeros((b, h, dk, dv), jnp.float32)
    _, o = lax.scan(step, state0, (chunks(q), chunks(k), chunks(v), chunks(log_f)))
    return o.transpose(1, 0, 3, 2, 4).reshape(b, s, h, dv)


def hybrid_layer(x, norm_w, w_in, fox_f_bias, conv_w, hgrn_lb, hgrn_norm_w, w_branch, w_merge, b_merge, w_out):
    b, s, d = x.shape
    pos = jnp.arange(s)
    h = rms_norm(x, norm_w)
    u = h @ w_in
    (a_q, a_k, a_v, a_iq, a_ik, a_iw, a_g,
     b_b, b_c, b_x, b_g,
     c_q, c_k, c_v, c_f, c_g,
     d_q, d_f, d_i, d_g) = jnp.split(u, IN_OFFSETS, axis=-1)

    def heads(t, n):
        return t.reshape(b, s, n, -1)

    qa = rotary(heads(a_q, DSA_HEADS), pos)
    ka = rotary(a_k[:, :, None, :], pos)[:, :, 0]
    iq = rotary(heads(a_iq, IDX_HEADS), pos)
    ik = rotary(a_ik[:, :, None, :], pos)[:, :, 0]
    y_a = dsa_sparse_attention(qa, ka, a_v, iq, ik, a_iw) * jax.nn.silu(a_g)

    y_b = short_gated_conv(b_b, b_c, b_x, conv_w) * jax.nn.silu(b_g)

    log_f_c = jax.nn.log_sigmoid(c_f.astype(jnp.float32) + fox_f_bias.astype(jnp.float32))
    o_c = forgetting_attention(heads(c_q, FOX_HEADS), heads(c_k, FOX_HEADS), heads(c_v, FOX_HEADS), log_f_c)
    y_c = o_c.reshape(b, s, -1) * jax.nn.silu(c_g)

    lb = hgrn_lb.astype(jnp.float32)
    f = lb + (1.0 - lb) * jax.nn.sigmoid(d_f.astype(jnp.float32))
    o_d = hgrn2_recurrence(heads(d_q.astype(jnp.float32), HGRN_HEADS), heads(1.0 - f, HGRN_HEADS),
                           heads(d_i.astype(jnp.float32), HGRN_HEADS), heads(jnp.log(f), HGRN_HEADS))
    o_d = rms_norm(o_d, hgrn_norm_w.reshape(HGRN_HEADS, HGRN_DV)).astype(x.dtype)
    y_d = o_d.reshape(b, s, -1) * jax.nn.silu(d_g)

    h_blk = h.reshape(b, s, MERGE_BLOCKS, MERGE_BLOCK_DIM)

    def gated_branch(i, y):
        g = jnp.einsum('bsnc,ncd->bsnd', h_blk, w_merge[i]).reshape(b, s, d)
        return jax.nn.sigmoid(g + b_merge[i]) * (y @ w_branch[i])

    merged = gated_branch(0, y_a) + gated_branch(1, y_b) + gated_branch(2, y_c) + gated_branch(3, y_d)
    return x + merged @ w_out


def setup_inputs(seed: int = 0) -> dict:
    key = jax.random.key(seed)
    ks = jax.random.split(key, 12)

    def nrm(k, shape, scale):
        return jax.random.normal(k, shape, jnp.float32) * scale

    return {
        "x": nrm(ks[0], (BATCH, SEQ, D_MODEL), 1.0),
        "norm_w": 1.0 + nrm(ks[1], (DEPTH, D_MODEL), 0.02),
        "w_in": nrm(ks[2], (DEPTH, D_MODEL, IN_WIDTH), D_MODEL ** -0.5),
        "fox_f_bias": 1.0 + nrm(ks[3], (DEPTH, FOX_HEADS), 0.1),
        "conv_w": nrm(ks[4], (DEPTH, CONV_K, CONV_WIDTH), CONV_K ** -0.5),
        "hgrn_gamma": nrm(ks[5], (DEPTH, HGRN_HEADS * HGRN_DK), 1.0),
        "hgrn_norm_w": 1.0 + nrm(ks[6], (DEPTH, HGRN_HEADS * HGRN_DV), 0.02),
        "w_branch": nrm(ks[7], (DEPTH, N_BRANCH, BRANCH_WIDTH, D_MODEL), BRANCH_WIDTH ** -0.5),
        "w_merge": nrm(ks[8], (DEPTH, N_BRANCH, MERGE_BLOCKS, MERGE_BLOCK_DIM, MERGE_BLOCK_DIM), MERGE_BLOCK_DIM ** -0.5),
        "b_merge": nrm(ks[9], (DEPTH, N_BRANCH, D_MODEL), 0.01),
        "w_out": nrm(ks[10], (DEPTH, D_MODEL, D_MODEL), D_MODEL ** -0.5),
        "final_norm_w": 1.0 + nrm(ks[11], (D_MODEL,), 0.02),
    }


def reference(x, norm_w, w_in, fox_f_bias, conv_w, hgrn_gamma, hgrn_norm_w, w_branch, w_merge, b_merge, w_out, final_norm_w):
    gam = jax.nn.softmax(hgrn_gamma.astype(jnp.float32), axis=0)
    lower_bounds = jnp.cumsum(gam, axis=0) - gam[0]
    for layer in range(DEPTH):
        x = hybrid_layer(x, norm_w[layer], w_in[layer], fox_f_bias[layer], conv_w[layer],
                         lower_bounds[layer], hgrn_norm_w[layer], w_branch[layer],
                         w_merge[layer], b_merge[layer], w_out[layer])
    return rms_norm(x, final_norm_w)
```

```python
import functools

import numpy as np
import jax
import jax.numpy as jnp
from jax import lax
from jax.experimental import pallas as pl
from jax.experimental.pallas import tpu as pltpu

F32 = jnp.float32
BF16 = jnp.bfloat16

D_MODEL = 4096
HEAD_DIM = 128
ROPE_THETA = 10000.0
NORM_EPS = 1e-6
Q_BLOCK = 128
BRANCH_WIDTH = D_MODEL // 4
N_HEADS = BRANCH_WIDTH // HEAD_DIM
IDX_HEADS = 16
IDX_DIM = 64
DSA_TOPK_MAX = 256
CONV_K = 3
HGRN_CHUNK = 64
HGRN_SUB = 16
MERGE_BLOCKS = 16
MERGE_BLOCK_DIM = D_MODEL // MERGE_BLOCKS

LANES = 128
VMEM_LIMIT = 56 * 1024 * 1024

_SRC_LAYOUT = (
    ("a_q", 1024), ("a_k", 128), ("a_v", 128), ("a_iq", 1024), ("a_ik", 64), ("a_iw", 16), ("a_g", 1024),
    ("b_b", 1024), ("b_c", 1024), ("b_x", 1024), ("b_g", 1024),
    ("c_q", 1024), ("c_k", 1024), ("c_v", 1024), ("c_f", 8), ("c_g", 1024),
    ("d_q", 1024), ("d_f", 1024), ("d_i", 1024), ("d_g", 1024),
)
_SRC_OFF = {}
_o = 0
for _n, _w in _SRC_LAYOUT:
    _SRC_OFF[_n] = (_o, _w)
    _o += _w
IN_WIDTH = _o

_BIG = ("a_q", "a_iq", "a_g", "b_b", "b_c", "b_x", "b_g", "c_q", "c_k", "c_v", "c_g", "d_q", "d_f", "d_i", "d_g")
BIG = {n: i for i, n in enumerate(_BIG)}
SMALL_BASE = len(_BIG) * 1024 // LANES
SLOT_AK, SLOT_AV, SLOT_AIKW, SLOT_CF = SMALL_BASE, SMALL_BASE + 1, SMALL_BASE + 2, SMALL_BASE + 3
PAD_WIDTH = (SMALL_BASE + 4) * LANES

INT_MIN = np.int32(-2 ** 31)
NT_DIMS = (((1,), (1,)), ((), ()))
TN_DIMS = (((0,), (0,)), ((), ()))


def _cparams(sem):
    return pltpu.CompilerParams(dimension_semantics=sem, vmem_limit_bytes=VMEM_LIMIT)


def _pad_in_proj(w_in):
    def seg(n):
        o, w = _SRC_OFF[n]
        return w_in[..., o:o + w]

    def zeros(w):
        return jnp.zeros(w_in.shape[:-1] + (w,), w_in.dtype)

    parts = [seg(n) for n in _BIG]
    parts += [seg("a_k"), seg("a_v"), seg("a_ik"), seg("a_iw"), zeros(LANES - IDX_DIM - IDX_HEADS),
              seg("c_f"), zeros(LANES - N_HEADS)]
    return jnp.concatenate(parts, axis=-1).astype(BF16)


def _lb_kernel(g_ref, o_ref):
    g = g_ref[...]
    e = jnp.exp(g - jnp.max(g, axis=0, keepdims=True))
    sm = e / jnp.sum(e, axis=0, keepdims=True)
    acc = jnp.zeros_like(sm[0:1])
    rows = []
    for layer in range(g.shape[0]):
        acc = acc + sm[layer:layer + 1]
        rows.append(acc - sm[0:1])
    o_ref[...] = jnp.concatenate(rows, axis=0)


def _lower_bounds(gamma):
    return pl.pallas_call(
        _lb_kernel, out_shape=jax.ShapeDtypeStruct(gamma.shape, F32), name="hgrn_lower_bounds",
    )(gamma.astype(F32))


def _rmsnorm_kernel(x_ref, w_ref, o_ref):
    x = x_ref[...]
    ms = jnp.mean(x * x, axis=-1, keepdims=True)
    o_ref[...] = (x * lax.rsqrt(ms + NORM_EPS) * w_ref[...]).astype(o_ref.dtype)


def _rmsnorm(x, w, out_dtype, tm=256):
    t, d = x.shape
    return pl.pallas_call(
        _rmsnorm_kernel,
        out_shape=jax.ShapeDtypeStruct((t, d), out_dtype),
        grid=(t // tm,),
        in_specs=[pl.BlockSpec((tm, d), lambda i: (i, 0)), pl.BlockSpec((1, d), lambda i: (0, 0))],
        out_specs=pl.BlockSpec((tm, d), lambda i: (i, 0)),
        compiler_params=_cparams(("parallel",)),
        name="rmsnorm",
    )(x, w.reshape(1, d))


def _in_proj_kernel(a_ref, b_ref, o32_ref, o16_ref):
    acc = jnp.dot(a_ref[...], b_ref[...], preferred_element_type=F32)
    o32_ref[...] = acc
    o16_ref[...] = acc.astype(BF16)


def _in_proj(h, w, tm=1024, tn=512):
    t, k = h.shape
    n = w.shape[1]
    tm = min(tm, t)
    return pl.pallas_call(
        _in_proj_kernel,
        out_shape=(jax.ShapeDtypeStruct((t, n), F32), jax.ShapeDtypeStruct((t, n), BF16)),
        grid=(t // tm, n // tn),
        in_specs=[pl.BlockSpec((tm, k), lambda i, j: (i, 0)), pl.BlockSpec((k, tn), lambda i, j: (0, j))],
        out_specs=(pl.BlockSpec((tm, tn), lambda i, j: (i, j)), pl.BlockSpec((tm, tn), lambda i, j: (i, j))),
        compiler_params=_cparams(("parallel", "parallel")),
        name="in_proj",
    )(h, w)


def _out_proj_kernel(a_ref, b_ref, r_ref, o_ref):
    o_ref[...] = r_ref[...] + jnp.dot(a_ref[...], b_ref[...], preferred_element_type=F32)


def _out_proj(a, w, resid, tm=1024, tn=512):
    t, k = a.shape
    n = w.shape[1]
    tm = min(tm, t)
    return pl.pallas_call(
        _out_proj_kernel,
        out_shape=jax.ShapeDtypeStruct((t, n), F32),
        grid=(t // tm, n // tn),
        in_specs=[pl.BlockSpec((tm, k), lambda i, j: (i, 0)), pl.BlockSpec((k, tn), lambda i, j: (0, j)),
                  pl.BlockSpec((tm, tn), lambda i, j: (i, j))],
        out_specs=pl.BlockSpec((tm, tn), lambda i, j: (i, j)),
        compiler_params=_cparams(("parallel", "parallel")),
        name="out_proj",
    )(a, w, resid)


def _rope_tables(seq):
    pos = jnp.arange(seq).astype(F32)

    def tables(dim):
        half = dim // 2
        inv = ROPE_THETA ** (-jnp.arange(half, dtype=F32) / half)
        ang = pos[:, None] * inv[None, :]
        cos, sin = jnp.cos(ang), jnp.sin(ang)
        reps = LANES // dim
        cos_t = jnp.tile(jnp.concatenate([cos, cos], axis=1), (1, reps))
        sin_t = jnp.tile(jnp.concatenate([-sin, sin], axis=1), (1, reps))
        return cos_t, sin_t

    return tables(HEAD_DIM) + tables(IDX_DIM)


def _dsa_prep_kernel(q_ref, iq_ref, k_ref, ikw_ref, cos_ref, sin_ref, cosi_ref, sini_ref,
                     qo_ref, ko_ref, iqo_ref, iko_ref, iwo_ref):
    cos, sin = cos_ref[...], sin_ref[...]
    cosi, sini = cosi_ref[...], sini_ref[...]
    tm = cos.shape[0]

    def rot_head(x):
        return x * cos + pltpu.roll(x, HEAD_DIM // 2, axis=1) * sin

    lane = lax.broadcasted_iota(jnp.int32, (tm, LANES), 1)
    first_half = (lane % IDX_DIM) < (IDX_DIM // 2)

    def rot_idx(x):
        partner = jnp.where(first_half, pltpu.roll(x, LANES - IDX_DIM // 2, axis=1),
                            pltpu.roll(x, IDX_DIM // 2, axis=1))
        return x * cosi + partner * sini

    for h in range(N_HEADS):
        qo_ref[:, h * HEAD_DIM:(h + 1) * HEAD_DIM] = rot_head(q_ref[:, h * HEAD_DIM:(h + 1) * HEAD_DIM]).astype(BF16)
    ko_ref[...] = rot_head(k_ref[...]).astype(BF16)
    for p in range(IDX_HEADS // 2):
        r = rot_idx(iq_ref[:, p * LANES:(p + 1) * LANES]).astype(BF16)
        iqo_ref[2 * p] = r[:, :IDX_DIM]
        iqo_ref[2 * p + 1] = r[:, IDX_DIM:]
    ikw = ikw_ref[...]
    iko_ref[...] = rot_idx(ikw)[:, :IDX_DIM].astype(BF16)
    iwo_ref[...] = ikw[:, IDX_DIM:IDX_DIM + IDX_HEADS] * (IDX_HEADS ** -0.5 * IDX_DIM ** -0.5)


def _dsa_prep(u32, tabs, seq, tm=256):
    t = u32.shape[0]
    nseq = seq // tm
    cos_a, sin_a, cos_i, sin_i = tabs
    tab_spec = pl.BlockSpec((tm, LANES), lambda i: (i % nseq, 0))
    return pl.pallas_call(
        _dsa_prep_kernel,
        out_shape=(jax.ShapeDtypeStruct((t, BRANCH_WIDTH), BF16),
                   jax.ShapeDtypeStruct((t, HEAD_DIM), BF16),
                   jax.ShapeDtypeStruct((IDX_HEADS, t, IDX_DIM), BF16),
                   jax.ShapeDtypeStruct((t, IDX_DIM), BF16),
                   jax.ShapeDtypeStruct((t, IDX_HEADS), F32)),
        grid=(t // tm,),
        in_specs=[pl.BlockSpec((tm, 1024), lambda i: (i, BIG["a_q"])),
                  pl.BlockSpec((tm, 1024), lambda i: (i, BIG["a_iq"])),
                  pl.BlockSpec((tm, LANES), lambda i: (i, SLOT_AK)),
                  pl.BlockSpec((tm, LANES), lambda i: (i, SLOT_AIKW)),
                  tab_spec, tab_spec, tab_spec, tab_spec],
        out_specs=(pl.BlockSpec((tm, BRANCH_WIDTH), lambda i: (i, 0)),
                   pl.BlockSpec((tm, HEAD_DIM), lambda i: (i, 0)),
                   pl.BlockSpec((IDX_HEADS, tm, IDX_DIM), lambda i: (0, i, 0)),
                   pl.BlockSpec((tm, IDX_DIM), lambda i: (i, 0)),
                   pl.BlockSpec((tm, IDX_HEADS), lambda i: (i, 0))),
        compiler_params=_cparams(("parallel",)),
        name="dsa_prep",
    )(u32, u32, u32, u32, cos_a, sin_a, cos_i, sin_i)


DSA_TK = 256


def _dsa_kernel(iq_ref, iw_ref, ik_ref, q_ref, k_ref, v_ref, g_ref, o_ref,
                keys_ref, wb_ref, m_ref, l_ref, acc_ref, *, topk, seq):
    qb = Q_BLOCK
    tk = DSA_TK
    nl = tk // LANES
    i = pl.program_id(1)
    q0 = i * qb
    nch = (q0 + qb + tk - 1) // tk
    scale = HEAD_DIM ** -0.5

    iw = iw_ref[...]
    for h in range(IDX_HEADS):
        wb_ref[h] = jnp.broadcast_to(iw[:, h:h + 1], (qb, LANES))
    iq2 = iq_ref[...].reshape(IDX_HEADS * qb, IDX_DIM)
    row = q0 + lax.broadcasted_iota(jnp.int32, (qb, tk), 0)
    col0 = lax.broadcasted_iota(jnp.int32, (qb, tk), 1)

    def score_chunk(c, carry):
        k0 = pl.multiple_of(c * tk, tk)
        logit = lax.dot_general(iq2, ik_ref[pl.ds(k0, tk), :], NT_DIMS, preferred_element_type=F32)
        parts = []
        for j in range(nl):
            s = jnp.zeros((qb, LANES), F32)
            for h in range(IDX_HEADS):
                s = s + wb_ref[h] * jnp.maximum(logit[h * qb:(h + 1) * qb, j * LANES:(j + 1) * LANES], 0.0)
            parts.append(s)
        s = jnp.concatenate(parts, axis=1)
        bits = lax.bitcast_convert_type(s, jnp.int32)
        key = bits ^ ((bits >> 31) & jnp.int32(0x7FFFFFFF))
        keys_ref[c] = jnp.where(k0 + col0 <= row, key, INT_MIN)
        return carry

    lax.fori_loop(0, nch, score_chunk, 0)

    def count(pred):
        lane = lax.broadcasted_iota(jnp.int32, (qb, LANES), 1)

        def body(c, acc):
            kc = keys_ref[c]
            for j in range(nl):
                idx = c * tk + j * LANES + lane
                acc = acc + jnp.where(pred(kc[:, j * LANES:(j + 1) * LANES], idx), 1, 0)
            return acc

        acc = lax.fori_loop(0, nch, body, jnp.zeros((qb, LANES), jnp.int32))
        return jnp.broadcast_to(jnp.sum(acc, axis=1, keepdims=True), (qb, LANES))

    zero = jnp.zeros((qb, LANES), jnp.int32)
    prefix = jnp.where(count(lambda k, idx: k >= zero) >= topk, zero, INT_MIN)

    def bisect(it, prefix):
        cand = prefix + jnp.left_shift(jnp.int32(1), 30 - it)
        return jnp.where(count(lambda k, idx: k >= cand) >= topk, cand, prefix)

    tau = lax.fori_loop(0, 31, bisect, prefix)
    n_gt = count(lambda k, idx: k > tau)
    n_eq = count(lambda k, idx: k == tau)
    need = topk - n_gt

    def tie_search(_):
        def body(it, p):
            cand = p + jnp.left_shift(jnp.int32(1), 14 - it)
            below = count(lambda k, idx: (k == tau) & (idx < cand))
            return jnp.where(below < need, cand, p)
        return lax.fori_loop(0, 15, body, zero) + 1

    has_tie = jnp.max(jnp.where((n_eq > need) & (tau > INT_MIN), 1, 0)) > 0
    jlim = lax.cond(has_tie, tie_search, lambda _: jnp.full((qb, LANES), seq, jnp.int32), 0)

    q = q_ref[...]
    q2 = jnp.concatenate([q[:, h * HEAD_DIM:(h + 1) * HEAD_DIM] for h in range(N_HEADS)], axis=0)
    m_ref[...] = jnp.full(m_ref.shape, -jnp.inf, F32)
    l_ref[...] = jnp.zeros(l_ref.shape, F32)
    acc_ref[...] = jnp.zeros(acc_ref.shape, F32)
    lane = lax.broadcasted_iota(jnp.int32, (qb, LANES), 1)

    def attend(c, carry):
        k0 = pl.multiple_of(c * tk, tk)
        kc = keys_ref[c]
        bias = []
        for j in range(nl):
            kj = kc[:, j * LANES:(j + 1) * LANES]
            idx = c * tk + j * LANES + lane
            sel = (kj > INT_MIN) & ((kj > tau) | ((kj == tau) & (idx < jlim)))
            bias.append(jnp.where(sel, 0.0, -jnp.inf))
        bias = jnp.concatenate(bias, axis=1)
        s = lax.dot_general(q2, k_ref[pl.ds(k0, tk), :], NT_DIMS, preferred_element_type=F32) * scale
        s = s + jnp.concatenate([bias] * N_HEADS, axis=0)
        m_prev = m_ref[...]
        m_new = jnp.maximum(m_prev, jnp.max(s, axis=1, keepdims=True))
        m_safe = jnp.where(m_new == -jnp.inf, 0.0, m_new)
        p = jnp.exp(s - m_safe)
        alpha = jnp.exp(m_prev - m_safe)
        l_ref[...] = alpha * l_ref[...] + jnp.sum(p, axis=1, keepdims=True)
        acc_ref[...] = alpha * acc_ref[...] + jnp.dot(p.astype(BF16), v_ref[pl.ds(k0, tk), :],
                                                      preferred_element_type=F32)
        m_ref[...] = m_new
        return carry

    lax.fori_loop(0, nch, attend, 0)
    o = acc_ref[...] / l_ref[...]
    o = jnp.concatenate([o[h * qb:(h + 1) * qb, :] for h in range(N_HEADS)], axis=1)
    g = g_ref[...]
    o_ref[...] = (o * (g * jax.nn.sigmoid(g))).astype(o_ref.dtype)


def _dsa(iq, iw, ik, qa, ka, u16, u32, batch, seq):
    t = qa.shape[0]
    nb = seq // Q_BLOCK
    topk = min(DSA_TOPK_MAX, seq // 4)
    kern = functools.partial(_dsa_kernel, topk=topk, seq=seq)
    return pl.pallas_call(
        kern,
        out_shape=jax.ShapeDtypeStruct((t, BRANCH_WIDTH), BF16),
        grid=(batch, nb),
        in_specs=[pl.BlockSpec((IDX_HEADS, Q_BLOCK, IDX_DIM), lambda b, i: (0, b * nb + i, 0)),
                  pl.BlockSpec((Q_BLOCK, IDX_HEADS), lambda b, i: (b * nb + i, 0)),
                  pl.BlockSpec((seq, IDX_DIM), lambda b, i: (b, 0)),
                  pl.BlockSpec((Q_BLOCK, BRANCH_WIDTH), lambda b, i: (b * nb + i, 0)),
                  pl.BlockSpec((seq, HEAD_DIM), lambda b, i: (b, 0)),
                  pl.BlockSpec((seq, HEAD_DIM), lambda b, i: (b, SLOT_AV)),
                  pl.BlockSpec((Q_BLOCK, 1024), lambda b, i: (b * nb + i, BIG["a_g"]))],
        out_specs=pl.BlockSpec((Q_BLOCK, BRANCH_WIDTH), lambda b, i: (b * nb + i, 0)),
        scratch_shapes=[pltpu.VMEM((seq // DSA_TK, Q_BLOCK, DSA_TK), jnp.int32),
                        pltpu.VMEM((IDX_HEADS, Q_BLOCK, LANES), F32),
                        pltpu.VMEM((N_HEADS * Q_BLOCK, 1), F32),
                        pltpu.VMEM((N_HEADS * Q_BLOCK, 1), F32),
                        pltpu.VMEM((N_HEADS * Q_BLOCK, HEAD_DIM), F32)],
        compiler_params=_cparams(("parallel", "arbitrary")),
        name="dsa_attention",
    )(iq, iw, ik, qa, ka, u16, u32)


def _conv_kernel(b_ref, c_ref, x_ref, g_ref, cp_ref, xp_ref, w_ref, o_ref, *, tiles_per_seq):
    i = pl.program_id(0)
    tm = b_ref.shape[0]
    u = c_ref[...] * x_ref[...]
    prev = cp_ref[...] * xp_ref[...]
    prev = jnp.where(i % tiles_per_seq == 0, 0.0, prev)
    row = lax.broadcasted_iota(jnp.int32, u.shape, 0)
    u1 = jnp.where(row == 0, prev[7:8, :], pltpu.roll(u, 1, axis=0))
    u2 = jnp.where(row == 0, prev[6:7, :], jnp.where(row == 1, prev[7:8, :], pltpu.roll(u, 2, axis=0)))
    w = w_ref[...]
    y = w[0:1, :] * u2 + w[1:2, :] * u1 + w[2:3, :] * u
    g = g_ref[...]
    o_ref[...] = (b_ref[...] * y * (g * jax.nn.sigmoid(g))).astype(o_ref.dtype)
    del tm


def _conv(u32, conv_w, seq, tm=512):
    t = u32.shape[0]
    tps = seq // tm
    kern = functools.partial(_conv_kernel, tiles_per_seq=tps)

    def big(name):
        return pl.BlockSpec((tm, 1024), lambda i: (i, BIG[name]))

    def halo(name):
        return pl.BlockSpec((8, 1024), lambda i: (jnp.maximum(i * (tm // 8) - 1, 0), BIG[name]))

    return pl.pallas_call(
        kern,
        out_shape=jax.ShapeDtypeStruct((t, BRANCH_WIDTH), BF16),
        grid=(t // tm,),
        in_specs=[big("b_b"), big("b_c"), big("b_x"), big("b_g"), halo("b_c"), halo("b_x"),
                  pl.BlockSpec((CONV_K, BRANCH_WIDTH), lambda i: (0, 0))],
        out_specs=pl.BlockSpec((tm, BRANCH_WIDTH), lambda i: (i, 0)),
        compiler_params=_cparams(("parallel",)),
        name="short_conv",
    )(u32, u32, u32, u32, u32, u32, conv_w)


FOX_TK = 512


def _fox_cum_kernel(cf_ref, bias_ref, ccol_ref, crow_ref, carry_ref):
    j = pl.program_id(1)

    @pl.when(j == 0)
    def _():
        carry_ref[...] = jnp.zeros_like(carry_ref)

    x = cf_ref[...] + bias_ref[...]
    log_f = jnp.minimum(x, 0.0) - jnp.log1p(jnp.exp(-jnp.abs(x)))
    tc = x.shape[0]
    tri = (lax.broadcasted_iota(jnp.int32, (tc, tc), 1) <= lax.broadcasted_iota(jnp.int32, (tc, tc), 0)).astype(F32)
    cs = jnp.dot(tri, log_f, precision=lax.Precision.HIGHEST, preferred_element_type=F32) + carry_ref[...]
    ccol_ref[...] = cs
    carry_ref[...] = cs[tc - 1:tc, :]
    crow_ref[...] = cs.T[0:N_HEADS, :]


def _fox_cum(u32, bias, batch, seq):
    t = u32.shape[0]
    tc = FOX_TK
    ns = seq // tc
    bias_p = jnp.zeros((1, LANES), F32).at[0, :N_HEADS].set(bias.astype(F32))
    return pl.pallas_call(
        _fox_cum_kernel,
        out_shape=(jax.ShapeDtypeStruct((t, LANES), F32),
                   jax.ShapeDtypeStruct((batch, ns, N_HEADS, tc), F32)),
        grid=(batch, ns),
        in_specs=[pl.BlockSpec((tc, LANES), lambda b, j: (b * ns + j, SLOT_CF)),
                  pl.BlockSpec((1, LANES), lambda b, j: (0, 0))],
        out_specs=(pl.BlockSpec((tc, LANES), lambda b, j: (b * ns + j, 0)),
                   pl.BlockSpec((None, None, N_HEADS, tc), lambda b, j: (b, j, 0, 0))),
        scratch_shapes=[pltpu.VMEM((1, LANES), F32)],
        compiler_params=_cparams(("parallel", "arbitrary")),
        name="fox_cumsum",
    )(u32, bias_p)


def _fox_kernel(q_ref, k_ref, v_ref, ccol_ref, crow_ref, g_ref, o_ref, m_ref, l_ref, acc_ref, *, tq):
    tk = FOX_TK
    h = pl.program_id(1)
    q0 = pl.program_id(2) * tq
    nk = (q0 + tq + tk - 1) // tk
    scale = HEAD_DIM ** -0.5
    q = q_ref[...]
    lane = lax.broadcasted_iota(jnp.int32, (tq, LANES), 1)
    cq = jnp.sum(jnp.where(lane == h, ccol_ref[...], 0.0), axis=1, keepdims=True)
    m_ref[...] = jnp.full(m_ref.shape, -jnp.inf, F32)
    l_ref[...] = jnp.zeros(l_ref.shape, F32)
    acc_ref[...] = jnp.zeros(acc_ref.shape, F32)
    row = q0 + lax.broadcasted_iota(jnp.int32, (tq, tk), 0)
    col0 = lax.broadcasted_iota(jnp.int32, (tq, tk), 1)
    sub = lax.broadcasted_iota(jnp.int32, (N_HEADS, tk), 0)

    def body(c, carry):
        k0 = pl.multiple_of(c * tk, tk)
        ck = jnp.sum(jnp.where(sub == h, crow_ref[c], 0.0), axis=0, keepdims=True)
        s = lax.dot_general(q, k_ref[pl.ds(k0, tk), :], NT_DIMS, preferred_element_type=F32) * scale
        s = s + cq - ck
        s = jnp.where(k0 + col0 <= row, s, -jnp.inf)
        m_prev = m_ref[...]
        m_new = jnp.maximum(m_prev, jnp.max(s, axis=1, keepdims=True))
        p = jnp.exp(s - m_new)
        alpha = jnp.exp(m_prev - m_new)
        l_ref[...] = alpha * l_ref[...] + jnp.sum(p, axis=1, keepdims=True)
        acc_ref[...] = alpha * acc_ref[...] + jnp.dot(p.astype(BF16), v_ref[pl.ds(k0, tk), :],
                                                      preferred_element_type=F32)
        m_ref[...] = m_new
        return carry

    lax.fori_loop(0, nk, body, 0)
    g = g_ref[...]
    o_ref[...] = (acc_ref[...] / l_ref[...] * (g * jax.nn.sigmoid(g))).astype(o_ref.dtype)


def _fox(u16, u32, ccol, crow, batch, seq, tq=256):
    t = u16.shape[0]
    tq = min(tq, seq)
    nq = seq // tq
    nkc = seq // FOX_TK
    cq, ck, cv, cg = (BIG[n] * 1024 // LANES for n in ("c_q", "c_k", "c_v", "c_g"))
    kern = functools.partial(_fox_kernel, tq=tq)
    return pl.pallas_call(
        kern,
        out_shape=jax.ShapeDtypeStruct((t, BRANCH_WIDTH), BF16),
        grid=(batch, N_HEADS, nq),
        in_specs=[pl.BlockSpec((tq, HEAD_DIM), lambda b, h, i: (b * nq + i, cq + h)),
                  pl.BlockSpec((seq, HEAD_DIM), lambda b, h, i: (b, ck + h)),
                  pl.BlockSpec((seq, HEAD_DIM), lambda b, h, i: (b, cv + h)),
                  pl.BlockSpec((tq, LANES), lambda b, h, i: (b * nq + i, 0)),
                  pl.BlockSpec((None, nkc, N_HEADS, FOX_TK), lambda b, h, i: (b, 0, 0, 0)),
                  pl.BlockSpec((tq, HEAD_DIM), lambda b, h, i: (b * nq + i, cg + h))],
        out_specs=pl.BlockSpec((tq, HEAD_DIM), lambda b, h, i: (b * nq + i, h)),
        scratch_shapes=[pltpu.VMEM((tq, 1), F32), pltpu.VMEM((tq, 1), F32), pltpu.VMEM((tq, HEAD_DIM), F32)],
        compiler_params=_cparams(("parallel", "parallel", "arbitrary")),
        name="fox_attention",
    )(u16, u16, u16, ccol, crow, u32)


def _hgrn_kernel(q_ref, f_ref, i_ref, g_ref, lb_ref, nw_ref, o_ref, st_ref):
    cs, sub = HGRN_CHUNK, HGRN_SUB
    ts = q_ref.shape[0]

    @pl.when(pl.program_id(2) == 0)
    def _():
        st_ref[...] = jnp.zeros_like(st_ref)

    lb = lb_ref[...]
    nw = nw_ref[...]
    tri = (lax.broadcasted_iota(jnp.int32, (cs, cs), 1) <= lax.broadcasted_iota(jnp.int32, (cs, cs), 0)).astype(F32)
    row_c = lax.broadcasted_iota(jnp.int32, (cs, HEAD_DIM), 0)
    row_s = lax.broadcasted_iota(jnp.int32, (sub, HEAD_DIM), 0)
    lane_s = lax.broadcasted_iota(jnp.int32, (sub, cs), 1)

    def chunk(ci, carry):
        r0 = pl.multiple_of(ci * cs, cs)
        f = lb + (1.0 - lb) * jax.nn.sigmoid(f_ref[pl.ds(r0, cs), :])
        kk = 1.0 - f
        bc = jnp.dot(tri, jnp.log(f), precision=lax.Precision.HIGHEST, preferred_element_type=F32)
        q = q_ref[pl.ds(r0, cs), :]
        vb = i_ref[pl.ds(r0, cs), :].astype(BF16)
        st = st_ref[...]
        o = lax.dot_general((q * jnp.exp(bc)).astype(BF16), st.astype(BF16), NT_DIMS, preferred_element_type=F32)
        att_rows = []
        for si in range(cs // sub):
            lo = si * sub
            bi = bc[lo:lo + sub, :]
            qi = q[lo:lo + sub, :]
            att = jnp.zeros((sub, cs), F32)
            if si > 0:
                b0 = bc[lo - 1:lo, :]
                kt = kk * jnp.exp(jnp.where(row_c < lo, b0 - bc, -jnp.inf))
                qs = qi * jnp.exp(bi - b0)
                att = lax.dot_general(qs.astype(BF16), kt.astype(BF16), NT_DIMS, preferred_element_type=F32)
            for s in range(sub):
                bs = bc[lo + s:lo + s + 1, :]
                ks = kk[lo + s:lo + s + 1, :]
                p = qi * ks * jnp.exp(jnp.where(row_s >= s, bi - bs, -jnp.inf))
                att = att + jnp.where(lane_s == lo + s, jnp.sum(p, axis=1, keepdims=True), 0.0)
            att_rows.append(att)
        att = jnp.concatenate(att_rows, axis=0)
        o = o + jnp.dot(att.astype(BF16), vb, preferred_element_type=F32)
        bl = bc[cs - 1:cs, :]
        kd = kk * jnp.exp(bl - bc)
        st_ref[...] = st * jnp.exp(bl) + lax.dot_general(vb, kd.astype(BF16), TN_DIMS, preferred_element_type=F32)
        y = o * lax.rsqrt(jnp.mean(o * o, axis=1, keepdims=True) + NORM_EPS) * nw
        g = g_ref[pl.ds(r0, cs), :]
        o_ref[pl.ds(r0, cs), :] = (y * (g * jax.nn.sigmoid(g))).astype(o_ref.dtype)
        return carry

    lax.fori_loop(0, ts // cs, chunk, 0)


def _hgrn(u32, lb, norm_w, batch, seq, ts=512):
    t = u32.shape[0]
    ts = min(ts, seq)
    ns = seq // ts
    dq, df, di, dg = (BIG[n] * 1024 // LANES for n in ("d_q", "d_f", "d_i", "d_g"))

    def col(c0):
        return pl.BlockSpec((ts, HEAD_DIM), lambda b, h, j: (b * ns + j, c0 + h))

    vec = pl.BlockSpec((1, HEAD_DIM), lambda b, h, j: (0, h))
    return pl.pallas_call(
        _hgrn_kernel,
        out_shape=jax.ShapeDtypeStruct((t, BRANCH_WIDTH), BF16),
        grid=(batch, N_HEADS, ns),
        in_specs=[col(dq), col(df), col(di), col(dg), vec, vec],
        out_specs=pl.BlockSpec((ts, HEAD_DIM), lambda b, h, j: (b * ns + j, h)),
        scratch_shapes=[pltpu.VMEM((HEAD_DIM, HEAD_DIM), F32)],
        compiler_params=_cparams(("parallel", "parallel", "arbitrary")),
        name="hgrn2",
    )(u32, u32, u32, u32, lb.reshape(1, BRANCH_WIDTH), norm_w.reshape(1, BRANCH_WIDTH).astype(F32))


def _merge_kernel(ya_ref, yb_ref, yc_ref, yd_ref, h_ref, wb_ref, wm_ref, bm_ref, o_ref):
    tn = o_ref.shape[1]
    h = h_ref[...]
    merged = None
    for br, y_ref in enumerate((ya_ref, yb_ref, yc_ref, yd_ref)):
        proj = jnp.dot(y_ref[...], wb_ref[br], preferred_element_type=F32)
        gate = jnp.concatenate(
            [jnp.dot(h[:, n * MERGE_BLOCK_DIM:(n + 1) * MERGE_BLOCK_DIM], wm_ref[br, n], preferred_element_type=F32)
             for n in range(tn // MERGE_BLOCK_DIM)], axis=1)
        term = jax.nn.sigmoid(gate + bm_ref[br:br + 1, :]) * proj
        merged = term if merged is None else merged + term
    o_ref[...] = merged.astype(o_ref.dtype)


def _merge(ys, h, wb, wm, bm, tm=1024, tn=512):
    t = h.shape[0]
    tm = min(tm, t)
    y_spec = pl.BlockSpec((tm, BRANCH_WIDTH), lambda i, j: (i, 0))
    nmb = tn // MERGE_BLOCK_DIM
    return pl.pallas_call(
        _merge_kernel,
        out_shape=jax.ShapeDtypeStruct((t, D_MODEL), BF16),
        grid=(t // tm, D_MODEL // tn),
        in_specs=[y_spec, y_spec, y_spec, y_spec,
                  pl.BlockSpec((tm, tn), lambda i, j: (i, j)),
                  pl.BlockSpec((4, BRANCH_WIDTH, tn), lambda i, j: (0, 0, j)),
                  pl.BlockSpec((4, nmb, MERGE_BLOCK_DIM, MERGE_BLOCK_DIM), lambda i, j: (0, j, 0, 0)),
                  pl.BlockSpec((4, tn), lambda i, j: (0, j))],
        out_specs=pl.BlockSpec((tm, tn), lambda i, j: (i, j)),
        compiler_params=_cparams(("parallel", "parallel")),
        name="gated_merge",
    )(*ys, h, wb, wm, bm)


def kernel(x, norm_w, w_in, fox_f_bias, conv_w, hgrn_gamma, hgrn_norm_w, w_branch, w_merge, b_merge, w_out, final_norm_w):
    batch, seq, d = x.shape
    depth = w_in.shape[0]
    assert d == D_MODEL and w_in.shape[-1] == IN_WIDTH and seq % FOX_TK == 0
    xf = x.reshape(batch * seq, d)
    w_in_p = _pad_in_proj(w_in)
    wb = w_branch.astype(BF16)
    wm = w_merge.astype(BF16)
    wo = w_out.astype(BF16)
    lower = _lower_bounds(hgrn_gamma)
    tabs = _rope_tables(seq)
    for layer in range(depth):
        h = _rmsnorm(xf, norm_w[layer], BF16)
        u32, u16 = _in_proj(h, w_in_p[layer])
        qa, ka, iq, ik, iw = _dsa_prep(u32, tabs, seq)
        y_a = _dsa(iq, iw, ik, qa, ka, u16, u32, batch, seq)
        y_b = _conv(u32, conv_w[layer].astype(F32), seq)
        ccol, crow = _fox_cum(u32, fox_f_bias[layer], batch, seq)
        y_c = _fox(u16, u32, ccol, crow, batch, seq)
        y_d = _hgrn(u32, lower[layer], hgrn_norm_w[layer], batch, seq)
        merged = _merge((y_a, y_b, y_c, y_d), h, wb[layer], wm[layer], b_merge[layer].astype(F32))
        xf = _out_proj(merged, wo[layer], xf)
    return _rmsnorm(xf, final_norm_w, F32).reshape(batch, seq, d)
```

```python
import functools

import numpy as np
import jax
import jax.numpy as jnp
from jax import lax
from jax.experimental import pallas as pl
from jax.experimental.pallas import tpu as pltpu

F32 = jnp.float32
BF16 = jnp.bfloat16

D_MODEL = 4096
HEAD_DIM = 128
ROPE_THETA = 10000.0
NORM_EPS = 1e-6
Q_BLOCK = 128
BRANCH_WIDTH = D_MODEL // 4
N_HEADS = BRANCH_WIDTH // HEAD_DIM
IDX_HEADS = 16
IDX_DIM = 64
DSA_TOPK_MAX = 256
CONV_K = 3
HGRN_CHUNK = 64
HGRN_SUB = 16
HGRN_HEADS_PER_STEP = 4
MERGE_BLOCKS = 16
MERGE_BLOCK_DIM = D_MODEL // MERGE_BLOCKS

LANES = 128
VMEM_LIMIT = 56 * 1024 * 1024

_SRC_LAYOUT = (
    ("a_q", 1024), ("a_k", 128), ("a_v", 128), ("a_iq", 1024), ("a_ik", 64), ("a_iw", 16), ("a_g", 1024),
    ("b_b", 1024), ("b_c", 1024), ("b_x", 1024), ("b_g", 1024),
    ("c_q", 1024), ("c_k", 1024), ("c_v", 1024), ("c_f", 8), ("c_g", 1024),
    ("d_q", 1024), ("d_f", 1024), ("d_i", 1024), ("d_g", 1024),
)
_SRC_OFF = {}
_o = 0
for _n, _w in _SRC_LAYOUT:
    _SRC_OFF[_n] = (_o, _w)
    _o += _w
IN_WIDTH = _o

_BIG = ("a_q", "a_iq", "a_g", "b_b", "b_c", "b_x", "b_g", "c_q", "c_k", "c_v", "c_g", "d_q", "d_f", "d_i", "d_g")
BIG = {n: i for i, n in enumerate(_BIG)}
SMALL_BASE = len(_BIG) * 1024 // LANES
SLOT_AK, SLOT_AV, SLOT_AIKW, SLOT_CF = SMALL_BASE, SMALL_BASE + 1, SMALL_BASE + 2, SMALL_BASE + 3
PAD_WIDTH = (SMALL_BASE + 4) * LANES

INT_MIN = np.int32(-2 ** 31)
NT_DIMS = (((1,), (1,)), ((), ()))
TN_DIMS = (((0,), (0,)), ((), ()))


def _cparams(sem):
    return pltpu.CompilerParams(dimension_semantics=sem, vmem_limit_bytes=VMEM_LIMIT)


def _pad_in_proj(w_in):
    def seg(n):
        o, w = _SRC_OFF[n]
        return w_in[..., o:o + w]

    def zeros(w):
        return jnp.zeros(w_in.shape[:-1] + (w,), w_in.dtype)

    parts = [seg(n) for n in _BIG]
    parts += [seg("a_k"), seg("a_v"), seg("a_ik"), seg("a_iw"), zeros(LANES - IDX_DIM - IDX_HEADS),
              seg("c_f"), zeros(LANES - N_HEADS)]
    return jnp.concatenate(parts, axis=-1).astype(BF16)


def _lb_kernel(g_ref, o_ref):
    g = g_ref[...]
    e = jnp.exp(g - jnp.max(g, axis=0, keepdims=True))
    sm = e / jnp.sum(e, axis=0, keepdims=True)
    acc = jnp.zeros_like(sm[0:1])
    rows = []
    for layer in range(g.shape[0]):
        acc = acc + sm[layer:layer + 1]
        rows.append(acc - sm[0:1])
    o_ref[...] = jnp.concatenate(rows, axis=0)


def _lower_bounds(gamma):
    return pl.pallas_call(
        _lb_kernel, out_shape=jax.ShapeDtypeStruct(gamma.shape, F32), name="hgrn_lower_bounds",
    )(gamma.astype(F32))


def _rmsnorm_kernel(x_ref, w_ref, o_ref):
    x = x_ref[...]
    ms = jnp.mean(x * x, axis=-1, keepdims=True)
    o_ref[...] = (x * lax.rsqrt(ms + NORM_EPS) * w_ref[...]).astype(o_ref.dtype)


def _rmsnorm(x, w, out_dtype, tm=256):
    t, d = x.shape
    return pl.pallas_call(
        _rmsnorm_kernel,
        out_shape=jax.ShapeDtypeStruct((t, d), out_dtype),
        grid=(t // tm,),
        in_specs=[pl.BlockSpec((tm, d), lambda i: (i, 0)), pl.BlockSpec((1, d), lambda i: (0, 0))],
        out_specs=pl.BlockSpec((tm, d), lambda i: (i, 0)),
        compiler_params=_cparams(("parallel",)),
        name="rmsnorm",
    )(x, w.reshape(1, d))


def _in_proj_kernel(a_ref, b_ref, o32_ref, o16_ref):
    acc = jnp.dot(a_ref[...], b_ref[...], preferred_element_type=F32)
    o32_ref[...] = acc
    o16_ref[...] = acc.astype(BF16)


def _in_proj(h, w, tm=1024, tn=512):
    t, k = h.shape
    n = w.shape[1]
    tm = min(tm, t)
    return pl.pallas_call(
        _in_proj_kernel,
        out_shape=(jax.ShapeDtypeStruct((t, n), F32), jax.ShapeDtypeStruct((t, n), BF16)),
        grid=(t // tm, n // tn),
        in_specs=[pl.BlockSpec((tm, k), lambda i, j: (i, 0)), pl.BlockSpec((k, tn), lambda i, j: (0, j))],
        out_specs=(pl.BlockSpec((tm, tn), lambda i, j: (i, j)), pl.BlockSpec((tm, tn), lambda i, j: (i, j))),
        compiler_params=_cparams(("parallel", "parallel")),
        name="in_proj",
    )(h, w)


def _out_proj_kernel(a_ref, b_ref, r_ref, o_ref):
    o_ref[...] = r_ref[...] + jnp.dot(a_ref[...], b_ref[...], preferred_element_type=F32)


def _out_proj(a, w, resid, tm=1024, tn=512):
    t, k = a.shape
    n = w.shape[1]
    tm = min(tm, t)
    return pl.pallas_call(
        _out_proj_kernel,
        out_shape=jax.ShapeDtypeStruct((t, n), F32),
        grid=(t // tm, n // tn),
        in_specs=[pl.BlockSpec((tm, k), lambda i, j: (i, 0)), pl.BlockSpec((k, tn), lambda i, j: (0, j)),
                  pl.BlockSpec((tm, tn), lambda i, j: (i, j))],
        out_specs=pl.BlockSpec((tm, tn), lambda i, j: (i, j)),
        compiler_params=_cparams(("parallel", "parallel")),
        name="out_proj",
    )(a, w, resid)


def _rope_tables(seq):
    pos = jnp.arange(seq).astype(F32)

    def tables(dim):
        half = dim // 2
        inv = ROPE_THETA ** (-jnp.arange(half, dtype=F32) / half)
        ang = pos[:, None] * inv[None, :]
        cos, sin = jnp.cos(ang), jnp.sin(ang)
        reps = LANES // dim
        cos_t = jnp.tile(jnp.concatenate([cos, cos], axis=1), (1, reps))
        sin_t = jnp.tile(jnp.concatenate([-sin, sin], axis=1), (1, reps))
        return cos_t, sin_t

    return tables(HEAD_DIM) + tables(IDX_DIM)


def _dsa_prep_kernel(q_ref, iq_ref, k_ref, v_ref, ikw_ref, cos_ref, sin_ref, cosi_ref, sini_ref,
                     qo_ref, ko_ref, vto_ref, iqo_ref, iko_ref, iwo_ref):
    cos, sin = cos_ref[...], sin_ref[...]
    cosi, sini = cosi_ref[...], sini_ref[...]
    tm = cos.shape[0]

    def rot_head(x):
        return x * cos + pltpu.roll(x, HEAD_DIM // 2, axis=1) * sin

    lane = lax.broadcasted_iota(jnp.int32, (tm, LANES), 1)
    first_half = (lane % IDX_DIM) < (IDX_DIM // 2)

    def rot_idx(x):
        partner = jnp.where(first_half, pltpu.roll(x, LANES - IDX_DIM // 2, axis=1),
                            pltpu.roll(x, IDX_DIM // 2, axis=1))
        return x * cosi + partner * sini

    for h in range(N_HEADS):
        qo_ref[:, h * HEAD_DIM:(h + 1) * HEAD_DIM] = rot_head(q_ref[:, h * HEAD_DIM:(h + 1) * HEAD_DIM]).astype(BF16)
    ko_ref[...] = rot_head(k_ref[...]).astype(BF16)
    vto_ref[...] = v_ref[...].T.astype(BF16)
    for p in range(IDX_HEADS // 2):
        r = rot_idx(iq_ref[:, p * LANES:(p + 1) * LANES]).astype(BF16)
        iqo_ref[2 * p] = r[:, :IDX_DIM]
        iqo_ref[2 * p + 1] = r[:, IDX_DIM:]
    ikw = ikw_ref[...]
    iko_ref[...] = rot_idx(ikw)[:, :IDX_DIM].astype(BF16)
    iwo_ref[...] = ikw.T[IDX_DIM:IDX_DIM + IDX_HEADS, :] * (IDX_HEADS ** -0.5 * IDX_DIM ** -0.5)


def _dsa_prep(u32, tabs, seq):
    t = u32.shape[0]
    tm = DSA_TK
    nseq = seq // tm
    cos_a, sin_a, cos_i, sin_i = tabs
    tab_spec = pl.BlockSpec((tm, LANES), lambda i: (i % nseq, 0))
    return pl.pallas_call(
        _dsa_prep_kernel,
        out_shape=(jax.ShapeDtypeStruct((t, BRANCH_WIDTH), BF16),
                   jax.ShapeDtypeStruct((t, HEAD_DIM), BF16),
                   jax.ShapeDtypeStruct((t // tm, HEAD_DIM, tm), BF16),
                   jax.ShapeDtypeStruct((IDX_HEADS, t, IDX_DIM), BF16),
                   jax.ShapeDtypeStruct((t, IDX_DIM), BF16),
                   jax.ShapeDtypeStruct((IDX_HEADS, t), F32)),
        grid=(t // tm,),
        in_specs=[pl.BlockSpec((tm, 1024), lambda i: (i, BIG["a_q"])),
                  pl.BlockSpec((tm, 1024), lambda i: (i, BIG["a_iq"])),
                  pl.BlockSpec((tm, LANES), lambda i: (i, SLOT_AK)),
                  pl.BlockSpec((tm, LANES), lambda i: (i, SLOT_AV)),
                  pl.BlockSpec((tm, LANES), lambda i: (i, SLOT_AIKW)),
                  tab_spec, tab_spec, tab_spec, tab_spec],
        out_specs=(pl.BlockSpec((tm, BRANCH_WIDTH), lambda i: (i, 0)),
                   pl.BlockSpec((tm, HEAD_DIM), lambda i: (i, 0)),
                   pl.BlockSpec((None, HEAD_DIM, tm), lambda i: (i, 0, 0)),
                   pl.BlockSpec((IDX_HEADS, tm, IDX_DIM), lambda i: (0, i, 0)),
                   pl.BlockSpec((tm, IDX_DIM), lambda i: (i, 0)),
                   pl.BlockSpec((IDX_HEADS, tm), lambda i: (0, i))),
        compiler_params=_cparams(("parallel",)),
        name="dsa_prep",
    )(u32, u32, u32, u32, u32, cos_a, sin_a, cos_i, sin_i)


DSA_TK = 256


def _bit_transpose32(words):
    a = list(words)
    j, m = 16, 0x0000FFFF
    while j:
        k = 0
        while k < 32:
            t = (a[k] ^ lax.shift_right_logical(a[k + j], jnp.int32(j))) & jnp.int32(m)
            a[k] = a[k] ^ t
            a[k + j] = a[k + j] ^ (t << j)
            k = (k + j + 1) & ~j
        j >>= 1
        m = (m ^ (m << j)) & 0xFFFFFFFF if j else m
    return a


def _dsa_kernel(iq_ref, iw_ref, ik_ref, q_ref, k_ref, vt_ref, g_ref, o_ref,
                keys_ref, planes_ref, m_ref, l_ref, acc_ref, *, topk, seq):
    qb = Q_BLOCK
    tk = DSA_TK
    i = pl.program_id(1)
    q0 = i * qb
    nch = (q0 + qb + tk - 1) // tk
    prow = planes_ref.shape[1]

    @pl.when((pl.program_id(0) == 0) & (i == 0))
    def _():
        planes_ref[...] = jnp.zeros(planes_ref.shape, jnp.int32)

    iw = iw_ref[...]
    iq2 = iq_ref[...].reshape(IDX_HEADS * qb, IDX_DIM)
    krow = lax.broadcasted_iota(jnp.int32, (tk, qb), 0)
    qcol = q0 + lax.broadcasted_iota(jnp.int32, (tk, qb), 1)

    def score_chunk(c, carry):
        k0 = pl.multiple_of(c * tk, tk)
        logit = lax.dot_general(ik_ref[pl.ds(k0, tk), :], iq2, NT_DIMS, preferred_element_type=F32)
        s = jnp.zeros((tk, qb), F32)
        for h in range(IDX_HEADS):
            s = s + iw[h:h + 1, :] * jnp.maximum(logit[:, h * qb:(h + 1) * qb], 0.0)
        bits = lax.bitcast_convert_type(s, jnp.int32)
        key = bits ^ ((bits >> 31) & jnp.int32(0x7FFFFFFF))
        key = jnp.where(k0 + krow <= qcol, key, INT_MIN)
        keys_ref[c] = key
        ukey = key ^ INT_MIN
        planes = _bit_transpose32([ukey[8 * j:8 * j + 8, :] for j in range(32)])
        r0 = pl.multiple_of(c * 8, 8)
        for b in range(32):
            planes_ref[b, pl.ds(r0, 8), :] = planes[b]
        return carry

    lax.fori_loop(0, nch, score_chunk, 0)

    live0 = jnp.where(lax.broadcasted_iota(jnp.int32, (prow, qb), 0) < 8 * nch, jnp.int32(-1), jnp.int32(0))
    zero = jnp.zeros((1, qb), jnp.int32)

    def popcount_rows(x):
        cnt = lax.population_count(x)
        return jnp.sum(jnp.sum(cnt.reshape(prow // 8, 8, qb), axis=0), axis=0, keepdims=True)

    def select_bit(b, carry):
        live, n_above, prefix = carry
        plane = planes_ref[b]
        ones = live & plane
        c1 = popcount_rows(ones)
        take = n_above + c1 >= topk
        n_above = jnp.where(take, n_above, n_above + c1)
        live = jnp.where(take, ones, live & ~plane)
        prefix = jnp.where(take, prefix | jnp.left_shift(jnp.int32(1), 31 - b), prefix)
        return live, n_above, prefix

    live, n_gt, prefix = lax.fori_loop(0, 32, select_bit, (live0, zero, zero))
    tau = prefix ^ INT_MIN
    n_eq = popcount_rows(live)
    need = topk - n_gt

    def count(pred):
        def body(c, acc):
            hit = jnp.where(pred(keys_ref[c], c * tk + krow), 1, 0)
            return acc + jnp.sum(hit.reshape(tk // 8, 8, qb), axis=0)

        acc = lax.fori_loop(0, nch, body, jnp.zeros((8, qb), jnp.int32))
        return jnp.sum(acc, axis=0, keepdims=True)

    def tie_search(_):
        def body(it, p):
            cand = p + jnp.left_shift(jnp.int32(1), 14 - it)
            below = count(lambda k, idx: (k == tau) & (idx < cand))
            return jnp.where(below < need, cand, p)
        return lax.fori_loop(0, 15, body, zero) + 1

    has_tie = jnp.max(jnp.where((n_eq > need) & (tau > INT_MIN), 1, 0)) > 0
    jlim = lax.cond(has_tie, tie_search, lambda _: jnp.full((1, qb), seq, jnp.int32), 0)

    q = q_ref[...]
    q2 = jnp.concatenate([q[:, h * HEAD_DIM:(h + 1) * HEAD_DIM] for h in range(N_HEADS)], axis=0)
    m_ref[...] = jnp.full(m_ref.shape, -jnp.inf, F32)
    l_ref[...] = jnp.zeros(l_ref.shape, F32)
    acc_ref[...] = jnp.zeros(acc_ref.shape, F32)
    exp2_scale = HEAD_DIM ** -0.5 * np.log2(np.e)

    def attend(c, carry):
        k0 = pl.multiple_of(c * tk, tk)
        kc = keys_ref[c]
        sel = (kc > INT_MIN) & ((kc > tau) | ((kc == tau) & (c * tk + krow < jlim)))
        bias = jnp.where(sel, 0.0, -jnp.inf)
        s = lax.dot_general(k_ref[pl.ds(k0, tk), :], q2, NT_DIMS, preferred_element_type=F32)
        m_prev, l_prev = m_ref[...], l_ref[...]
        ps, ms, ls, alphas = [], [], [], []
        for h in range(N_HEADS):
            cols = slice(h * qb, (h + 1) * qb)
            z = s[:, cols] + bias
            m_new = jnp.maximum(m_prev[:, cols], jnp.max(z, axis=0, keepdims=True))
            m_safe = jnp.where(m_new == -jnp.inf, 0.0, m_new)
            p = jnp.exp2((z - m_safe) * exp2_scale)
            alpha = jnp.exp2((m_prev[:, cols] - m_safe) * exp2_scale)
            ps.append(p.astype(BF16))
            ms.append(m_new)
            alphas.append(alpha)
            ls.append(alpha * l_prev[:, cols] + jnp.sum(p, axis=0, keepdims=True))
        m_ref[...] = jnp.concatenate(ms, axis=1)
        l_ref[...] = jnp.concatenate(ls, axis=1)
        acc_ref[...] = jnp.concatenate(alphas, axis=1) * acc_ref[...] + jnp.dot(
            vt_ref[c], jnp.concatenate(ps, axis=1), preferred_element_type=F32)
        return carry

    lax.fori_loop(0, nch, attend, 0)
    ot = acc_ref[...] / l_ref[...]
    o = jnp.concatenate([ot[:, h * qb:(h + 1) * qb].T for h in range(N_HEADS)], axis=1)
    g = g_ref[...]
    o_ref[...] = (o * (g * jax.nn.sigmoid(g))).astype(o_ref.dtype)


def _dsa(iq, iwt, ik, qa, ka, vt, u32, batch, seq):
    t = qa.shape[0]
    nb = seq // Q_BLOCK
    topk = min(DSA_TOPK_MAX, seq // 4)
    kern = functools.partial(_dsa_kernel, topk=topk, seq=seq)
    return pl.pallas_call(
        kern,
        out_shape=jax.ShapeDtypeStruct((t, BRANCH_WIDTH), BF16),
        grid=(batch, nb),
        in_specs=[pl.BlockSpec((IDX_HEADS, Q_BLOCK, IDX_DIM), lambda b, i: (0, b * nb + i, 0)),
                  pl.BlockSpec((IDX_HEADS, Q_BLOCK), lambda b, i: (0, b * nb + i)),
                  pl.BlockSpec((seq, IDX_DIM), lambda b, i: (b, 0)),
                  pl.BlockSpec((Q_BLOCK, BRANCH_WIDTH), lambda b, i: (b * nb + i, 0)),
                  pl.BlockSpec((seq, HEAD_DIM), lambda b, i: (b, 0)),
                  pl.BlockSpec((seq // DSA_TK, HEAD_DIM, DSA_TK), lambda b, i: (b, 0, 0)),
                  pl.BlockSpec((Q_BLOCK, 1024), lambda b, i: (b * nb + i, BIG["a_g"]))],
        out_specs=pl.BlockSpec((Q_BLOCK, BRANCH_WIDTH), lambda b, i: (b * nb + i, 0)),
        scratch_shapes=[pltpu.VMEM((seq // DSA_TK, DSA_TK, Q_BLOCK), jnp.int32),
                        pltpu.VMEM((32, seq // DSA_TK * 8, Q_BLOCK), jnp.int32),
                        pltpu.VMEM((1, N_HEADS * Q_BLOCK), F32),
                        pltpu.VMEM((1, N_HEADS * Q_BLOCK), F32),
                        pltpu.VMEM((HEAD_DIM, N_HEADS * Q_BLOCK), F32)],
        compiler_params=_cparams(("arbitrary", "arbitrary")),
        name="dsa_attention",
    )(iq, iwt, ik, qa, ka, vt, u32)


def _conv_kernel(b_ref, c_ref, x_ref, g_ref, cp_ref, xp_ref, w_ref, o_ref, *, tiles_per_seq):
    i = pl.program_id(0)
    tm = b_ref.shape[0]
    u = c_ref[...] * x_ref[...]
    prev = cp_ref[...] * xp_ref[...]
    prev = jnp.where(i % tiles_per_seq == 0, 0.0, prev)
    row = lax.broadcasted_iota(jnp.int32, u.shape, 0)
    u1 = jnp.where(row == 0, prev[7:8, :], pltpu.roll(u, 1, axis=0))
    u2 = jnp.where(row == 0, prev[6:7, :], jnp.where(row == 1, prev[7:8, :], pltpu.roll(u, 2, axis=0)))
    w = w_ref[...]
    y = w[0:1, :] * u2 + w[1:2, :] * u1 + w[2:3, :] * u
    g = g_ref[...]
    o_ref[...] = (b_ref[...] * y * (g * jax.nn.sigmoid(g))).astype(o_ref.dtype)
    del tm


def _conv(u32, conv_w, seq, tm=512):
    t = u32.shape[0]
    tps = seq // tm
    kern = functools.partial(_conv_kernel, tiles_per_seq=tps)

    def big(name):
        return pl.BlockSpec((tm, 1024), lambda i: (i, BIG[name]))

    def halo(name):
        return pl.BlockSpec((8, 1024), lambda i: (jnp.maximum(i * (tm // 8) - 1, 0), BIG[name]))

    return pl.pallas_call(
        kern,
        out_shape=jax.ShapeDtypeStruct((t, BRANCH_WIDTH), BF16),
        grid=(t // tm,),
        in_specs=[big("b_b"), big("b_c"), big("b_x"), big("b_g"), halo("b_c"), halo("b_x"),
                  pl.BlockSpec((CONV_K, BRANCH_WIDTH), lambda i: (0, 0))],
        out_specs=pl.BlockSpec((tm, BRANCH_WIDTH), lambda i: (i, 0)),
        compiler_params=_cparams(("parallel",)),
        name="short_conv",
    )(u32, u32, u32, u32, u32, u32, conv_w)


FOX_TK = 256
FOX_TQ = 1024
FOX_TC = 1024


def _fox_cum_kernel(cf_ref, bias_ref, ccol_ref, crow_ref, carry_ref):
    j = pl.program_id(1)

    @pl.when(j == 0)
    def _():
        carry_ref[...] = jnp.zeros_like(carry_ref)

    x = cf_ref[...] + bias_ref[...]
    log_f = jnp.minimum(x, 0.0) - jnp.log1p(jnp.exp(-jnp.abs(x)))
    tc = x.shape[0]
    tri = (lax.broadcasted_iota(jnp.int32, (tc, tc), 1) <= lax.broadcasted_iota(jnp.int32, (tc, tc), 0)).astype(F32)
    cs = jnp.dot(tri, log_f, precision=lax.Precision.HIGHEST, preferred_element_type=F32) + carry_ref[...]
    ccol_ref[...] = cs
    carry_ref[...] = cs[tc - 1:tc, :]
    cst = cs.T
    for part in range(tc // FOX_TQ):
        crow_ref[part] = cst[0:N_HEADS, part * FOX_TQ:(part + 1) * FOX_TQ]


def _fox_cum(u32, bias, batch, seq):
    t = u32.shape[0]
    tc = FOX_TC
    ns = seq // tc
    per = tc // FOX_TQ
    bias_p = jnp.zeros((1, LANES), F32).at[0, :N_HEADS].set(bias.astype(F32))
    return pl.pallas_call(
        _fox_cum_kernel,
        out_shape=(jax.ShapeDtypeStruct((t, LANES), F32),
                   jax.ShapeDtypeStruct((t // FOX_TQ, N_HEADS, FOX_TQ), F32)),
        grid=(batch, ns),
        in_specs=[pl.BlockSpec((tc, LANES), lambda b, j: (b * ns + j, SLOT_CF)),
                  pl.BlockSpec((1, LANES), lambda b, j: (0, 0))],
        out_specs=(pl.BlockSpec((tc, LANES), lambda b, j: (b * ns + j, 0)),
                   pl.BlockSpec((per, N_HEADS, FOX_TQ), lambda b, j: (b * ns + j, 0, 0))),
        scratch_shapes=[pltpu.VMEM((1, LANES), F32)],
        compiler_params=_cparams(("parallel", "arbitrary")),
        name="fox_cumsum",
    )(u32, bias_p)


def _fox_kernel(q_ref, k_ref, v_ref, ccol_ref, crow_ref, g_ref, o_ref, ckb_ref, m_ref, l_ref, acc_ref, *, seq):
    tq, tk, tc = FOX_TQ, FOX_TK, FOX_TC
    h = pl.program_id(1)
    q0 = pl.program_id(2) * tq
    scale = HEAD_DIM ** -0.5
    log2e = np.log2(np.e)

    @pl.when(pl.program_id(2) == 0)
    def _():
        lane = lax.broadcasted_iota(jnp.int32, (tc, LANES), 1)

        def fill(r, carry):
            r0 = pl.multiple_of(r * tc, tc)
            col = jnp.sum(jnp.where(lane == h, ccol_ref[pl.ds(r0, tc), :], 0.0), axis=1, keepdims=True)
            ckb_ref[pl.ds(r0, tc), :] = jnp.broadcast_to(col, (tc, LANES))
            return carry

        lax.fori_loop(0, seq // tc, fill, 0)

    q = q_ref[...]
    sub = lax.broadcasted_iota(jnp.int32, (N_HEADS, tq), 0)
    cq = jnp.sum(jnp.where(sub == h, crow_ref[...], 0.0), axis=0, keepdims=True)
    m_ref[...] = jnp.full(m_ref.shape, -jnp.inf, F32)
    l_ref[...] = jnp.zeros(l_ref.shape, F32)
    acc_ref[...] = jnp.zeros(acc_ref.shape, F32)
    krow = lax.broadcasted_iota(jnp.int32, (tk, LANES), 0)
    qlane = lax.broadcasted_iota(jnp.int32, (tk, LANES), 1)

    def make_body(masked):
        def body(c, carry):
            k0 = pl.multiple_of(c * tk, tk)
            s = lax.dot_general(k_ref[pl.ds(k0, tk), :], q, NT_DIMS, preferred_element_type=F32)
            ckb = ckb_ref[pl.ds(k0, tk), :]
            m_prev, l_prev = m_ref[...], l_ref[...]
            ps, ms, ls, alphas = [], [], [], []
            for j in range(tq // LANES):
                cols = slice(j * LANES, (j + 1) * LANES)
                z = s[:, cols] * scale + cq[:, cols] - ckb
                if masked:
                    z = jnp.where(k0 + krow <= q0 + j * LANES + qlane, z, -jnp.inf)
                m_new = jnp.maximum(m_prev[:, cols], jnp.max(z, axis=0, keepdims=True))
                p = jnp.exp2((z - m_new) * log2e)
                alpha = jnp.exp2((m_prev[:, cols] - m_new) * log2e)
                ps.append(p.astype(BF16))
                ms.append(m_new)
                alphas.append(alpha)
                ls.append(alpha * l_prev[:, cols] + jnp.sum(p, axis=0, keepdims=True))
            m_ref[...] = jnp.concatenate(ms, axis=1)
            l_ref[...] = jnp.concatenate(ls, axis=1)
            acc_ref[...] = jnp.concatenate(alphas, axis=1) * acc_ref[...] + lax.dot_general(
                v_ref[pl.ds(k0, tk), :], jnp.concatenate(ps, axis=1), TN_DIMS, preferred_element_type=F32)
            return carry
        return body

    n_full = q0 // tk
    n_all = (q0 + tq + tk - 1) // tk
    lax.fori_loop(0, n_full, make_body(False), 0)
    lax.fori_loop(n_full, n_all, make_body(True), 0)
    g = g_ref[...]
    o_ref[...] = ((acc_ref[...] / l_ref[...]).T * (g * jax.nn.sigmoid(g))).astype(o_ref.dtype)


def _fox(u16, u32, ccol, crow, batch, seq):
    t = u16.shape[0]
    tq = FOX_TQ
    nq = seq // tq
    cq, ck, cv, cg = (BIG[n] * 1024 // LANES for n in ("c_q", "c_k", "c_v", "c_g"))
    kern = functools.partial(_fox_kernel, seq=seq)
    return pl.pallas_call(
        kern,
        out_shape=jax.ShapeDtypeStruct((t, BRANCH_WIDTH), BF16),
        grid=(batch, N_HEADS, nq),
        in_specs=[pl.BlockSpec((tq, HEAD_DIM), lambda b, h, i: (b * nq + i, cq + h)),
                  pl.BlockSpec((seq, HEAD_DIM), lambda b, h, i: (b, ck + h)),
                  pl.BlockSpec((seq, HEAD_DIM), lambda b, h, i: (b, cv + h)),
                  pl.BlockSpec((seq, LANES), lambda b, h, i: (b, 0)),
                  pl.BlockSpec((None, N_HEADS, tq), lambda b, h, i: (b * nq + i, 0, 0)),
                  pl.BlockSpec((tq, HEAD_DIM), lambda b, h, i: (b * nq + i, cg + h))],
        out_specs=pl.BlockSpec((tq, HEAD_DIM), lambda b, h, i: (b * nq + i, h)),
        scratch_shapes=[pltpu.VMEM((seq, LANES), F32),
                        pltpu.VMEM((1, tq), F32), pltpu.VMEM((1, tq), F32), pltpu.VMEM((HEAD_DIM, tq), F32)],
        compiler_params=_cparams(("parallel", "parallel", "arbitrary")),
        name="fox_attention",
    )(u16, u16, u16, ccol, crow, u32)


def _hgrn_kernel(q_ref, f_ref, i_ref, g_ref, lb_ref, nw_ref, o_ref, st_ref):
    cs, sub = HGRN_CHUNK, HGRN_SUB
    ts = q_ref.shape[0]

    @pl.when(pl.program_id(2) == 0)
    def _():
        st_ref[...] = jnp.zeros_like(st_ref)

    tri = (lax.broadcasted_iota(jnp.int32, (cs, cs), 1) <= lax.broadcasted_iota(jnp.int32, (cs, cs), 0)).astype(F32)
    row_c = lax.broadcasted_iota(jnp.int32, (cs, HEAD_DIM), 0)
    row_s = lax.broadcasted_iota(jnp.int32, (sub, HEAD_DIM), 0)
    lane_s = lax.broadcasted_iota(jnp.int32, (sub, cs), 1)

    def chunk(ci, carry):
        for hh in range(HGRN_HEADS_PER_STEP):
            head_chunk(pl.multiple_of(ci * cs, cs), hh)
        return carry

    def head_chunk(r0, hh):
        cols = slice(hh * HEAD_DIM, (hh + 1) * HEAD_DIM)
        lb = lb_ref[:, cols]
        nw = nw_ref[:, cols]
        f = lb + (1.0 - lb) * jax.nn.sigmoid(f_ref[pl.ds(r0, cs), cols])
        kk = 1.0 - f
        bc = jnp.dot(tri, jnp.log(f), precision=lax.Precision.HIGHEST, preferred_element_type=F32)
        q = q_ref[pl.ds(r0, cs), cols]
        vb = i_ref[pl.ds(r0, cs), cols].astype(BF16)
        st = st_ref[hh]
        o = lax.dot_general((q * jnp.exp(bc)).astype(BF16), st.astype(BF16), NT_DIMS, preferred_element_type=F32)
        att_rows = []
        for si in range(cs // sub):
            lo = si * sub
            bi = bc[lo:lo + sub, :]
            qi = q[lo:lo + sub, :]
            att = jnp.zeros((sub, cs), F32)
            if si > 0:
                b0 = bc[lo - 1:lo, :]
                kt = kk * jnp.exp(jnp.where(row_c < lo, b0 - bc, -jnp.inf))
                qs = qi * jnp.exp(bi - b0)
                att = lax.dot_general(qs.astype(BF16), kt.astype(BF16), NT_DIMS, preferred_element_type=F32)
            for s in range(sub):
                bs = bc[lo + s:lo + s + 1, :]
                ks = kk[lo + s:lo + s + 1, :]
                p = qi * ks * jnp.exp(jnp.where(row_s >= s, bi - bs, -jnp.inf))
                att = att + jnp.where(lane_s == lo + s, jnp.sum(p, axis=1, keepdims=True), 0.0)
            att_rows.append(att)
        att = jnp.concatenate(att_rows, axis=0)
        o = o + jnp.dot(att.astype(BF16), vb, preferred_element_type=F32)
        bl = bc[cs - 1:cs, :]
        kd = kk * jnp.exp(bl - bc)
        st_ref[hh] = st * jnp.exp(bl) + lax.dot_general(vb, kd.astype(BF16), TN_DIMS, preferred_element_type=F32)
        y = o * lax.rsqrt(jnp.mean(o * o, axis=1, keepdims=True) + NORM_EPS) * nw
        g = g_ref[pl.ds(r0, cs), cols]
        o_ref[pl.ds(r0, cs), cols] = (y * (g * jax.nn.sigmoid(g))).astype(o_ref.dtype)

    lax.fori_loop(0, ts // cs, chunk, 0)


def _hgrn(u32, lb, norm_w, batch, seq, ts=512):
    t = u32.shape[0]
    ts = min(ts, seq)
    ns = seq // ts
    hp = HGRN_HEADS_PER_STEP
    wide = hp * HEAD_DIM
    dq, df, di, dg = (BIG[n] * 1024 // wide for n in ("d_q", "d_f", "d_i", "d_g"))

    def col(c0):
        return pl.BlockSpec((ts, wide), lambda b, h, j: (b * ns + j, c0 + h))

    vec = pl.BlockSpec((1, wide), lambda b, h, j: (0, h))
    return pl.pallas_call(
        _hgrn_kernel,
        out_shape=jax.ShapeDtypeStruct((t, BRANCH_WIDTH), BF16),
        grid=(batch, N_HEADS // hp, ns),
        in_specs=[col(dq), col(df), col(di), col(dg), vec, vec],
        out_specs=pl.BlockSpec((ts, wide), lambda b, h, j: (b * ns + j, h)),
        scratch_shapes=[pltpu.VMEM((hp, HEAD_DIM, HEAD_DIM), F32)],
        compiler_params=_cparams(("parallel", "parallel", "arbitrary")),
        name="hgrn2",
    )(u32, u32, u32, u32, lb.reshape(1, BRANCH_WIDTH), norm_w.reshape(1, BRANCH_WIDTH).astype(F32))


def _merge_kernel(ya_ref, yb_ref, yc_ref, yd_ref, h_ref, wb_ref, wm_ref, bm_ref, o_ref):
    tn = o_ref.shape[1]
    h = h_ref[...]
    merged = None
    for br, y_ref in enumerate((ya_ref, yb_ref, yc_ref, yd_ref)):
        proj = jnp.dot(y_ref[...], wb_ref[br], preferred_element_type=F32)
        gate = jnp.concatenate(
            [jnp.dot(h[:, n * MERGE_BLOCK_DIM:(n + 1) * MERGE_BLOCK_DIM], wm_ref[br, n], preferred_element_type=F32)
             for n in range(tn // MERGE_BLOCK_DIM)], axis=1)
        term = jax.nn.sigmoid(gate + bm_ref[br:br + 1, :]) * proj
        merged = term if merged is None else merged + term
    o_ref[...] = merged.astype(o_ref.dtype)


def _merge(ys, h, wb, wm, bm, tm=1024, tn=512):
    t = h.shape[0]
    tm = min(tm, t)
    y_spec = pl.BlockSpec((tm, BRANCH_WIDTH), lambda i, j: (i, 0))
    nmb = tn // MERGE_BLOCK_DIM
    return pl.pallas_call(
        _merge_kernel,
        out_shape=jax.ShapeDtypeStruct((t, D_MODEL), BF16),
        grid=(t // tm, D_MODEL // tn),
        in_specs=[y_spec, y_spec, y_spec, y_spec,
                  pl.BlockSpec((tm, tn), lambda i, j: (i, j)),
                  pl.BlockSpec((4, BRANCH_WIDTH, tn), lambda i, j: (0, 0, j)),
                  pl.BlockSpec((4, nmb, MERGE_BLOCK_DIM, MERGE_BLOCK_DIM), lambda i, j: (0, j, 0, 0)),
                  pl.BlockSpec((4, tn), lambda i, j: (0, j))],
        out_specs=pl.BlockSpec((tm, tn), lambda i, j: (i, j)),
        compiler_params=_cparams(("parallel", "parallel")),
        name="gated_merge",
    )(*ys, h, wb, wm, bm)


def kernel(x, norm_w, w_in, fox_f_bias, conv_w, hgrn_gamma, hgrn_norm_w, w_branch, w_merge, b_merge, w_out, final_norm_w):
    batch, seq, d = x.shape
    depth = w_in.shape[0]
    assert d == D_MODEL and w_in.shape[-1] == IN_WIDTH and seq % FOX_TC == 0
    xf = x.reshape(batch * seq, d)
    w_in_p = _pad_in_proj(w_in)
    wb = w_branch.astype(BF16)
    wm = w_merge.astype(BF16)
    wo = w_out.astype(BF16)
    lower = _lower_bounds(hgrn_gamma)
    tabs = _rope_tables(seq)
    for layer in range(depth):
        h = _rmsnorm(xf, norm_w[layer], BF16)
        u32, u16 = _in_proj(h, w_in_p[layer])
        qa, ka, vt, iq, ik, iwt = _dsa_prep(u32, tabs, seq)
        y_a = _dsa(iq, iwt, ik, qa, ka, vt, u32, batch, seq)
        y_b = _conv(u32, conv_w[layer].astype(F32), seq)
        ccol, crow = _fox_cum(u32, fox_f_bias[layer], batch, seq)
        y_c = _fox(u16, u32, ccol, crow, batch, seq)
        y_d = _hgrn(u32, lower[layer], hgrn_norm_w[layer], batch, seq)
        merged = _merge((y_a, y_b, y_c, y_d), h, wb[layer], wm[layer], b_merge[layer].astype(F32))
        xf = _out_proj(merged, wo[layer], xf)
    return _rmsnorm(xf, final_norm_w, F32).reshape(batch, seq, d)
```

```python
import functools

import numpy as np
import jax
import jax.numpy as jnp
from jax import lax
from jax.experimental import pallas as pl
from jax.experimental.pallas import tpu as pltpu

F32 = jnp.float32
BF16 = jnp.bfloat16

D_MODEL = 4096
HEAD_DIM = 128
ROPE_THETA = 10000.0
NORM_EPS = 1e-6
Q_BLOCK = 128
BRANCH_WIDTH = D_MODEL // 4
N_HEADS = BRANCH_WIDTH // HEAD_DIM
IDX_HEADS = 16
IDX_DIM = 64
DSA_TOPK_MAX = 256
CONV_K = 3
HGRN_CHUNK = 64
HGRN_SUB = 16
HGRN_HEADS_PER_STEP = 4
MERGE_BLOCKS = 16
MERGE_BLOCK_DIM = D_MODEL // MERGE_BLOCKS

LANES = 128
VT_ROWS = HEAD_DIM + 16
VMEM_LIMIT = 56 * 1024 * 1024

_SRC_LAYOUT = (
    ("a_q", 1024), ("a_k", 128), ("a_v", 128), ("a_iq", 1024), ("a_ik", 64), ("a_iw", 16), ("a_g", 1024),
    ("b_b", 1024), ("b_c", 1024), ("b_x", 1024), ("b_g", 1024),
    ("c_q", 1024), ("c_k", 1024), ("c_v", 1024), ("c_f", 8), ("c_g", 1024),
    ("d_q", 1024), ("d_f", 1024), ("d_i", 1024), ("d_g", 1024),
)
_SRC_OFF = {}
_o = 0
for _n, _w in _SRC_LAYOUT:
    _SRC_OFF[_n] = (_o, _w)
    _o += _w
IN_WIDTH = _o

_BIG = ("a_q", "a_iq", "a_g", "b_b", "b_c", "b_x", "b_g", "c_q", "c_k", "c_v", "c_g", "d_q", "d_f", "d_i", "d_g")
BIG = {n: i for i, n in enumerate(_BIG)}
SMALL_BASE = len(_BIG) * 1024 // LANES
SLOT_AK, SLOT_AV, SLOT_AIKW, SLOT_CF = SMALL_BASE, SMALL_BASE + 1, SMALL_BASE + 2, SMALL_BASE + 3
PAD_WIDTH = (SMALL_BASE + 4) * LANES

INT_MIN = np.int32(-2 ** 31)
NT_DIMS = (((1,), (1,)), ((), ()))
TN_DIMS = (((0,), (0,)), ((), ()))


def _cparams(sem):
    return pltpu.CompilerParams(dimension_semantics=sem, vmem_limit_bytes=VMEM_LIMIT)


def _pad_in_proj(w_in):
    w_in = w_in.astype(BF16)

    def seg(n):
        o, w = _SRC_OFF[n]
        return w_in[..., o:o + w]

    def zeros(w):
        return jnp.zeros(w_in.shape[:-1] + (w,), w_in.dtype)

    parts = [seg(n) for n in _BIG]
    parts += [seg("a_k"), seg("a_v"), seg("a_ik"), seg("a_iw"), zeros(LANES - IDX_DIM - IDX_HEADS),
              seg("c_f"), zeros(LANES - N_HEADS)]
    return jnp.concatenate(parts, axis=-1)


def _lb_kernel(g_ref, o_ref):
    g = g_ref[...]
    e = jnp.exp(g - jnp.max(g, axis=0, keepdims=True))
    sm = e / jnp.sum(e, axis=0, keepdims=True)
    acc = jnp.zeros_like(sm[0:1])
    rows = []
    for layer in range(g.shape[0]):
        acc = acc + sm[layer:layer + 1]
        rows.append(acc - sm[0:1])
    o_ref[...] = jnp.concatenate(rows, axis=0)


def _lower_bounds(gamma):
    return pl.pallas_call(
        _lb_kernel, out_shape=jax.ShapeDtypeStruct(gamma.shape, F32), name="hgrn_lower_bounds",
    )(gamma.astype(F32))


def _rmsnorm_kernel(x_ref, w_ref, o_ref):
    x = x_ref[...]
    ms = jnp.mean(x * x, axis=-1, keepdims=True)
    o_ref[...] = (x * lax.rsqrt(ms + NORM_EPS) * w_ref[...]).astype(o_ref.dtype)


def _rmsnorm(x, w, out_dtype, tm=256):
    t, d = x.shape
    return pl.pallas_call(
        _rmsnorm_kernel,
        out_shape=jax.ShapeDtypeStruct((t, d), out_dtype),
        grid=(t // tm,),
        in_specs=[pl.BlockSpec((tm, d), lambda i: (i, 0)), pl.BlockSpec((1, d), lambda i: (0, 0))],
        out_specs=pl.BlockSpec((tm, d), lambda i: (i, 0)),
        compiler_params=_cparams(("parallel",)),
        name="rmsnorm",
    )(x, w.reshape(1, d))


def _in_proj_kernel(a_ref, b_ref, o_ref):
    o_ref[...] = jnp.dot(a_ref[...], b_ref[...], preferred_element_type=F32)


def _in_proj(h, w, layer, tm=1024, tn=512):
    t, k = h.shape
    n = w.shape[2]
    tm = min(tm, t)
    return pl.pallas_call(
        _in_proj_kernel,
        out_shape=jax.ShapeDtypeStruct((t, n), F32),
        grid=(t // tm, n // tn),
        in_specs=[pl.BlockSpec((tm, k), lambda i, j: (i, 0)),
                  pl.BlockSpec((None, k, tn), lambda i, j: (layer, 0, j))],
        out_specs=pl.BlockSpec((tm, tn), lambda i, j: (i, j)),
        compiler_params=_cparams(("parallel", "parallel")),
        name="in_proj",
    )(h, w)


def _out_proj_kernel(a_ref, b_ref, r_ref, o_ref):
    o_ref[...] = r_ref[...] + jnp.dot(a_ref[...], b_ref[...], preferred_element_type=F32)


def _out_proj(a, w, layer, resid, tm=1024, tn=512):
    t, k = a.shape
    n = w.shape[2]
    tm = min(tm, t)
    return pl.pallas_call(
        _out_proj_kernel,
        out_shape=jax.ShapeDtypeStruct((t, n), F32),
        grid=(t // tm, n // tn),
        in_specs=[pl.BlockSpec((tm, k), lambda i, j: (i, 0)),
                  pl.BlockSpec((None, k, tn), lambda i, j: (layer, 0, j)),
                  pl.BlockSpec((tm, tn), lambda i, j: (i, j))],
        out_specs=pl.BlockSpec((tm, tn), lambda i, j: (i, j)),
        compiler_params=_cparams(("parallel", "parallel")),
        name="out_proj",
    )(a, w, resid)


def _rope_tables(seq):
    pos = jnp.arange(seq).astype(F32)

    def tables(dim):
        half = dim // 2
        inv = ROPE_THETA ** (-jnp.arange(half, dtype=F32) / half)
        ang = pos[:, None] * inv[None, :]
        cos, sin = jnp.cos(ang), jnp.sin(ang)
        reps = LANES // dim
        cos_t = jnp.tile(jnp.concatenate([cos, cos], axis=1), (1, reps))
        sin_t = jnp.tile(jnp.concatenate([-sin, sin], axis=1), (1, reps))
        return cos_t, sin_t

    return tables(HEAD_DIM) + tables(IDX_DIM)


def _dsa_prep_kernel(q_ref, iq_ref, k_ref, v_ref, ikw_ref, cos_ref, sin_ref, cosi_ref, sini_ref,
                     qo_ref, ko_ref, vto_ref, iqo_ref, iko_ref, iwo_ref):
    cos, sin = cos_ref[...], sin_ref[...]
    cosi, sini = cosi_ref[...], sini_ref[...]
    tm = cos.shape[0]

    def rot_head(x):
        return x * cos + pltpu.roll(x, HEAD_DIM // 2, axis=1) * sin

    lane = lax.broadcasted_iota(jnp.int32, (tm, LANES), 1)
    first_half = (lane % IDX_DIM) < (IDX_DIM // 2)

    def rot_idx(x):
        partner = jnp.where(first_half, pltpu.roll(x, LANES - IDX_DIM // 2, axis=1),
                            pltpu.roll(x, IDX_DIM // 2, axis=1))
        return x * cosi + partner * sini

    for h in range(N_HEADS):
        qo_ref[:, h * HEAD_DIM:(h + 1) * HEAD_DIM] = rot_head(q_ref[:, h * HEAD_DIM:(h + 1) * HEAD_DIM]).astype(BF16)
    ko_ref[...] = rot_head(k_ref[...]).astype(BF16)
    vto_ref[0:HEAD_DIM, :] = v_ref[...].T.astype(BF16)
    vto_ref[HEAD_DIM:VT_ROWS, :] = jnp.ones((VT_ROWS - HEAD_DIM, tm), BF16)
    for p in range(IDX_HEADS // 2):
        r = rot_idx(iq_ref[:, p * LANES:(p + 1) * LANES]).astype(BF16)
        iqo_ref[2 * p] = r[:, :IDX_DIM]
        iqo_ref[2 * p + 1] = r[:, IDX_DIM:]
    ikw = ikw_ref[...]
    iko_ref[...] = rot_idx(ikw)[:, :IDX_DIM].astype(BF16)
    iwo_ref[...] = ikw.T[IDX_DIM:IDX_DIM + IDX_HEADS, :] * (IDX_HEADS ** -0.5 * IDX_DIM ** -0.5)


def _dsa_prep(u32, tabs, seq):
    t = u32.shape[0]
    tm = DSA_TK
    nseq = seq // tm
    cos_a, sin_a, cos_i, sin_i = tabs
    tab_spec = pl.BlockSpec((tm, LANES), lambda i: (i % nseq, 0))
    return pl.pallas_call(
        _dsa_prep_kernel,
        out_shape=(jax.ShapeDtypeStruct((t, BRANCH_WIDTH), BF16),
                   jax.ShapeDtypeStruct((t, HEAD_DIM), BF16),
                   jax.ShapeDtypeStruct((t // tm, VT_ROWS, tm), BF16),
                   jax.ShapeDtypeStruct((IDX_HEADS, t, IDX_DIM), BF16),
                   jax.ShapeDtypeStruct((t, IDX_DIM), BF16),
                   jax.ShapeDtypeStruct((IDX_HEADS, t), F32)),
        grid=(t // tm,),
        in_specs=[pl.BlockSpec((tm, 1024), lambda i: (i, BIG["a_q"])),
                  pl.BlockSpec((tm, 1024), lambda i: (i, BIG["a_iq"])),
                  pl.BlockSpec((tm, LANES), lambda i: (i, SLOT_AK)),
                  pl.BlockSpec((tm, LANES), lambda i: (i, SLOT_AV)),
                  pl.BlockSpec((tm, LANES), lambda i: (i, SLOT_AIKW)),
                  tab_spec, tab_spec, tab_spec, tab_spec],
        out_specs=(pl.BlockSpec((tm, BRANCH_WIDTH), lambda i: (i, 0)),
                   pl.BlockSpec((tm, HEAD_DIM), lambda i: (i, 0)),
                   pl.BlockSpec((None, VT_ROWS, tm), lambda i: (i, 0, 0)),
                   pl.BlockSpec((IDX_HEADS, tm, IDX_DIM), lambda i: (0, i, 0)),
                   pl.BlockSpec((tm, IDX_DIM), lambda i: (i, 0)),
                   pl.BlockSpec((IDX_HEADS, tm), lambda i: (0, i))),
        compiler_params=_cparams(("parallel",)),
        name="dsa_prep",
    )(u32, u32, u32, u32, u32, cos_a, sin_a, cos_i, sin_i)


DSA_TK = 256


def _bit_transpose32(words):
    a = list(words)
    j, m = 16, 0x0000FFFF
    while j:
        k = 0
        while k < 32:
            t = (a[k] ^ lax.shift_right_logical(a[k + j], jnp.int32(j))) & jnp.int32(m)
            a[k] = a[k] ^ t
            a[k + j] = a[k + j] ^ (t << j)
            k = (k + j + 1) & ~j
        j >>= 1
        m = (m ^ (m << j)) & 0xFFFFFFFF if j else m
    return a


def _dsa_kernel(iq_ref, iw_ref, ik_ref, q_ref, k_ref, vt_ref, g_ref, o_ref,
                keys_ref, planes_ref, m_ref, acc_ref, *, topk, seq):
    qb = Q_BLOCK
    tk = DSA_TK
    i = pl.program_id(1)
    q0 = i * qb
    nch = (q0 + qb + tk - 1) // tk
    prow = planes_ref.shape[1]

    @pl.when((pl.program_id(0) == 0) & (i == 0))
    def _():
        planes_ref[...] = jnp.zeros(planes_ref.shape, jnp.int32)

    iw = iw_ref[...]
    iq2 = iq_ref[...].reshape(IDX_HEADS * qb, IDX_DIM)
    krow = lax.broadcasted_iota(jnp.int32, (tk, qb), 0)
    qcol = q0 + lax.broadcasted_iota(jnp.int32, (tk, qb), 1)

    def score_chunk(c):
        k0 = pl.multiple_of(c * tk, tk)
        logit = lax.dot_general(ik_ref[pl.ds(k0, tk), :], iq2, NT_DIMS, preferred_element_type=F32)
        s = jnp.zeros((tk, qb), F32)
        for h in range(IDX_HEADS):
            s = s + iw[h:h + 1, :] * jnp.maximum(logit[:, h * qb:(h + 1) * qb], 0.0)
        bits = lax.bitcast_convert_type(s, jnp.int32)
        key = bits ^ ((bits >> 31) & jnp.int32(0x7FFFFFFF))
        key = jnp.where(k0 + krow <= qcol, key, INT_MIN)
        keys_ref[c] = key
        ukey = key ^ INT_MIN
        planes = _bit_transpose32([ukey[8 * j:8 * j + 8, :] for j in range(32)])
        r0 = pl.multiple_of(c * 8, 8)
        for b in range(32):
            planes_ref[b, pl.ds(r0, 8), :] = planes[b]

    npair = (nch + 1) // 2

    def score_pair(cp, carry):
        score_chunk(2 * cp)
        score_chunk(2 * cp + 1)
        return carry

    lax.fori_loop(0, npair, score_pair, 0)

    live0 = jnp.where(lax.broadcasted_iota(jnp.int32, (prow, qb), 0) < 16 * npair, jnp.int32(-1), jnp.int32(0))
    zero = jnp.zeros((1, qb), jnp.int32)

    def popcount_rows(x):
        cnt = lax.population_count(x)
        return jnp.sum(jnp.sum(cnt.reshape(prow // 8, 8, qb), axis=0), axis=0, keepdims=True)

    def select_bit(b, carry):
        live, n_above, prefix = carry
        plane = planes_ref[b]
        ones = live & plane
        c1 = popcount_rows(ones)
        take = n_above + c1 >= topk
        n_above = jnp.where(take, n_above, n_above + c1)
        live = jnp.where(take, ones, live & ~plane)
        prefix = jnp.where(take, prefix | jnp.left_shift(jnp.int32(1), 31 - b), prefix)
        return live, n_above, prefix

    live, n_gt, prefix = lax.fori_loop(0, 32, select_bit, (live0, zero, zero))
    tau = prefix ^ INT_MIN
    n_eq = popcount_rows(live)
    need = topk - n_gt

    def count(pred):
        def body(c, acc):
            hit = jnp.where(pred(keys_ref[c], c * tk + krow), 1, 0)
            return acc + jnp.sum(hit.reshape(tk // 8, 8, qb), axis=0)

        acc = lax.fori_loop(0, nch, body, jnp.zeros((8, qb), jnp.int32))
        return jnp.sum(acc, axis=0, keepdims=True)

    def tie_search(_):
        def body(it, p):
            cand = p + jnp.left_shift(jnp.int32(1), 14 - it)
            below = count(lambda k, idx: (k == tau) & (idx < cand))
            return jnp.where(below < need, cand, p)
        return lax.fori_loop(0, 15, body, zero) + 1

    has_tie = jnp.max(jnp.where((n_eq > need) & (tau > INT_MIN), 1, 0)) > 0
    jlim = lax.cond(has_tie, tie_search, lambda _: jnp.full((1, qb), seq, jnp.int32), 0)

    q = q_ref[...]
    q2 = jnp.concatenate([q[:, h * HEAD_DIM:(h + 1) * HEAD_DIM] for h in range(N_HEADS)], axis=0)
    m_ref[...] = jnp.full(m_ref.shape, -jnp.inf, F32)
    acc_ref[...] = jnp.zeros(acc_ref.shape, F32)
    exp2_scale = HEAD_DIM ** -0.5 * np.log2(np.e)

    def attend(cp, carry):
        k0 = pl.multiple_of(cp * 2 * tk, 2 * tk)
        bias = []
        for c in (2 * cp, 2 * cp + 1):
            kc = keys_ref[c]
            sel = (kc > INT_MIN) & ((kc > tau) | ((kc == tau) & (c * tk + krow < jlim)))
            bias.append(jnp.where(sel, 0.0, -jnp.inf))
        bias = jnp.concatenate(bias, axis=0)
        s = lax.dot_general(k_ref[pl.ds(k0, 2 * tk), :], q2, NT_DIMS, preferred_element_type=F32)
        vt = jnp.concatenate([vt_ref[2 * cp], vt_ref[2 * cp + 1]], axis=1)
        m_prev = m_ref[...]
        ps, ms, alphas = [], [], []
        for h in range(N_HEADS):
            cols = slice(h * qb, (h + 1) * qb)
            z = s[:, cols] + bias
            m_new = jnp.maximum(m_prev[:, cols], jnp.max(z, axis=0, keepdims=True))
            m_safe = jnp.where(m_new == -jnp.inf, 0.0, m_new)
            ps.append(jnp.exp2((z - m_safe) * exp2_scale).astype(BF16))
            ms.append(m_new)
            alphas.append(jnp.exp2((m_prev[:, cols] - m_safe) * exp2_scale))
        m_ref[...] = jnp.concatenate(ms, axis=1)
        acc_ref[...] = jnp.concatenate(alphas, axis=1) * acc_ref[...] + jnp.dot(
            vt, jnp.concatenate(ps, axis=1), preferred_element_type=F32)
        return carry

    lax.fori_loop(0, npair, attend, 0)
    ot = acc_ref[0:HEAD_DIM, :] / acc_ref[HEAD_DIM:HEAD_DIM + 1, :]
    o = jnp.concatenate([ot[:, h * qb:(h + 1) * qb].T for h in range(N_HEADS)], axis=1)
    g = g_ref[...]
    o_ref[...] = (o * (g * jax.nn.sigmoid(g))).astype(o_ref.dtype)


def _dsa(iq, iwt, ik, qa, ka, vt, u32, batch, seq):
    t = qa.shape[0]
    nb = seq // Q_BLOCK
    topk = min(DSA_TOPK_MAX, seq // 4)
    kern = functools.partial(_dsa_kernel, topk=topk, seq=seq)
    return pl.pallas_call(
        kern,
        out_shape=jax.ShapeDtypeStruct((t, BRANCH_WIDTH), BF16),
        grid=(batch, nb),
        in_specs=[pl.BlockSpec((IDX_HEADS, Q_BLOCK, IDX_DIM), lambda b, i: (0, b * nb + i, 0)),
                  pl.BlockSpec((IDX_HEADS, Q_BLOCK), lambda b, i: (0, b * nb + i)),
                  pl.BlockSpec((seq, IDX_DIM), lambda b, i: (b, 0)),
                  pl.BlockSpec((Q_BLOCK, BRANCH_WIDTH), lambda b, i: (b * nb + i, 0)),
                  pl.BlockSpec((seq, HEAD_DIM), lambda b, i: (b, 0)),
                  pl.BlockSpec((seq // DSA_TK, VT_ROWS, DSA_TK), lambda b, i: (b, 0, 0)),
                  pl.BlockSpec((Q_BLOCK, 1024), lambda b, i: (b * nb + i, BIG["a_g"]))],
        out_specs=pl.BlockSpec((Q_BLOCK, BRANCH_WIDTH), lambda b, i: (b * nb + i, 0)),
        scratch_shapes=[pltpu.VMEM((seq // DSA_TK, DSA_TK, Q_BLOCK), jnp.int32),
                        pltpu.VMEM((32, seq // DSA_TK * 8, Q_BLOCK), jnp.int32),
                        pltpu.VMEM((1, N_HEADS * Q_BLOCK), F32),
                        pltpu.VMEM((VT_ROWS, N_HEADS * Q_BLOCK), F32)],
        compiler_params=_cparams(("arbitrary", "arbitrary")),
        name="dsa_attention",
    )(iq, iwt, ik, qa, ka, vt, u32)


def _conv_kernel(b_ref, c_ref, x_ref, g_ref, cp_ref, xp_ref, w_ref, o_ref, *, tiles_per_seq):
    i = pl.program_id(0)
    tm = b_ref.shape[0]
    u = c_ref[...] * x_ref[...]
    prev = cp_ref[...] * xp_ref[...]
    prev = jnp.where(i % tiles_per_seq == 0, 0.0, prev)
    row = lax.broadcasted_iota(jnp.int32, u.shape, 0)
    u1 = jnp.where(row == 0, prev[7:8, :], pltpu.roll(u, 1, axis=0))
    u2 = jnp.where(row == 0, prev[6:7, :], jnp.where(row == 1, prev[7:8, :], pltpu.roll(u, 2, axis=0)))
    w = w_ref[...]
    y = w[0:1, :] * u2 + w[1:2, :] * u1 + w[2:3, :] * u
    g = g_ref[...]
    o_ref[...] = (b_ref[...] * y * (g * jax.nn.sigmoid(g))).astype(o_ref.dtype)
    del tm


def _conv(u32, conv_w, seq, tm=512):
    t = u32.shape[0]
    tps = seq // tm
    kern = functools.partial(_conv_kernel, tiles_per_seq=tps)

    def big(name):
        return pl.BlockSpec((tm, 1024), lambda i: (i, BIG[name]))

    def halo(name):
        return pl.BlockSpec((8, 1024), lambda i: (jnp.maximum(i * (tm // 8) - 1, 0), BIG[name]))

    return pl.pallas_call(
        kern,
        out_shape=jax.ShapeDtypeStruct((t, BRANCH_WIDTH), BF16),
        grid=(t // tm,),
        in_specs=[big("b_b"), big("b_c"), big("b_x"), big("b_g"), halo("b_c"), halo("b_x"),
                  pl.BlockSpec((CONV_K, BRANCH_WIDTH), lambda i: (0, 0))],
        out_specs=pl.BlockSpec((tm, BRANCH_WIDTH), lambda i: (i, 0)),
        compiler_params=_cparams(("parallel",)),
        name="short_conv",
    )(u32, u32, u32, u32, u32, u32, conv_w)


FOX_TK = 512
FOX_TQ = 1024
FOX_TC = 1024


def _fox_cum_kernel(cf_ref, bias_ref, ccol_ref, crow_ref, carry_ref):
    j = pl.program_id(1)

    @pl.when(j == 0)
    def _():
        carry_ref[...] = jnp.zeros_like(carry_ref)

    x = cf_ref[...] + bias_ref[...]
    log_f = jnp.minimum(x, 0.0) - jnp.log1p(jnp.exp(-jnp.abs(x)))
    tc = x.shape[0]
    tri = (lax.broadcasted_iota(jnp.int32, (tc, tc), 1) <= lax.broadcasted_iota(jnp.int32, (tc, tc), 0)).astype(F32)
    cs = jnp.dot(tri, log_f, precision=lax.Precision.HIGHEST, preferred_element_type=F32) + carry_ref[...]
    ccol_ref[...] = cs
    carry_ref[...] = cs[tc - 1:tc, :]
    cst = cs.T
    for part in range(tc // FOX_TQ):
        crow_ref[part] = cst[0:N_HEADS, part * FOX_TQ:(part + 1) * FOX_TQ]


def _fox_cum(u32, bias, batch, seq):
    t = u32.shape[0]
    tc = FOX_TC
    ns = seq // tc
    per = tc // FOX_TQ
    bias_p = jnp.zeros((1, LANES), F32).at[0, :N_HEADS].set(bias.astype(F32))
    return pl.pallas_call(
        _fox_cum_kernel,
        out_shape=(jax.ShapeDtypeStruct((t, LANES), F32),
                   jax.ShapeDtypeStruct((t // FOX_TQ, N_HEADS, FOX_TQ), F32)),
        grid=(batch, ns),
        in_specs=[pl.BlockSpec((tc, LANES), lambda b, j: (b * ns + j, SLOT_CF)),
                  pl.BlockSpec((1, LANES), lambda b, j: (0, 0))],
        out_specs=(pl.BlockSpec((tc, LANES), lambda b, j: (b * ns + j, 0)),
                   pl.BlockSpec((per, N_HEADS, FOX_TQ), lambda b, j: (b * ns + j, 0, 0))),
        scratch_shapes=[pltpu.VMEM((1, LANES), F32)],
        compiler_params=_cparams(("parallel", "arbitrary")),
        name="fox_cumsum",
    )(u32, bias_p)


def _fox_prep_kernel(q_ref, k_ref, v_ref, qo_ref, ko_ref, vto_ref):
    tm = q_ref.shape[0]
    qo_ref[...] = q_ref[...].astype(BF16)
    ko_ref[...] = k_ref[...].astype(BF16)
    for h in range(N_HEADS):
        vto_ref[h, 0:HEAD_DIM, :] = v_ref[:, h * HEAD_DIM:(h + 1) * HEAD_DIM].T.astype(BF16)
        vto_ref[h, HEAD_DIM:VT_ROWS, :] = jnp.ones((VT_ROWS - HEAD_DIM, tm), BF16)


def _fox_prep(u32):
    t = u32.shape[0]
    tm = FOX_TK

    def big(name):
        return pl.BlockSpec((tm, 1024), lambda i: (i, BIG[name]))

    row_spec = pl.BlockSpec((tm, BRANCH_WIDTH), lambda i: (i, 0))
    return pl.pallas_call(
        _fox_prep_kernel,
        out_shape=(jax.ShapeDtypeStruct((t, BRANCH_WIDTH), BF16),
                   jax.ShapeDtypeStruct((t, BRANCH_WIDTH), BF16),
                   jax.ShapeDtypeStruct((t // tm, N_HEADS, VT_ROWS, tm), BF16)),
        grid=(t // tm,),
        in_specs=[big("c_q"), big("c_k"), big("c_v")],
        out_specs=(row_spec, row_spec, pl.BlockSpec((None, N_HEADS, VT_ROWS, tm), lambda i: (i, 0, 0, 0))),
        compiler_params=_cparams(("parallel",)),
        name="fox_prep",
    )(u32, u32, u32)


def _fox_kernel(q_ref, k_ref, vt_ref, ccol_ref, crow_ref, g_ref, o_ref, ckb_ref, m_ref, acc_ref, *, seq):
    tq, tk, tc = FOX_TQ, FOX_TK, FOX_TC
    h = pl.program_id(1)
    q0 = pl.program_id(2) * tq
    scale = HEAD_DIM ** -0.5
    log2e = np.log2(np.e)

    @pl.when(pl.program_id(2) == 0)
    def _():
        lane = lax.broadcasted_iota(jnp.int32, (tc, LANES), 1)

        def fill(r, carry):
            r0 = pl.multiple_of(r * tc, tc)
            col = jnp.sum(jnp.where(lane == h, ccol_ref[pl.ds(r0, tc), :], 0.0), axis=1, keepdims=True)
            ckb_ref[pl.ds(r0, tc), :] = jnp.broadcast_to(col * log2e, (tc, LANES))
            return carry

        lax.fori_loop(0, seq // tc, fill, 0)

    q = q_ref[...]
    sub = lax.broadcasted_iota(jnp.int32, (N_HEADS, tq), 0)
    cq = jnp.sum(jnp.where(sub == h, crow_ref[...], 0.0), axis=0, keepdims=True) * log2e
    m_ref[...] = jnp.full(m_ref.shape, -jnp.inf, F32)
    acc_ref[...] = jnp.zeros(acc_ref.shape, F32)
    krow = lax.broadcasted_iota(jnp.int32, (tk, LANES), 0)
    qlane = lax.broadcasted_iota(jnp.int32, (tk, LANES), 1)

    def make_body(masked):
        def body(c, carry):
            k0 = pl.multiple_of(c * tk, tk)
            s = lax.dot_general(k_ref[pl.ds(k0, tk), :], q, NT_DIMS, preferred_element_type=F32)
            ckb = ckb_ref[pl.ds(k0, tk), :]
            m_prev = m_ref[...]
            ps, ms, alphas = [], [], []
            for j in range(tq // LANES):
                cols = slice(j * LANES, (j + 1) * LANES)
                y = s[:, cols] * (scale * log2e) - ckb
                if masked:
                    y = jnp.where(k0 + krow <= q0 + j * LANES + qlane, y, -jnp.inf)
                m_new = jnp.maximum(m_prev[:, cols], jnp.max(y, axis=0, keepdims=True) + cq[:, cols])
                ps.append(jnp.exp2(y - (m_new - cq[:, cols])).astype(BF16))
                ms.append(m_new)
                alphas.append(jnp.exp2(m_prev[:, cols] - m_new))
            m_ref[...] = jnp.concatenate(ms, axis=1)
            acc_ref[...] = jnp.concatenate(alphas, axis=1) * acc_ref[...] + jnp.dot(
                vt_ref[c], jnp.concatenate(ps, axis=1), preferred_element_type=F32)
            return carry
        return body

    n_full = q0 // tk
    n_all = (q0 + tq + tk - 1) // tk
    lax.fori_loop(0, n_full, make_body(False), 0)
    lax.fori_loop(n_full, n_all, make_body(True), 0)
    g = g_ref[...]
    ot = acc_ref[0:HEAD_DIM, :] / acc_ref[HEAD_DIM:HEAD_DIM + 1, :]
    o_ref[...] = (ot.T * (g * jax.nn.sigmoid(g))).astype(o_ref.dtype)


def _fox(qb, kb, vt, u32, ccol, crow, batch, seq):
    t = qb.shape[0]
    tq = FOX_TQ
    nq = seq // tq
    nkc = seq // FOX_TK
    cg = BIG["c_g"] * 1024 // LANES
    kern = functools.partial(_fox_kernel, seq=seq)
    return pl.pallas_call(
        kern,
        out_shape=jax.ShapeDtypeStruct((t, BRANCH_WIDTH), BF16),
        grid=(batch, N_HEADS, nq),
        in_specs=[pl.BlockSpec((tq, HEAD_DIM), lambda b, h, i: (b * nq + i, h)),
                  pl.BlockSpec((seq, HEAD_DIM), lambda b, h, i: (b, h)),
                  pl.BlockSpec((nkc, None, VT_ROWS, FOX_TK), lambda b, h, i: (b, h, 0, 0)),
                  pl.BlockSpec((seq, LANES), lambda b, h, i: (b, 0)),
                  pl.BlockSpec((None, N_HEADS, tq), lambda b, h, i: (b * nq + i, 0, 0)),
                  pl.BlockSpec((tq, HEAD_DIM), lambda b, h, i: (b * nq + i, cg + h))],
        out_specs=pl.BlockSpec((tq, HEAD_DIM), lambda b, h, i: (b * nq + i, h)),
        scratch_shapes=[pltpu.VMEM((seq, LANES), F32),
                        pltpu.VMEM((1, tq), F32), pltpu.VMEM((VT_ROWS, tq), F32)],
        compiler_params=_cparams(("parallel", "parallel", "arbitrary")),
        name="fox_attention",
    )(qb, kb, vt, ccol, crow, u32)


def _hgrn_kernel(q_ref, f_ref, i_ref, g_ref, lb_ref, nw_ref, o_ref, st_ref):
    cs, sub = HGRN_CHUNK, HGRN_SUB
    ts = q_ref.shape[0]

    @pl.when(pl.program_id(2) == 0)
    def _():
        st_ref[...] = jnp.zeros_like(st_ref)

    tri = (lax.broadcasted_iota(jnp.int32, (cs, cs), 1) <= lax.broadcasted_iota(jnp.int32, (cs, cs), 0)).astype(F32)
    row_c = lax.broadcasted_iota(jnp.int32, (cs, HEAD_DIM), 0)
    row_s = lax.broadcasted_iota(jnp.int32, (sub, HEAD_DIM), 0)
    lane_s = lax.broadcasted_iota(jnp.int32, (sub, cs), 1)

    def chunk(ci, carry):
        for hh in range(HGRN_HEADS_PER_STEP):
            head_chunk(pl.multiple_of(ci * cs, cs), hh)
        return carry

    def head_chunk(r0, hh):
        cols = slice(hh * HEAD_DIM, (hh + 1) * HEAD_DIM)
        lb = lb_ref[:, cols]
        nw = nw_ref[:, cols]
        f = lb + (1.0 - lb) * jax.nn.sigmoid(f_ref[pl.ds(r0, cs), cols])
        kk = 1.0 - f
        bc = jnp.dot(tri, jnp.log(f), precision=lax.Precision.HIGHEST, preferred_element_type=F32)
        q = q_ref[pl.ds(r0, cs), cols]
        vb = i_ref[pl.ds(r0, cs), cols].astype(BF16)
        st = st_ref[hh]
        o = lax.dot_general((q * jnp.exp(bc)).astype(BF16), st.astype(BF16), NT_DIMS, preferred_element_type=F32)
        att_rows = []
        for si in range(cs // sub):
            lo = si * sub
            bi = bc[lo:lo + sub, :]
            qi = q[lo:lo + sub, :]
            att = jnp.zeros((sub, cs), F32)
            if si > 0:
                b0 = bc[lo - 1:lo, :]
                kt = kk * jnp.exp(jnp.where(row_c < lo, b0 - bc, -jnp.inf))
                qs = qi * jnp.exp(bi - b0)
                att = lax.dot_general(qs.astype(BF16), kt.astype(BF16), NT_DIMS, preferred_element_type=F32)
            for s in range(sub):
                bs = bc[lo + s:lo + s + 1, :]
                ks = kk[lo + s:lo + s + 1, :]
                p = qi * ks * jnp.exp(jnp.where(row_s >= s, bi - bs, -jnp.inf))
                att = att + jnp.where(lane_s == lo + s, jnp.sum(p, axis=1, keepdims=True), 0.0)
            att_rows.append(att)
        att = jnp.concatenate(att_rows, axis=0)
        o = o + jnp.dot(att.astype(BF16), vb, preferred_element_type=F32)
        bl = bc[cs - 1:cs, :]
        kd = kk * jnp.exp(bl - bc)
        st_ref[hh] = st * jnp.exp(bl) + lax.dot_general(vb, kd.astype(BF16), TN_DIMS, preferred_element_type=F32)
        y = o * lax.rsqrt(jnp.mean(o * o, axis=1, keepdims=True) + NORM_EPS) * nw
        g = g_ref[pl.ds(r0, cs), cols]
        o_ref[pl.ds(r0, cs), cols] = (y * (g * jax.nn.sigmoid(g))).astype(o_ref.dtype)

    lax.fori_loop(0, ts // cs, chunk, 0)


def _hgrn(u32, lb, norm_w, batch, seq, ts=512):
    t = u32.shape[0]
    ts = min(ts, seq)
    ns = seq // ts
    hp = HGRN_HEADS_PER_STEP
    wide = hp * HEAD_DIM
    dq, df, di, dg = (BIG[n] * 1024 // wide for n in ("d_q", "d_f", "d_i", "d_g"))

    def col(c0):
        return pl.BlockSpec((ts, wide), lambda b, h, j: (b * ns + j, c0 + h))

    vec = pl.BlockSpec((1, wide), lambda b, h, j: (0, h))
    return pl.pallas_call(
        _hgrn_kernel,
        out_shape=jax.ShapeDtypeStruct((t, BRANCH_WIDTH), BF16),
        grid=(batch, N_HEADS // hp, ns),
        in_specs=[col(dq), col(df), col(di), col(dg), vec, vec],
        out_specs=pl.BlockSpec((ts, wide), lambda b, h, j: (b * ns + j, h)),
        scratch_shapes=[pltpu.VMEM((hp, HEAD_DIM, HEAD_DIM), F32)],
        compiler_params=_cparams(("parallel", "parallel", "arbitrary")),
        name="hgrn2",
    )(u32, u32, u32, u32, lb.reshape(1, BRANCH_WIDTH), norm_w.reshape(1, BRANCH_WIDTH).astype(F32))


def _merge_kernel(ya_ref, yb_ref, yc_ref, yd_ref, h_ref, wb_ref, wm_ref, bm_ref, o_ref):
    tn = o_ref.shape[1]
    h = h_ref[...]
    merged = None
    for br, y_ref in enumerate((ya_ref, yb_ref, yc_ref, yd_ref)):
        proj = jnp.dot(y_ref[...], wb_ref[br], preferred_element_type=F32)
        gate = jnp.concatenate(
            [jnp.dot(h[:, n * MERGE_BLOCK_DIM:(n + 1) * MERGE_BLOCK_DIM], wm_ref[br, n], preferred_element_type=F32)
             for n in range(tn // MERGE_BLOCK_DIM)], axis=1)
        term = jax.nn.sigmoid(gate + bm_ref[br:br + 1, :]) * proj
        merged = term if merged is None else merged + term
    o_ref[...] = merged.astype(o_ref.dtype)


def _merge(ys, h, wb, wm, bm, layer, tm=1024, tn=512):
    t = h.shape[0]
    tm = min(tm, t)
    y_spec = pl.BlockSpec((tm, BRANCH_WIDTH), lambda i, j: (i, 0))
    nmb = tn // MERGE_BLOCK_DIM
    return pl.pallas_call(
        _merge_kernel,
        out_shape=jax.ShapeDtypeStruct((t, D_MODEL), BF16),
        grid=(t // tm, D_MODEL // tn),
        in_specs=[y_spec, y_spec, y_spec, y_spec,
                  pl.BlockSpec((tm, tn), lambda i, j: (i, j)),
                  pl.BlockSpec((None, 4, BRANCH_WIDTH, tn), lambda i, j: (layer, 0, 0, j)),
                  pl.BlockSpec((None, 4, nmb, MERGE_BLOCK_DIM, MERGE_BLOCK_DIM), lambda i, j: (layer, 0, j, 0, 0)),
                  pl.BlockSpec((None, 4, tn), lambda i, j: (layer, 0, j))],
        out_specs=pl.BlockSpec((tm, tn), lambda i, j: (i, j)),
        compiler_params=_cparams(("parallel", "parallel")),
        name="gated_merge",
    )(*ys, h, wb, wm, bm)


def kernel(x, norm_w, w_in, fox_f_bias, conv_w, hgrn_gamma, hgrn_norm_w, w_branch, w_merge, b_merge, w_out, final_norm_w):
    batch, seq, d = x.shape
    depth = w_in.shape[0]
    assert d == D_MODEL and w_in.shape[-1] == IN_WIDTH and seq % FOX_TC == 0
    xf = x.reshape(batch * seq, d)
    w_in_p = _pad_in_proj(w_in)
    wb = w_branch.astype(BF16)
    wm = w_merge.astype(BF16)
    wo = w_out.astype(BF16)
    lower = _lower_bounds(hgrn_gamma)
    tabs = _rope_tables(seq)
    for layer in range(depth):
        h = _rmsnorm(xf, norm_w[layer], BF16)
        u32 = _in_proj(h, w_in_p, layer)
        qa, ka, vt, iq, ik, iwt = _dsa_prep(u32, tabs, seq)
        y_a = _dsa(iq, iwt, ik, qa, ka, vt, u32, batch, seq)
        y_b = _conv(u32, conv_w[layer].astype(F32), seq)
        ccol, crow = _fox_cum(u32, fox_f_bias[layer], batch, seq)
        fq, fk, fvt = _fox_prep(u32)
        y_c = _fox(fq, fk, fvt, u32, ccol, crow, batch, seq)
        y_d = _hgrn(u32, lower[layer], hgrn_norm_w[layer], batch, seq)
        merged = _merge((y_a, y_b, y_c, y_d), h, wb, wm, b_merge.astype(F32), layer)
        xf = _out_proj(merged, wo, layer, xf)
    return _rmsnorm(xf, final_norm_w, F32).reshape(batch, seq, d)
```

```python
import functools

import numpy as np
import jax
import jax.numpy as jnp
from jax import lax
from jax.experimental import pallas as pl
from jax.experimental.pallas import tpu as pltpu

F32 = jnp.float32
BF16 = jnp.bfloat16

D_MODEL = 4096
HEAD_DIM = 128
ROPE_THETA = 10000.0
NORM_EPS = 1e-6
Q_BLOCK = 128
BRANCH_WIDTH = D_MODEL // 4
N_HEADS = BRANCH_WIDTH // HEAD_DIM
IDX_HEADS = 16
IDX_DIM = 64
DSA_TOPK_MAX = 256
CONV_K = 3
HGRN_CHUNK = 64
HGRN_SUB = 16
HGRN_HEADS_PER_STEP = 8
MERGE_BLOCKS = 16
MERGE_BLOCK_DIM = D_MODEL // MERGE_BLOCKS

LANES = 128
VT_ROWS = HEAD_DIM + 16
VMEM_LIMIT = 56 * 1024 * 1024

_SRC_LAYOUT = (
    ("a_q", 1024), ("a_k", 128), ("a_v", 128), ("a_iq", 1024), ("a_ik", 64), ("a_iw", 16), ("a_g", 1024),
    ("b_b", 1024), ("b_c", 1024), ("b_x", 1024), ("b_g", 1024),
    ("c_q", 1024), ("c_k", 1024), ("c_v", 1024), ("c_f", 8), ("c_g", 1024),
    ("d_q", 1024), ("d_f", 1024), ("d_i", 1024), ("d_g", 1024),
)
_SRC_OFF = {}
_o = 0
for _n, _w in _SRC_LAYOUT:
    _SRC_OFF[_n] = (_o, _w)
    _o += _w
IN_WIDTH = _o

_BIG = ("a_q", "a_iq", "a_g", "b_b", "b_c", "b_x", "b_g", "c_q", "c_k", "c_v", "c_g", "d_q", "d_f", "d_i", "d_g")
BIG = {n: i for i, n in enumerate(_BIG)}
SMALL_BASE = len(_BIG) * 1024 // LANES
SLOT_AK, SLOT_AV, SLOT_AIKW, SLOT_CF = SMALL_BASE, SMALL_BASE + 1, SMALL_BASE + 2, SMALL_BASE + 3
PAD_WIDTH = (SMALL_BASE + 4) * LANES

INT_MIN = np.int32(-2 ** 31)
NT_DIMS = (((1,), (1,)), ((), ()))
TN_DIMS = (((0,), (0,)), ((), ()))


def _cparams(sem):
    return pltpu.CompilerParams(dimension_semantics=sem, vmem_limit_bytes=VMEM_LIMIT)


def _pad_in_proj_kernel(w_ref, o_ref):
    rows = w_ref.shape[0]

    def put(dst, name, width=None):
        o, w = _SRC_OFF[name]
        w = w if width is None else width
        o_ref[:, dst:dst + w] = w_ref[:, o:o + w].astype(BF16)
        return dst + w

    dst = 0
    for name in _BIG:
        dst = put(dst, name)
    dst = put(dst, "a_k")
    dst = put(dst, "a_v")
    dst = put(dst, "a_ik", IDX_DIM + IDX_HEADS)
    o_ref[:, dst:dst + LANES - IDX_DIM - IDX_HEADS] = jnp.zeros((rows, LANES - IDX_DIM - IDX_HEADS), BF16)
    dst = put(dst + LANES - IDX_DIM - IDX_HEADS, "c_f")
    o_ref[:, dst:dst + LANES - N_HEADS] = jnp.zeros((rows, LANES - N_HEADS), BF16)


def _pad_in_proj(w_in, tr=128):
    depth, d, n = w_in.shape
    return pl.pallas_call(
        _pad_in_proj_kernel,
        out_shape=jax.ShapeDtypeStruct((depth, d, PAD_WIDTH), BF16),
        grid=(depth, d // tr),
        in_specs=[pl.BlockSpec((None, tr, n), lambda l, i: (l, i, 0))],
        out_specs=pl.BlockSpec((None, tr, PAD_WIDTH), lambda l, i: (l, i, 0)),
        compiler_params=_cparams(("parallel", "parallel")),
        name="pad_in_proj",
    )(w_in)


def _lb_kernel(g_ref, o_ref):
    g = g_ref[...]
    e = jnp.exp(g - jnp.max(g, axis=0, keepdims=True))
    sm = e / jnp.sum(e, axis=0, keepdims=True)
    acc = jnp.zeros_like(sm[0:1])
    rows = []
    for layer in range(g.shape[0]):
        acc = acc + sm[layer:layer + 1]
        rows.append(acc - sm[0:1])
    o_ref[...] = jnp.concatenate(rows, axis=0)


def _lower_bounds(gamma):
    return pl.pallas_call(
        _lb_kernel, out_shape=jax.ShapeDtypeStruct(gamma.shape, F32), name="hgrn_lower_bounds",
    )(gamma.astype(F32))


def _rmsnorm_kernel(x_ref, w_ref, o_ref):
    x = x_ref[...]
    ms = jnp.mean(x * x, axis=-1, keepdims=True)
    o_ref[...] = (x * lax.rsqrt(ms + NORM_EPS) * w_ref[...]).astype(o_ref.dtype)


def _rmsnorm(x, w, out_dtype, tm=256):
    t, d = x.shape
    return pl.pallas_call(
        _rmsnorm_kernel,
        out_shape=jax.ShapeDtypeStruct((t, d), out_dtype),
        grid=(t // tm,),
        in_specs=[pl.BlockSpec((tm, d), lambda i: (i, 0)), pl.BlockSpec((1, d), lambda i: (0, 0))],
        out_specs=pl.BlockSpec((tm, d), lambda i: (i, 0)),
        compiler_params=_cparams(("parallel",)),
        name="rmsnorm",
    )(x, w.reshape(1, d))


def _in_proj_kernel(a_ref, b_ref, o_ref):
    o_ref[...] = jnp.dot(a_ref[...], b_ref[...], preferred_element_type=F32)


def _in_proj(h, w, layer, tm=1024, tn=512):
    t, k = h.shape
    n = w.shape[2]
    tm = min(tm, t)
    return pl.pallas_call(
        _in_proj_kernel,
        out_shape=jax.ShapeDtypeStruct((t, n), F32),
        grid=(t // tm, n // tn),
        in_specs=[pl.BlockSpec((tm, k), lambda i, j: (i, 0)),
                  pl.BlockSpec((None, k, tn), lambda i, j: (layer, 0, j))],
        out_specs=pl.BlockSpec((tm, tn), lambda i, j: (i, j)),
        compiler_params=_cparams(("parallel", "parallel")),
        name="in_proj",
    )(h, w)


def _out_proj_kernel(a_ref, b_ref, r_ref, o_ref):
    o_ref[...] = r_ref[...] + jnp.dot(a_ref[...], b_ref[...], preferred_element_type=F32)


def _out_proj(a, w, layer, resid, tm=1024, tn=512):
    t, k = a.shape
    n = w.shape[2]
    tm = min(tm, t)
    return pl.pallas_call(
        _out_proj_kernel,
        out_shape=jax.ShapeDtypeStruct((t, n), F32),
        grid=(t // tm, n // tn),
        in_specs=[pl.BlockSpec((tm, k), lambda i, j: (i, 0)),
                  pl.BlockSpec((None, k, tn), lambda i, j: (layer, 0, j)),
                  pl.BlockSpec((tm, tn), lambda i, j: (i, j))],
        out_specs=pl.BlockSpec((tm, tn), lambda i, j: (i, j)),
        compiler_params=_cparams(("parallel", "parallel")),
        name="out_proj",
    )(a, w, resid)


def _rope_tables(seq):
    pos = jnp.arange(seq).astype(F32)

    def tables(dim):
        half = dim // 2
        inv = ROPE_THETA ** (-jnp.arange(half, dtype=F32) / half)
        ang = pos[:, None] * inv[None, :]
        cos, sin = jnp.cos(ang), jnp.sin(ang)
        reps = LANES // dim
        cos_t = jnp.tile(jnp.concatenate([cos, cos], axis=1), (1, reps))
        sin_t = jnp.tile(jnp.concatenate([-sin, sin], axis=1), (1, reps))
        return cos_t, sin_t

    return tables(HEAD_DIM) + tables(IDX_DIM)


def _dsa_prep_kernel(q_ref, iq_ref, k_ref, v_ref, ikw_ref, cos_ref, sin_ref, cosi_ref, sini_ref,
                     qo_ref, ko_ref, vto_ref, iqo_ref, iko_ref, iwo_ref):
    cos, sin = cos_ref[...], sin_ref[...]
    cosi, sini = cosi_ref[...], sini_ref[...]
    tm = cos.shape[0]

    def rot_head(x):
        return x * cos + pltpu.roll(x, HEAD_DIM // 2, axis=1) * sin

    lane = lax.broadcasted_iota(jnp.int32, (tm, LANES), 1)
    first_half = (lane % IDX_DIM) < (IDX_DIM // 2)

    def rot_idx(x):
        partner = jnp.where(first_half, pltpu.roll(x, LANES - IDX_DIM // 2, axis=1),
                            pltpu.roll(x, IDX_DIM // 2, axis=1))
        return x * cosi + partner * sini

    for h in range(N_HEADS):
        qo_ref[:, h * HEAD_DIM:(h + 1) * HEAD_DIM] = rot_head(q_ref[:, h * HEAD_DIM:(h + 1) * HEAD_DIM]).astype(BF16)
    ko_ref[...] = rot_head(k_ref[...]).astype(BF16)
    vto_ref[0:HEAD_DIM, :] = v_ref[...].T.astype(BF16)
    vto_ref[HEAD_DIM:VT_ROWS, :] = jnp.ones((VT_ROWS - HEAD_DIM, tm), BF16)
    for p in range(IDX_HEADS // 2):
        r = rot_idx(iq_ref[:, p * LANES:(p + 1) * LANES]).astype(BF16)
        iqo_ref[2 * p] = r[:, :IDX_DIM]
        iqo_ref[2 * p + 1] = r[:, IDX_DIM:]
    ikw = ikw_ref[...]
    iko_ref[...] = rot_idx(ikw)[:, :IDX_DIM].astype(BF16)
    iwo_ref[...] = ikw.T[IDX_DIM:IDX_DIM + IDX_HEADS, :] * (IDX_HEADS ** -0.5 * IDX_DIM ** -0.5)


def _dsa_prep(u32, tabs, seq):
    t = u32.shape[0]
    tm = DSA_TK
    nseq = seq // tm
    cos_a, sin_a, cos_i, sin_i = tabs
    tab_spec = pl.BlockSpec((tm, LANES), lambda i: (i % nseq, 0))
    return pl.pallas_call(
        _dsa_prep_kernel,
        out_shape=(jax.ShapeDtypeStruct((t, BRANCH_WIDTH), BF16),
                   jax.ShapeDtypeStruct((t, HEAD_DIM), BF16),
                   jax.ShapeDtypeStruct((t // tm, VT_ROWS, tm), BF16),
                   jax.ShapeDtypeStruct((IDX_HEADS, t, IDX_DIM), BF16),
                   jax.ShapeDtypeStruct((t, IDX_DIM), BF16),
                   jax.ShapeDtypeStruct((IDX_HEADS, t), F32)),
        grid=(t // tm,),
        in_specs=[pl.BlockSpec((tm, 1024), lambda i: (i, BIG["a_q"])),
                  pl.BlockSpec((tm, 1024), lambda i: (i, BIG["a_iq"])),
                  pl.BlockSpec((tm, LANES), lambda i: (i, SLOT_AK)),
                  pl.BlockSpec((tm, LANES), lambda i: (i, SLOT_AV)),
                  pl.BlockSpec((tm, LANES), lambda i: (i, SLOT_AIKW)),
                  tab_spec, tab_spec, tab_spec, tab_spec],
        out_specs=(pl.BlockSpec((tm, BRANCH_WIDTH), lambda i: (i, 0)),
                   pl.BlockSpec((tm, HEAD_DIM), lambda i: (i, 0)),
                   pl.BlockSpec((None, VT_ROWS, tm), lambda i: (i, 0, 0)),
                   pl.BlockSpec((IDX_HEADS, tm, IDX_DIM), lambda i: (0, i, 0)),
                   pl.BlockSpec((tm, IDX_DIM), lambda i: (i, 0)),
                   pl.BlockSpec((IDX_HEADS, tm), lambda i: (0, i))),
        compiler_params=_cparams(("parallel",)),
        name="dsa_prep",
    )(u32, u32, u32, u32, u32, cos_a, sin_a, cos_i, sin_i)


DSA_TK = 256


def _bit_transpose32(words):
    a = list(words)
    j, m = 16, 0x0000FFFF
    while j:
        k = 0
        while k < 32:
            t = (a[k] ^ lax.shift_right_logical(a[k + j], jnp.int32(j))) & jnp.int32(m)
            a[k] = a[k] ^ t
            a[k + j] = a[k + j] ^ (t << j)
            k = (k + j + 1) & ~j
        j >>= 1
        m = (m ^ (m << j)) & 0xFFFFFFFF if j else m
    return a


def _dsa_kernel(iq_ref, iw_ref, ik_ref, q_ref, k_ref, vt_ref, g_ref, o_ref,
                keys_ref, planes_ref, m_ref, acc_ref, *, topk, seq):
    qb = Q_BLOCK
    tk = DSA_TK
    i = pl.program_id(1)
    q0 = i * qb
    nch = (q0 + qb + tk - 1) // tk
    prow = planes_ref.shape[1]

    @pl.when((pl.program_id(0) == 0) & (i == 0))
    def _():
        planes_ref[...] = jnp.zeros(planes_ref.shape, jnp.int32)

    iw = iw_ref[...]
    iq2 = iq_ref[...].reshape(IDX_HEADS * qb, IDX_DIM)
    krow = lax.broadcasted_iota(jnp.int32, (tk, qb), 0)
    qcol = q0 + lax.broadcasted_iota(jnp.int32, (tk, qb), 1)

    def score_chunk(c):
        k0 = pl.multiple_of(c * tk, tk)
        logit = lax.dot_general(ik_ref[pl.ds(k0, tk), :], iq2, NT_DIMS, preferred_element_type=F32)
        s = jnp.zeros((tk, qb), F32)
        for h in range(IDX_HEADS):
            s = s + iw[h:h + 1, :] * jnp.maximum(logit[:, h * qb:(h + 1) * qb], 0.0)
        bits = lax.bitcast_convert_type(s, jnp.int32)
        key = bits ^ ((bits >> 31) & jnp.int32(0x7FFFFFFF))
        key = jnp.where(k0 + krow <= qcol, key, INT_MIN)
        keys_ref[c] = key
        ukey = key ^ INT_MIN
        planes = _bit_transpose32([ukey[8 * j:8 * j + 8, :] for j in range(32)])
        r0 = pl.multiple_of(c * 8, 8)
        for b in range(32):
            planes_ref[b, pl.ds(r0, 8), :] = planes[b]

    npair = (nch + 1) // 2

    def score_pair(cp, carry):
        score_chunk(2 * cp)
        score_chunk(2 * cp + 1)
        return carry

    lax.fori_loop(0, npair, score_pair, 0)

    live0 = jnp.where(lax.broadcasted_iota(jnp.int32, (prow, qb), 0) < 16 * npair, jnp.int32(-1), jnp.int32(0))
    zero = jnp.zeros((1, qb), jnp.int32)

    def popcount_rows(x):
        cnt = lax.population_count(x)
        return jnp.sum(jnp.sum(cnt.reshape(prow // 8, 8, qb), axis=0), axis=0, keepdims=True)

    def select_bit(b, carry):
        live, n_above, prefix = carry
        plane = planes_ref[b]
        ones = live & plane
        c1 = popcount_rows(ones)
        take = n_above + c1 >= topk
        n_above = jnp.where(take, n_above, n_above + c1)
        live = jnp.where(take, ones, live & ~plane)
        prefix = jnp.where(take, prefix | jnp.left_shift(jnp.int32(1), 31 - b), prefix)
        return live, n_above, prefix

    live, n_gt, prefix = lax.fori_loop(0, 32, select_bit, (live0, zero, zero))
    tau = prefix ^ INT_MIN
    n_eq = popcount_rows(live)
    need = topk - n_gt

    def count(pred):
        def body(c, acc):
            hit = jnp.where(pred(keys_ref[c], c * tk + krow), 1, 0)
            return acc + jnp.sum(hit.reshape(tk // 8, 8, qb), axis=0)

        acc = lax.fori_loop(0, nch, body, jnp.zeros((8, qb), jnp.int32))
        return jnp.sum(acc, axis=0, keepdims=True)

    def tie_search(_):
        def body(it, p):
            cand = p + jnp.left_shift(jnp.int32(1), 14 - it)
            below = count(lambda k, idx: (k == tau) & (idx < cand))
            return jnp.where(below < need, cand, p)
        return lax.fori_loop(0, 15, body, zero) + 1

    has_tie = jnp.max(jnp.where((n_eq > need) & (tau > INT_MIN), 1, 0)) > 0
    jlim = lax.cond(has_tie, tie_search, lambda _: jnp.full((1, qb), seq, jnp.int32), 0)

    q = q_ref[...]
    q2 = jnp.concatenate([q[:, h * HEAD_DIM:(h + 1) * HEAD_DIM] for h in range(N_HEADS)], axis=0)
    m_ref[...] = jnp.full(m_ref.shape, -jnp.inf, F32)
    acc_ref[...] = jnp.zeros(acc_ref.shape, F32)
    exp2_scale = HEAD_DIM ** -0.5 * np.log2(np.e)

    def attend(cp, carry):
        k0 = pl.multiple_of(cp * 2 * tk, 2 * tk)
        bias = []
        for c in (2 * cp, 2 * cp + 1):
            kc = keys_ref[c]
            sel = (kc > INT_MIN) & ((kc > tau) | ((kc == tau) & (c * tk + krow < jlim)))
            bias.append(jnp.where(sel, 0.0, -jnp.inf))
        bias = jnp.concatenate(bias, axis=0)
        s = lax.dot_general(k_ref[pl.ds(k0, 2 * tk), :], q2, NT_DIMS, preferred_element_type=F32)
        vt = jnp.concatenate([vt_ref[2 * cp], vt_ref[2 * cp + 1]], axis=1)
        m_prev = m_ref[...]
        ps, ms, alphas = [], [], []
        for h in range(N_HEADS):
            cols = slice(h * qb, (h + 1) * qb)
            z = s[:, cols] + bias
            m_new = jnp.maximum(m_prev[:, cols], jnp.max(z, axis=0, keepdims=True))
            m_safe = jnp.where(m_new == -jnp.inf, 0.0, m_new)
            ps.append(jnp.exp2((z - m_safe) * exp2_scale).astype(BF16))
            ms.append(m_new)
            alphas.append(jnp.exp2((m_prev[:, cols] - m_safe) * exp2_scale))
        m_ref[...] = jnp.concatenate(ms, axis=1)
        acc_ref[...] = jnp.concatenate(alphas, axis=1) * acc_ref[...] + jnp.dot(
            vt, jnp.concatenate(ps, axis=1), preferred_element_type=F32)
        return carry

    lax.fori_loop(0, npair, attend, 0)
    ot = acc_ref[0:HEAD_DIM, :] / acc_ref[HEAD_DIM:HEAD_DIM + 1, :]
    o = jnp.concatenate([ot[:, h * qb:(h + 1) * qb].T for h in range(N_HEADS)], axis=1)
    g = g_ref[...]
    o_ref[...] = (o * (g * jax.nn.sigmoid(g))).astype(o_ref.dtype)


def _dsa(iq, iwt, ik, qa, ka, vt, u32, batch, seq):
    t = qa.shape[0]
    nb = seq // Q_BLOCK
    topk = min(DSA_TOPK_MAX, seq // 4)
    kern = functools.partial(_dsa_kernel, topk=topk, seq=seq)
    return pl.pallas_call(
        kern,
        out_shape=jax.ShapeDtypeStruct((t, BRANCH_WIDTH), BF16),
        grid=(batch, nb),
        in_specs=[pl.BlockSpec((IDX_HEADS, Q_BLOCK, IDX_DIM), lambda b, i: (0, b * nb + i, 0)),
                  pl.BlockSpec((IDX_HEADS, Q_BLOCK), lambda b, i: (0, b * nb + i)),
                  pl.BlockSpec((seq, IDX_DIM), lambda b, i: (b, 0)),
                  pl.BlockSpec((Q_BLOCK, BRANCH_WIDTH), lambda b, i: (b * nb + i, 0)),
                  pl.BlockSpec((seq, HEAD_DIM), lambda b, i: (b, 0)),
                  pl.BlockSpec((seq // DSA_TK, VT_ROWS, DSA_TK), lambda b, i: (b, 0, 0)),
                  pl.BlockSpec((Q_BLOCK, 1024), lambda b, i: (b * nb + i, BIG["a_g"]))],
        out_specs=pl.BlockSpec((Q_BLOCK, BRANCH_WIDTH), lambda b, i: (b * nb + i, 0)),
        scratch_shapes=[pltpu.VMEM((seq // DSA_TK, DSA_TK, Q_BLOCK), jnp.int32),
                        pltpu.VMEM((32, seq // DSA_TK * 8, Q_BLOCK), jnp.int32),
                        pltpu.VMEM((1, N_HEADS * Q_BLOCK), F32),
                        pltpu.VMEM((VT_ROWS, N_HEADS * Q_BLOCK), F32)],
        compiler_params=_cparams(("arbitrary", "arbitrary")),
        name="dsa_attention",
    )(iq, iwt, ik, qa, ka, vt, u32)


def _conv_kernel(b_ref, c_ref, x_ref, g_ref, cp_ref, xp_ref, w_ref, o_ref, *, tiles_per_seq):
    i = pl.program_id(0)
    tm = b_ref.shape[0]
    u = c_ref[...] * x_ref[...]
    prev = cp_ref[...] * xp_ref[...]
    prev = jnp.where(i % tiles_per_seq == 0, 0.0, prev)
    row = lax.broadcasted_iota(jnp.int32, u.shape, 0)
    u1 = jnp.where(row == 0, prev[7:8, :], pltpu.roll(u, 1, axis=0))
    u2 = jnp.where(row == 0, prev[6:7, :], jnp.where(row == 1, prev[7:8, :], pltpu.roll(u, 2, axis=0)))
    w = w_ref[...]
    y = w[0:1, :] * u2 + w[1:2, :] * u1 + w[2:3, :] * u
    g = g_ref[...]
    o_ref[...] = (b_ref[...] * y * (g * jax.nn.sigmoid(g))).astype(o_ref.dtype)
    del tm


def _conv(u32, conv_w, seq, tm=512):
    t = u32.shape[0]
    tps = seq // tm
    kern = functools.partial(_conv_kernel, tiles_per_seq=tps)

    def big(name):
        return pl.BlockSpec((tm, 1024), lambda i: (i, BIG[name]))

    def halo(name):
        return pl.BlockSpec((8, 1024), lambda i: (jnp.maximum(i * (tm // 8) - 1, 0), BIG[name]))

    return pl.pallas_call(
        kern,
        out_shape=jax.ShapeDtypeStruct((t, BRANCH_WIDTH), BF16),
        grid=(t // tm,),
        in_specs=[big("b_b"), big("b_c"), big("b_x"), big("b_g"), halo("b_c"), halo("b_x"),
                  pl.BlockSpec((CONV_K, BRANCH_WIDTH), lambda i: (0, 0))],
        out_specs=pl.BlockSpec((tm, BRANCH_WIDTH), lambda i: (i, 0)),
        compiler_params=_cparams(("parallel",)),
        name="short_conv",
    )(u32, u32, u32, u32, u32, u32, conv_w)


FOX_TK = 512
FOX_TQ = 1024
FOX_TC = 1024


def _fox_cum_kernel(cf_ref, bias_ref, ccol_ref, crow_ref, carry_ref):
    j = pl.program_id(1)

    @pl.when(j == 0)
    def _():
        carry_ref[...] = jnp.zeros_like(carry_ref)

    x = cf_ref[...] + bias_ref[...]
    log_f = jnp.minimum(x, 0.0) - jnp.log1p(jnp.exp(-jnp.abs(x)))
    tc = x.shape[0]
    tri = (lax.broadcasted_iota(jnp.int32, (tc, tc), 1) <= lax.broadcasted_iota(jnp.int32, (tc, tc), 0)).astype(F32)
    cs = jnp.dot(tri, log_f, precision=lax.Precision.HIGHEST, preferred_element_type=F32) + carry_ref[...]
    ccol_ref[...] = cs
    carry_ref[...] = cs[tc - 1:tc, :]
    cst = cs.T
    for part in range(tc // FOX_TQ):
        crow_ref[part] = cst[0:N_HEADS, part * FOX_TQ:(part + 1) * FOX_TQ]


def _fox_cum(u32, bias, batch, seq):
    t = u32.shape[0]
    tc = FOX_TC
    ns = seq // tc
    per = tc // FOX_TQ
    bias_p = jnp.zeros((1, LANES), F32).at[0, :N_HEADS].set(bias.astype(F32))
    return pl.pallas_call(
        _fox_cum_kernel,
        out_shape=(jax.ShapeDtypeStruct((t, LANES), F32),
                   jax.ShapeDtypeStruct((t // FOX_TQ, N_HEADS, FOX_TQ), F32)),
        grid=(batch, ns),
        in_specs=[pl.BlockSpec((tc, LANES), lambda b, j: (b * ns + j, SLOT_CF)),
                  pl.BlockSpec((1, LANES), lambda b, j: (0, 0))],
        out_specs=(pl.BlockSpec((tc, LANES), lambda b, j: (b * ns + j, 0)),
                   pl.BlockSpec((per, N_HEADS, FOX_TQ), lambda b, j: (b * ns + j, 0, 0))),
        scratch_shapes=[pltpu.VMEM((1, LANES), F32)],
        compiler_params=_cparams(("parallel", "arbitrary")),
        name="fox_cumsum",
    )(u32, bias_p)


def _fox_prep_kernel(q_ref, k_ref, v_ref, qo_ref, ko_ref, vto_ref):
    tm = q_ref.shape[0]
    qo_ref[...] = q_ref[...].astype(BF16)
    ko_ref[...] = k_ref[...].astype(BF16)
    for h in range(N_HEADS):
        vto_ref[h, 0:HEAD_DIM, :] = v_ref[:, h * HEAD_DIM:(h + 1) * HEAD_DIM].T.astype(BF16)
        vto_ref[h, HEAD_DIM:VT_ROWS, :] = jnp.ones((VT_ROWS - HEAD_DIM, tm), BF16)


def _fox_prep(u32):
    t = u32.shape[0]
    tm = FOX_TK

    def big(name):
        return pl.BlockSpec((tm, 1024), lambda i: (i, BIG[name]))

    row_spec = pl.BlockSpec((tm, BRANCH_WIDTH), lambda i: (i, 0))
    return pl.pallas_call(
        _fox_prep_kernel,
        out_shape=(jax.ShapeDtypeStruct((t, BRANCH_WIDTH), BF16),
                   jax.ShapeDtypeStruct((t, BRANCH_WIDTH), BF16),
                   jax.ShapeDtypeStruct((t // tm, N_HEADS, VT_ROWS, tm), BF16)),
        grid=(t // tm,),
        in_specs=[big("c_q"), big("c_k"), big("c_v")],
        out_specs=(row_spec, row_spec, pl.BlockSpec((None, N_HEADS, VT_ROWS, tm), lambda i: (i, 0, 0, 0))),
        compiler_params=_cparams(("parallel",)),
        name="fox_prep",
    )(u32, u32, u32)


def _fox_kernel(q_ref, k_ref, vt_ref, ccol_ref, crow_ref, g_ref, o_ref, ckb_ref, m_ref, acc_ref, *, seq):
    tq, tk, tc = FOX_TQ, FOX_TK, FOX_TC
    h = pl.program_id(1)
    q0 = pl.program_id(2) * tq
    scale = HEAD_DIM ** -0.5
    log2e = np.log2(np.e)

    @pl.when(pl.program_id(2) == 0)
    def _():
        lane = lax.broadcasted_iota(jnp.int32, (tc, LANES), 1)

        def fill(r, carry):
            r0 = pl.multiple_of(r * tc, tc)
            col = jnp.sum(jnp.where(lane == h, ccol_ref[pl.ds(r0, tc), :], 0.0), axis=1, keepdims=True)
            ckb_ref[pl.ds(r0, tc), :] = jnp.broadcast_to(col * log2e, (tc, LANES))
            return carry

        lax.fori_loop(0, seq // tc, fill, 0)

    q = q_ref[...]
    sub = lax.broadcasted_iota(jnp.int32, (N_HEADS, tq), 0)
    cq = jnp.sum(jnp.where(sub == h, crow_ref[...], 0.0), axis=0, keepdims=True) * log2e
    m_ref[...] = jnp.full(m_ref.shape, -jnp.inf, F32)
    acc_ref[...] = jnp.zeros(acc_ref.shape, F32)
    krow = lax.broadcasted_iota(jnp.int32, (tk, LANES), 0)
    qlane = lax.broadcasted_iota(jnp.int32, (tk, LANES), 1)

    def make_body(masked):
        def body(c, carry):
            k0 = pl.multiple_of(c * tk, tk)
            s = lax.dot_general(k_ref[pl.ds(k0, tk), :], q, NT_DIMS, preferred_element_type=F32)
            ckb = ckb_ref[pl.ds(k0, tk), :]
            m_prev = m_ref[...]
            ps, ms, alphas = [], [], []
            for j in range(tq // LANES):
                cols = slice(j * LANES, (j + 1) * LANES)
                y = s[:, cols] * (scale * log2e) - ckb
                if masked:
                    y = jnp.where(k0 + krow <= q0 + j * LANES + qlane, y, -jnp.inf)
                m_new = jnp.maximum(m_prev[:, cols], jnp.max(y, axis=0, keepdims=True) + cq[:, cols])
                ps.append(jnp.exp2(y - (m_new - cq[:, cols])).astype(BF16))
                ms.append(m_new)
                alphas.append(jnp.exp2(m_prev[:, cols] - m_new))
            m_ref[...] = jnp.concatenate(ms, axis=1)
            acc_ref[...] = jnp.concatenate(alphas, axis=1) * acc_ref[...] + jnp.dot(
                vt_ref[c], jnp.concatenate(ps, axis=1), preferred_element_type=F32)
            return carry
        return body

    n_full = q0 // tk
    n_all = (q0 + tq + tk - 1) // tk
    lax.fori_loop(0, n_full, make_body(False), 0)
    lax.fori_loop(n_full, n_all, make_body(True), 0)
    g = g_ref[...]
    ot = acc_ref[0:HEAD_DIM, :] / acc_ref[HEAD_DIM:HEAD_DIM + 1, :]
    o_ref[...] = (ot.T * (g * jax.nn.sigmoid(g))).astype(o_ref.dtype)


def _fox(qb, kb, vt, u32, ccol, crow, batch, seq):
    t = qb.shape[0]
    tq = FOX_TQ
    nq = seq // tq
    nkc = seq // FOX_TK
    cg = BIG["c_g"] * 1024 // LANES
    kern = functools.partial(_fox_kernel, seq=seq)
    return pl.pallas_call(
        kern,
        out_shape=jax.ShapeDtypeStruct((t, BRANCH_WIDTH), BF16),
        grid=(batch, N_HEADS, nq),
        in_specs=[pl.BlockSpec((tq, HEAD_DIM), lambda b, h, i: (b * nq + i, h)),
                  pl.BlockSpec((seq, HEAD_DIM), lambda b, h, i: (b, h)),
                  pl.BlockSpec((nkc, None, VT_ROWS, FOX_TK), lambda b, h, i: (b, h, 0, 0)),
                  pl.BlockSpec((seq, LANES), lambda b, h, i: (b, 0)),
                  pl.BlockSpec((None, N_HEADS, tq), lambda b, h, i: (b * nq + i, 0, 0)),
                  pl.BlockSpec((tq, HEAD_DIM), lambda b, h, i: (b * nq + i, cg + h))],
        out_specs=pl.BlockSpec((tq, HEAD_DIM), lambda b, h, i: (b * nq + i, h)),
        scratch_shapes=[pltpu.VMEM((seq, LANES), F32),
                        pltpu.VMEM((1, tq), F32), pltpu.VMEM((VT_ROWS, tq), F32)],
        compiler_params=_cparams(("parallel", "parallel", "arbitrary")),
        name="fox_attention",
    )(qb, kb, vt, ccol, crow, u32)


def _hgrn_kernel(q_ref, f_ref, i_ref, g_ref, lb_ref, nw_ref, o_ref, st_ref):
    cs, sub = HGRN_CHUNK, HGRN_SUB
    ts = q_ref.shape[0]

    @pl.when(pl.program_id(2) == 0)
    def _():
        st_ref[...] = jnp.zeros_like(st_ref)

    tri = (lax.broadcasted_iota(jnp.int32, (cs, cs), 1) <= lax.broadcasted_iota(jnp.int32, (cs, cs), 0)).astype(F32)
    row_c = lax.broadcasted_iota(jnp.int32, (cs, HEAD_DIM), 0)
    row_h = lax.broadcasted_iota(jnp.int32, (sub // 2, HEAD_DIM), 0)
    lane_h = lax.broadcasted_iota(jnp.int32, (sub // 2, cs), 1)

    def chunk(ci, carry):
        for hh in range(HGRN_HEADS_PER_STEP):
            head_chunk(pl.multiple_of(ci * cs, cs), hh)
        return carry

    def head_chunk(r0, hh):
        cols = slice(hh * HEAD_DIM, (hh + 1) * HEAD_DIM)
        lb = lb_ref[:, cols]
        nw = nw_ref[:, cols]
        f = lb + (1.0 - lb) * jax.nn.sigmoid(f_ref[pl.ds(r0, cs), cols])
        kk = 1.0 - f
        bc = jnp.dot(tri, jnp.log(f), precision=lax.Precision.HIGHEST, preferred_element_type=F32)
        q = q_ref[pl.ds(r0, cs), cols]
        vb = i_ref[pl.ds(r0, cs), cols].astype(BF16)
        st = st_ref[hh]
        o = lax.dot_general((q * jnp.exp(bc)).astype(BF16), st.astype(BF16), NT_DIMS, preferred_element_type=F32)
        att_rows = []
        for si in range(cs // sub):
            lo = si * sub
            bi = bc[lo:lo + sub, :]
            qi = q[lo:lo + sub, :]
            att = jnp.zeros((sub, cs), F32)
            if si > 0:
                b0 = bc[lo - 1:lo, :]
                kt = kk * jnp.exp(jnp.where(row_c < lo, b0 - bc, -jnp.inf))
                qs = qi * jnp.exp(bi - b0)
                att = lax.dot_general(qs.astype(BF16), kt.astype(BF16), NT_DIMS, preferred_element_type=F32)
            half = sub // 2
            diag = [jnp.zeros((half, cs), F32), jnp.zeros((half, cs), F32)]
            for s in range(sub):
                bs = bc[lo + s:lo + s + 1, :]
                ks = kk[lo + s:lo + s + 1, :]
                for part in range(s // half, 2):
                    rows = slice(part * half, (part + 1) * half)
                    arg = bi[rows, :] - bs
                    if part == s // half:
                        arg = jnp.where(row_h >= s - part * half, arg, -jnp.inf)
                    p = qi[rows, :] * ks * jnp.exp(arg)
                    diag[part] = diag[part] + jnp.where(lane_h == lo + s, jnp.sum(p, axis=1, keepdims=True), 0.0)
            att_rows.append(att + jnp.concatenate(diag, axis=0))
        att = jnp.concatenate(att_rows, axis=0)
        o = o + jnp.dot(att.astype(BF16), vb, preferred_element_type=F32)
        bl = bc[cs - 1:cs, :]
        kd = kk * jnp.exp(bl - bc)
        st_ref[hh] = st * jnp.exp(bl) + lax.dot_general(vb, kd.astype(BF16), TN_DIMS, preferred_element_type=F32)
        y = o * lax.rsqrt(jnp.mean(o * o, axis=1, keepdims=True) + NORM_EPS) * nw
        g = g_ref[pl.ds(r0, cs), cols]
        o_ref[pl.ds(r0, cs), cols] = (y * (g * jax.nn.sigmoid(g))).astype(o_ref.dtype)

    lax.fori_loop(0, ts // cs, chunk, 0)


def _hgrn(u32, lb, norm_w, batch, seq, ts=512):
    t = u32.shape[0]
    ts = min(ts, seq)
    ns = seq // ts
    hp = HGRN_HEADS_PER_STEP
    wide = hp * HEAD_DIM
    dq, df, di, dg = (BIG[n] * 1024 // wide for n in ("d_q", "d_f", "d_i", "d_g"))

    def col(c0):
        return pl.BlockSpec((ts, wide), lambda b, h, j: (b * ns + j, c0 + h))

    vec = pl.BlockSpec((1, wide), lambda b, h, j: (0, h))
    return pl.pallas_call(
        _hgrn_kernel,
        out_shape=jax.ShapeDtypeStruct((t, BRANCH_WIDTH), BF16),
        grid=(batch, N_HEADS // hp, ns),
        in_specs=[col(dq), col(df), col(di), col(dg), vec, vec],
        out_specs=pl.BlockSpec((ts, wide), lambda b, h, j: (b * ns + j, h)),
        scratch_shapes=[pltpu.VMEM((hp, HEAD_DIM, HEAD_DIM), F32)],
        compiler_params=_cparams(("parallel", "parallel", "arbitrary")),
        name="hgrn2",
    )(u32, u32, u32, u32, lb.reshape(1, BRANCH_WIDTH), norm_w.reshape(1, BRANCH_WIDTH).astype(F32))


def _merge_kernel(ya_ref, yb_ref, yc_ref, yd_ref, h_ref, wb_ref, wm_ref, bm_ref, o_ref):
    tn = o_ref.shape[1]
    h = h_ref[...]
    merged = None
    for br, y_ref in enumerate((ya_ref, yb_ref, yc_ref, yd_ref)):
        proj = jnp.dot(y_ref[...], wb_ref[br], preferred_element_type=F32)
        gate = jnp.concatenate(
            [jnp.dot(h[:, n * MERGE_BLOCK_DIM:(n + 1) * MERGE_BLOCK_DIM], wm_ref[br, n], preferred_element_type=F32)
             for n in range(tn // MERGE_BLOCK_DIM)], axis=1)
        term = jax.nn.sigmoid(gate + bm_ref[br:br + 1, :]) * proj
        merged = term if merged is None else merged + term
    o_ref[...] = merged.astype(o_ref.dtype)


def _merge(ys, h, wb, wm, bm, layer, tm=1024, tn=512):
    t = h.shape[0]
    tm = min(tm, t)
    y_spec = pl.BlockSpec((tm, BRANCH_WIDTH), lambda i, j: (i, 0))
    nmb = tn // MERGE_BLOCK_DIM
    return pl.pallas_call(
        _merge_kernel,
        out_shape=jax.ShapeDtypeStruct((t, D_MODEL), BF16),
        grid=(t // tm, D_MODEL // tn),
        in_specs=[y_spec, y_spec, y_spec, y_spec,
                  pl.BlockSpec((tm, tn), lambda i, j: (i, j)),
                  pl.BlockSpec((None, 4, BRANCH_WIDTH, tn), lambda i, j: (layer, 0, 0, j)),
                  pl.BlockSpec((None, 4, nmb, MERGE_BLOCK_DIM, MERGE_BLOCK_DIM), lambda i, j: (layer, 0, j, 0, 0)),
                  pl.BlockSpec((None, 4, tn), lambda i, j: (layer, 0, j))],
        out_specs=pl.BlockSpec((tm, tn), lambda i, j: (i, j)),
        compiler_params=_cparams(("parallel", "parallel")),
        name="gated_merge",
    )(*ys, h, wb, wm, bm)


def kernel(x, norm_w, w_in, fox_f_bias, conv_w, hgrn_gamma, hgrn_norm_w, w_branch, w_merge, b_merge, w_out, final_norm_w):
    batch, seq, d = x.shape
    depth = w_in.shape[0]
    assert d == D_MODEL and w_in.shape[-1] == IN_WIDTH and seq % FOX_TC == 0
    xf = x.reshape(batch * seq, d)
    w_in_p = _pad_in_proj(w_in)
    wb = w_branch.astype(BF16)
    wm = w_merge.astype(BF16)
    wo = w_out.astype(BF16)
    lower = _lower_bounds(hgrn_gamma)
    tabs = _rope_tables(seq)
    for layer in range(depth):
        h = _rmsnorm(xf, norm_w[layer], BF16)
        u32 = _in_proj(h, w_in_p, layer)
        qa, ka, vt, iq, ik, iwt = _dsa_prep(u32, tabs, seq)
        y_a = _dsa(iq, iwt, ik, qa, ka, vt, u32, batch, seq)
        y_b = _conv(u32, conv_w[layer].astype(F32), seq)
        ccol, crow = _fox_cum(u32, fox_f_bias[layer], batch, seq)
        fq, fk, fvt = _fox_prep(u32)
        y_c = _fox(fq, fk, fvt, u32, ccol, crow, batch, seq)
        y_d = _hgrn(u32, lower[layer], hgrn_norm_w[layer], batch, seq)
        merged = _merge((y_a, y_b, y_c, y_d), h, wb, wm, b_merge.astype(F32), layer)
        xf = _out_proj(merged, wo, layer, xf)
    return _rmsnorm(xf, final_norm_w, F32).reshape(batch, seq, d)
```

```python
import functools

import numpy as np
import jax
import jax.numpy as jnp
from jax import lax
from jax.experimental import pallas as pl
from jax.experimental.pallas import tpu as pltpu

F32 = jnp.float32
BF16 = jnp.bfloat16

D_MODEL = 4096
HEAD_DIM = 128
ROPE_THETA = 10000.0
NORM_EPS = 1e-6
Q_BLOCK = 128
BRANCH_WIDTH = D_MODEL // 4
N_HEADS = BRANCH_WIDTH // HEAD_DIM
IDX_HEADS = 16
IDX_DIM = 64
DSA_TOPK_MAX = 256
CONV_K = 3
HGRN_CHUNK = 64
HGRN_SUB = 16
HGRN_HEADS_PER_STEP = 8
MERGE_BLOCKS = 16
MERGE_BLOCK_DIM = D_MODEL // MERGE_BLOCKS

LANES = 128
VT_ROWS = HEAD_DIM + 16
VMEM_LIMIT = 56 * 1024 * 1024

_SRC_LAYOUT = (
    ("a_q", 1024), ("a_k", 128), ("a_v", 128), ("a_iq", 1024), ("a_ik", 64), ("a_iw", 16), ("a_g", 1024),
    ("b_b", 1024), ("b_c", 1024), ("b_x", 1024), ("b_g", 1024),
    ("c_q", 1024), ("c_k", 1024), ("c_v", 1024), ("c_f", 8), ("c_g", 1024),
    ("d_q", 1024), ("d_f", 1024), ("d_i", 1024), ("d_g", 1024),
)
_SRC_OFF = {}
_o = 0
for _n, _w in _SRC_LAYOUT:
    _SRC_OFF[_n] = (_o, _w)
    _o += _w
IN_WIDTH = _o

_BIG = ("a_q", "a_iq", "a_g", "b_b", "b_c", "b_x", "b_g", "c_q", "c_k", "c_v", "c_g", "d_q", "d_f", "d_i", "d_g")
BIG = {n: i for i, n in enumerate(_BIG)}
SMALL_BASE = len(_BIG) * 1024 // LANES
SLOT_AK, SLOT_AV, SLOT_AIKW, SLOT_CF = SMALL_BASE, SMALL_BASE + 1, SMALL_BASE + 2, SMALL_BASE + 3
PAD_WIDTH = (SMALL_BASE + 4) * LANES

INT_MIN = np.int32(-2 ** 31)
NT_DIMS = (((1,), (1,)), ((), ()))
TN_DIMS = (((0,), (0,)), ((), ()))


def _cparams(sem):
    return pltpu.CompilerParams(dimension_semantics=sem, vmem_limit_bytes=VMEM_LIMIT)


def _pad_in_proj_kernel(w_ref, o_ref):
    cols = w_ref.shape[1]

    def put(dst, name, width=None):
        o, w = _SRC_OFF[name]
        w = w if width is None else width
        o_ref[dst:dst + w, :] = w_ref[o:o + w, :].astype(BF16)
        return dst + w

    dst = 0
    for name in _BIG:
        dst = put(dst, name)
    dst = put(dst, "a_k")
    dst = put(dst, "a_v")
    dst = put(dst, "a_ik", IDX_DIM + IDX_HEADS)
    o_ref[dst:dst + LANES - IDX_DIM - IDX_HEADS, :] = jnp.zeros((LANES - IDX_DIM - IDX_HEADS, cols), BF16)
    dst = put(dst + LANES - IDX_DIM - IDX_HEADS, "c_f")
    o_ref[dst:dst + LANES - N_HEADS, :] = jnp.zeros((LANES - N_HEADS, cols), BF16)


def _pad_in_proj(w_in, tl=128):
    w_t = jnp.swapaxes(w_in, 1, 2)
    depth, n, d = w_t.shape
    return pl.pallas_call(
        _pad_in_proj_kernel,
        out_shape=jax.ShapeDtypeStruct((depth, PAD_WIDTH, d), BF16),
        grid=(depth, d // tl),
        in_specs=[pl.BlockSpec((None, n, tl), lambda l, i: (l, 0, i))],
        out_specs=pl.BlockSpec((None, PAD_WIDTH, tl), lambda l, i: (l, 0, i)),
        compiler_params=_cparams(("parallel", "parallel")),
        name="pad_in_proj",
    )(w_t)


def _lb_kernel(g_ref, o_ref):
    g = g_ref[...]
    e = jnp.exp(g - jnp.max(g, axis=0, keepdims=True))
    sm = e / jnp.sum(e, axis=0, keepdims=True)
    acc = jnp.zeros_like(sm[0:1])
    rows = []
    for layer in range(g.shape[0]):
        acc = acc + sm[layer:layer + 1]
        rows.append(acc - sm[0:1])
    o_ref[...] = jnp.concatenate(rows, axis=0)


def _lower_bounds(gamma):
    return pl.pallas_call(
        _lb_kernel, out_shape=jax.ShapeDtypeStruct(gamma.shape, F32), name="hgrn_lower_bounds",
    )(gamma.astype(F32))


def _rmsnorm_kernel(x_ref, w_ref, o_ref):
    x = x_ref[...]
    ms = jnp.mean(x * x, axis=-1, keepdims=True)
    o_ref[...] = (x * lax.rsqrt(ms + NORM_EPS) * w_ref[...]).astype(o_ref.dtype)


def _rmsnorm(x, w, out_dtype, tm=256):
    t, d = x.shape
    return pl.pallas_call(
        _rmsnorm_kernel,
        out_shape=jax.ShapeDtypeStruct((t, d), out_dtype),
        grid=(t // tm,),
        in_specs=[pl.BlockSpec((tm, d), lambda i: (i, 0)), pl.BlockSpec((1, d), lambda i: (0, 0))],
        out_specs=pl.BlockSpec((tm, d), lambda i: (i, 0)),
        compiler_params=_cparams(("parallel",)),
        name="rmsnorm",
    )(x, w.reshape(1, d))


def _in_proj_kernel(a_ref, b_ref, o_ref):
    o_ref[...] = lax.dot_general(a_ref[...], b_ref[...], NT_DIMS, preferred_element_type=F32)


def _in_proj(h, w, layer, tm=1024, tn=512):
    t, k = h.shape
    n = w.shape[1]
    tm = min(tm, t)
    return pl.pallas_call(
        _in_proj_kernel,
        out_shape=jax.ShapeDtypeStruct((t, n), F32),
        grid=(t // tm, n // tn),
        in_specs=[pl.BlockSpec((tm, k), lambda i, j: (i, 0)),
                  pl.BlockSpec((None, tn, k), lambda i, j: (layer, j, 0))],
        out_specs=pl.BlockSpec((tm, tn), lambda i, j: (i, j)),
        compiler_params=_cparams(("parallel", "parallel")),
        name="in_proj",
    )(h, w)


def _out_proj_kernel(a_ref, b_ref, r_ref, o_ref):
    o_ref[...] = r_ref[...] + jnp.dot(a_ref[...], b_ref[...], preferred_element_type=F32)


def _out_proj(a, w, layer, resid, tm=1024, tn=512):
    t, k = a.shape
    n = w.shape[2]
    tm = min(tm, t)
    return pl.pallas_call(
        _out_proj_kernel,
        out_shape=jax.ShapeDtypeStruct((t, n), F32),
        grid=(t // tm, n // tn),
        in_specs=[pl.BlockSpec((tm, k), lambda i, j: (i, 0)),
                  pl.BlockSpec((None, k, tn), lambda i, j: (layer, 0, j)),
                  pl.BlockSpec((tm, tn), lambda i, j: (i, j))],
        out_specs=pl.BlockSpec((tm, tn), lambda i, j: (i, j)),
        compiler_params=_cparams(("parallel", "parallel")),
        name="out_proj",
    )(a, w, resid)


def _rope_tables(seq):
    pos = jnp.arange(seq).astype(F32)

    def tables(dim):
        half = dim // 2
        inv = ROPE_THETA ** (-jnp.arange(half, dtype=F32) / half)
        ang = pos[:, None] * inv[None, :]
        cos, sin = jnp.cos(ang), jnp.sin(ang)
        reps = LANES // dim
        cos_t = jnp.tile(jnp.concatenate([cos, cos], axis=1), (1, reps))
        sin_t = jnp.tile(jnp.concatenate([-sin, sin], axis=1), (1, reps))
        return cos_t, sin_t

    return tables(HEAD_DIM) + tables(IDX_DIM)


def _dsa_prep_kernel(q_ref, iq_ref, k_ref, v_ref, ikw_ref, cos_ref, sin_ref, cosi_ref, sini_ref,
                     qo_ref, ko_ref, vto_ref, iqo_ref, iko_ref, iwo_ref):
    cos, sin = cos_ref[...], sin_ref[...]
    cosi, sini = cosi_ref[...], sini_ref[...]
    tm = cos.shape[0]

    def rot_head(x):
        return x * cos + pltpu.roll(x, HEAD_DIM // 2, axis=1) * sin

    lane = lax.broadcasted_iota(jnp.int32, (tm, LANES), 1)
    first_half = (lane % IDX_DIM) < (IDX_DIM // 2)

    def rot_idx(x):
        partner = jnp.where(first_half, pltpu.roll(x, LANES - IDX_DIM // 2, axis=1),
                            pltpu.roll(x, IDX_DIM // 2, axis=1))
        return x * cosi + partner * sini

    for h in range(N_HEADS):
        qo_ref[:, h * HEAD_DIM:(h + 1) * HEAD_DIM] = rot_head(q_ref[:, h * HEAD_DIM:(h + 1) * HEAD_DIM]).astype(BF16)
    ko_ref[...] = rot_head(k_ref[...]).astype(BF16)
    vto_ref[0:HEAD_DIM, :] = v_ref[...].T.astype(BF16)
    vto_ref[HEAD_DIM:VT_ROWS, :] = jnp.ones((VT_ROWS - HEAD_DIM, tm), BF16)
    for p in range(IDX_HEADS // 2):
        r = rot_idx(iq_ref[:, p * LANES:(p + 1) * LANES]).astype(BF16)
        iqo_ref[2 * p] = r[:, :IDX_DIM]
        iqo_ref[2 * p + 1] = r[:, IDX_DIM:]
    ikw = ikw_ref[...]
    iko_ref[...] = rot_idx(ikw)[:, :IDX_DIM].astype(BF16)
    iwo_ref[...] = ikw.T[IDX_DIM:IDX_DIM + IDX_HEADS, :] * (IDX_HEADS ** -0.5 * IDX_DIM ** -0.5)


def _dsa_prep(u32, tabs, seq):
    t = u32.shape[0]
    tm = DSA_TK
    nseq = seq // tm
    cos_a, sin_a, cos_i, sin_i = tabs
    tab_spec = pl.BlockSpec((tm, LANES), lambda i: (i % nseq, 0))
    return pl.pallas_call(
        _dsa_prep_kernel,
        out_shape=(jax.ShapeDtypeStruct((t, BRANCH_WIDTH), BF16),
                   jax.ShapeDtypeStruct((t, HEAD_DIM), BF16),
                   jax.ShapeDtypeStruct((t // tm, VT_ROWS, tm), BF16),
                   jax.ShapeDtypeStruct((IDX_HEADS, t, IDX_DIM), BF16),
                   jax.ShapeDtypeStruct((t, IDX_DIM), BF16),
                   jax.ShapeDtypeStruct((IDX_HEADS, t), F32)),
        grid=(t // tm,),
        in_specs=[pl.BlockSpec((tm, 1024), lambda i: (i, BIG["a_q"])),
                  pl.BlockSpec((tm, 1024), lambda i: (i, BIG["a_iq"])),
                  pl.BlockSpec((tm, LANES), lambda i: (i, SLOT_AK)),
                  pl.BlockSpec((tm, LANES), lambda i: (i, SLOT_AV)),
                  pl.BlockSpec((tm, LANES), lambda i: (i, SLOT_AIKW)),
                  tab_spec, tab_spec, tab_spec, tab_spec],
        out_specs=(pl.BlockSpec((tm, BRANCH_WIDTH), lambda i: (i, 0)),
                   pl.BlockSpec((tm, HEAD_DIM), lambda i: (i, 0)),
                   pl.BlockSpec((None, VT_ROWS, tm), lambda i: (i, 0, 0)),
                   pl.BlockSpec((IDX_HEADS, tm, IDX_DIM), lambda i: (0, i, 0)),
                   pl.BlockSpec((tm, IDX_DIM), lambda i: (i, 0)),
                   pl.BlockSpec((IDX_HEADS, tm), lambda i: (0, i))),
        compiler_params=_cparams(("parallel",)),
        name="dsa_prep",
    )(u32, u32, u32, u32, u32, cos_a, sin_a, cos_i, sin_i)


DSA_TK = 256


def _bit_transpose32(words):
    a = list(words)
    j, m = 16, 0x0000FFFF
    while j:
        k = 0
        while k < 32:
            t = (a[k] ^ lax.shift_right_logical(a[k + j], jnp.int32(j))) & jnp.int32(m)
            a[k] = a[k] ^ t
            a[k + j] = a[k + j] ^ (t << j)
            k = (k + j + 1) & ~j
        j >>= 1
        m = (m ^ (m << j)) & 0xFFFFFFFF if j else m
    return a


def _dsa_kernel(iq_ref, iw_ref, ik_ref, q_ref, k_ref, vt_ref, g_ref, o_ref,
                keys_ref, planes_ref, m_ref, acc_ref, *, topk, seq):
    qb = Q_BLOCK
    tk = DSA_TK
    i = pl.program_id(1)
    q0 = i * qb
    nch = (q0 + qb + tk - 1) // tk
    prow = planes_ref.shape[1]

    @pl.when((pl.program_id(0) == 0) & (i == 0))
    def _():
        planes_ref[...] = jnp.zeros(planes_ref.shape, jnp.int32)

    iw = iw_ref[...]
    iq2 = iq_ref[...].reshape(IDX_HEADS * qb, IDX_DIM)
    krow = lax.broadcasted_iota(jnp.int32, (tk, qb), 0)
    qcol = q0 + lax.broadcasted_iota(jnp.int32, (tk, qb), 1)

    def score_chunk(c):
        k0 = pl.multiple_of(c * tk, tk)
        logit = lax.dot_general(ik_ref[pl.ds(k0, tk), :], iq2, NT_DIMS, preferred_element_type=F32)
        s = jnp.zeros((tk, qb), F32)
        for h in range(IDX_HEADS):
            s = s + iw[h:h + 1, :] * jnp.maximum(logit[:, h * qb:(h + 1) * qb], 0.0)
        bits = lax.bitcast_convert_type(s, jnp.int32)
        key = bits ^ ((bits >> 31) & jnp.int32(0x7FFFFFFF))
        key = jnp.where(k0 + krow <= qcol, key, INT_MIN)
        keys_ref[c] = key
        ukey = key ^ INT_MIN
        planes = _bit_transpose32([ukey[8 * j:8 * j + 8, :] for j in range(32)])
        r0 = pl.multiple_of(c * 8, 8)
        for b in range(32):
            planes_ref[b, pl.ds(r0, 8), :] = planes[b]

    npair = (nch + 1) // 2

    def score_pair(cp, carry):
        score_chunk(2 * cp)
        score_chunk(2 * cp + 1)
        return carry

    lax.fori_loop(0, npair, score_pair, 0)

    live0 = jnp.where(lax.broadcasted_iota(jnp.int32, (prow, qb), 0) < 16 * npair, jnp.int32(-1), jnp.int32(0))
    zero = jnp.zeros((1, qb), jnp.int32)

    def popcount_rows(x):
        cnt = lax.population_count(x)
        return jnp.sum(jnp.sum(cnt.reshape(prow // 8, 8, qb), axis=0), axis=0, keepdims=True)

    def select_bit(b, carry):
        live, n_above, prefix = carry
        plane = planes_ref[b]
        ones = live & plane
        c1 = popcount_rows(ones)
        take = n_above + c1 >= topk
        n_above = jnp.where(take, n_above, n_above + c1)
        live = jnp.where(take, ones, live & ~plane)
        prefix = jnp.where(take, prefix | jnp.left_shift(jnp.int32(1), 31 - b), prefix)
        return live, n_above, prefix

    live, n_gt, prefix = lax.fori_loop(0, 32, select_bit, (live0, zero, zero))
    tau = prefix ^ INT_MIN
    n_eq = popcount_rows(live)
    need = topk - n_gt

    def count(pred):
        def body(c, acc):
            hit = jnp.where(pred(keys_ref[c], c * tk + krow), 1, 0)
            return acc + jnp.sum(hit.reshape(tk // 8, 8, qb), axis=0)

        acc = lax.fori_loop(0, nch, body, jnp.zeros((8, qb), jnp.int32))
        return jnp.sum(acc, axis=0, keepdims=True)

    def tie_search(_):
        def body(it, p):
            cand = p + jnp.left_shift(jnp.int32(1), 14 - it)
            below = count(lambda k, idx: (k == tau) & (idx < cand))
            return jnp.where(below < need, cand, p)
        return lax.fori_loop(0, 15, body, zero) + 1

    has_tie = jnp.max(jnp.where((n_eq > need) & (tau > INT_MIN), 1, 0)) > 0
    jlim = lax.cond(has_tie, tie_search, lambda _: jnp.full((1, qb), seq, jnp.int32), 0)

    q = q_ref[...]
    q2 = jnp.concatenate([q[:, h * HEAD_DIM:(h + 1) * HEAD_DIM] for h in range(N_HEADS)], axis=0)
    m_ref[...] = jnp.full(m_ref.shape, -jnp.inf, F32)
    acc_ref[...] = jnp.zeros(acc_ref.shape, F32)
    exp2_scale = HEAD_DIM ** -0.5 * np.log2(np.e)

    def attend(cp, carry):
        k0 = pl.multiple_of(cp * 2 * tk, 2 * tk)
        bias = []
        for c in (2 * cp, 2 * cp + 1):
            kc = keys_ref[c]
            sel = (kc > INT_MIN) & ((kc > tau) | ((kc == tau) & (c * tk + krow < jlim)))
            bias.append(jnp.where(sel, 0.0, -jnp.inf))
        bias = jnp.concatenate(bias, axis=0)
        s = lax.dot_general(k_ref[pl.ds(k0, 2 * tk), :], q2, NT_DIMS, preferred_element_type=F32)
        vt = jnp.concatenate([vt_ref[2 * cp], vt_ref[2 * cp + 1]], axis=1)
        m_prev = m_ref[...]
        ps, ms, alphas = [], [], []
        for h in range(N_HEADS):
            cols = slice(h * qb, (h + 1) * qb)
            z = s[:, cols] + bias
            m_new = jnp.maximum(m_prev[:, cols], jnp.max(z, axis=0, keepdims=True))
            m_safe = jnp.where(m_new == -jnp.inf, 0.0, m_new)
            ps.append(jnp.exp2((z - m_safe) * exp2_scale).astype(BF16))
            ms.append(m_new)
            alphas.append(jnp.exp2((m_prev[:, cols] - m_safe) * exp2_scale))
        m_ref[...] = jnp.concatenate(ms, axis=1)
        acc_ref[...] = jnp.concatenate(alphas, axis=1) * acc_ref[...] + jnp.dot(
            vt, jnp.concatenate(ps, axis=1), preferred_element_type=F32)
        return carry

    lax.fori_loop(0, npair, attend, 0)
    ot = acc_ref[0:HEAD_DIM, :] / acc_ref[HEAD_DIM:HEAD_DIM + 1, :]
    o = jnp.concatenate([ot[:, h * qb:(h + 1) * qb].T for h in range(N_HEADS)], axis=1)
    g = g_ref[...]
    o_ref[...] = (o * (g * jax.nn.sigmoid(g))).astype(o_ref.dtype)


def _dsa(iq, iwt, ik, qa, ka, vt, u32, batch, seq):
    t = qa.shape[0]
    nb = seq // Q_BLOCK
    topk = min(DSA_TOPK_MAX, seq // 4)
    kern = functools.partial(_dsa_kernel, topk=topk, seq=seq)
    return pl.pallas_call(
        kern,
        out_shape=jax.ShapeDtypeStruct((t, BRANCH_WIDTH), BF16),
        grid=(batch, nb),
        in_specs=[pl.BlockSpec((IDX_HEADS, Q_BLOCK, IDX_DIM), lambda b, i: (0, b * nb + i, 0)),
                  pl.BlockSpec((IDX_HEADS, Q_BLOCK), lambda b, i: (0, b * nb + i)),
                  pl.BlockSpec((seq, IDX_DIM), lambda b, i: (b, 0)),
                  pl.BlockSpec((Q_BLOCK, BRANCH_WIDTH), lambda b, i: (b * nb + i, 0)),
                  pl.BlockSpec((seq, HEAD_DIM), lambda b, i: (b, 0)),
                  pl.BlockSpec((seq // DSA_TK, VT_ROWS, DSA_TK), lambda b, i: (b, 0, 0)),
                  pl.BlockSpec((Q_BLOCK, 1024), lambda b, i: (b * nb + i, BIG["a_g"]))],
        out_specs=pl.BlockSpec((Q_BLOCK, BRANCH_WIDTH), lambda b, i: (b * nb + i, 0)),
        scratch_shapes=[pltpu.VMEM((seq // DSA_TK, DSA_TK, Q_BLOCK), jnp.int32),
                        pltpu.VMEM((32, seq // DSA_TK * 8, Q_BLOCK), jnp.int32),
                        pltpu.VMEM((1, N_HEADS * Q_BLOCK), F32),
                        pltpu.VMEM((VT_ROWS, N_HEADS * Q_BLOCK), F32)],
        compiler_params=_cparams(("arbitrary", "arbitrary")),
        name="dsa_attention",
    )(iq, iwt, ik, qa, ka, vt, u32)


def _conv_kernel(b_ref, c_ref, x_ref, g_ref, cp_ref, xp_ref, w_ref, o_ref, *, tiles_per_seq):
    i = pl.program_id(0)
    tm = b_ref.shape[0]
    u = c_ref[...] * x_ref[...]
    prev = cp_ref[...] * xp_ref[...]
    prev = jnp.where(i % tiles_per_seq == 0, 0.0, prev)
    row = lax.broadcasted_iota(jnp.int32, u.shape, 0)
    u1 = jnp.where(row == 0, prev[7:8, :], pltpu.roll(u, 1, axis=0))
    u2 = jnp.where(row == 0, prev[6:7, :], jnp.where(row == 1, prev[7:8, :], pltpu.roll(u, 2, axis=0)))
    w = w_ref[...]
    y = w[0:1, :] * u2 + w[1:2, :] * u1 + w[2:3, :] * u
    g = g_ref[...]
    o_ref[...] = (b_ref[...] * y * (g * jax.nn.sigmoid(g))).astype(o_ref.dtype)
    del tm


def _conv(u32, conv_w, seq, tm=512):
    t = u32.shape[0]
    tps = seq // tm
    kern = functools.partial(_conv_kernel, tiles_per_seq=tps)

    def big(name):
        return pl.BlockSpec((tm, 1024), lambda i: (i, BIG[name]))

    def halo(name):
        return pl.BlockSpec((8, 1024), lambda i: (jnp.maximum(i * (tm // 8) - 1, 0), BIG[name]))

    return pl.pallas_call(
        kern,
        out_shape=jax.ShapeDtypeStruct((t, BRANCH_WIDTH), BF16),
        grid=(t // tm,),
        in_specs=[big("b_b"), big("b_c"), big("b_x"), big("b_g"), halo("b_c"), halo("b_x"),
                  pl.BlockSpec((CONV_K, BRANCH_WIDTH), lambda i: (0, 0))],
        out_specs=pl.BlockSpec((tm, BRANCH_WIDTH), lambda i: (i, 0)),
        compiler_params=_cparams(("parallel",)),
        name="short_conv",
    )(u32, u32, u32, u32, u32, u32, conv_w)


FOX_TK = 512
FOX_TQ = 1024
FOX_TC = 1024
FOX_HEADS_PER_STEP = 2


def _fox_cum_kernel(cf_ref, bias_ref, ccol_ref, crow_ref, carry_ref):
    j = pl.program_id(1)

    @pl.when(j == 0)
    def _():
        carry_ref[...] = jnp.zeros_like(carry_ref)

    x = cf_ref[...] + bias_ref[...]
    log_f = jnp.minimum(x, 0.0) - jnp.log1p(jnp.exp(-jnp.abs(x)))
    tc = x.shape[0]
    tri = (lax.broadcasted_iota(jnp.int32, (tc, tc), 1) <= lax.broadcasted_iota(jnp.int32, (tc, tc), 0)).astype(F32)
    cs = jnp.dot(tri, log_f, precision=lax.Precision.HIGHEST, preferred_element_type=F32) + carry_ref[...]
    ccol_ref[...] = cs
    carry_ref[...] = cs[tc - 1:tc, :]
    cst = cs.T
    for part in range(tc // FOX_TQ):
        crow_ref[part] = cst[0:N_HEADS, part * FOX_TQ:(part + 1) * FOX_TQ]


def _fox_cum(u32, bias, batch, seq):
    t = u32.shape[0]
    tc = FOX_TC
    ns = seq // tc
    per = tc // FOX_TQ
    bias_p = jnp.zeros((1, LANES), F32).at[0, :N_HEADS].set(bias.astype(F32))
    return pl.pallas_call(
        _fox_cum_kernel,
        out_shape=(jax.ShapeDtypeStruct((t, LANES), F32),
                   jax.ShapeDtypeStruct((t // FOX_TQ, N_HEADS, FOX_TQ), F32)),
        grid=(batch, ns),
        in_specs=[pl.BlockSpec((tc, LANES), lambda b, j: (b * ns + j, SLOT_CF)),
                  pl.BlockSpec((1, LANES), lambda b, j: (0, 0))],
        out_specs=(pl.BlockSpec((tc, LANES), lambda b, j: (b * ns + j, 0)),
                   pl.BlockSpec((per, N_HEADS, FOX_TQ), lambda b, j: (b * ns + j, 0, 0))),
        scratch_shapes=[pltpu.VMEM((1, LANES), F32)],
        compiler_params=_cparams(("parallel", "arbitrary")),
        name="fox_cumsum",
    )(u32, bias_p)


def _fox_prep_kernel(q_ref, k_ref, v_ref, qo_ref, ko_ref, vto_ref):
    tm = q_ref.shape[0]
    qo_ref[...] = q_ref[...].astype(BF16)
    ko_ref[...] = k_ref[...].astype(BF16)
    for h in range(N_HEADS):
        vto_ref[h, 0:HEAD_DIM, :] = v_ref[:, h * HEAD_DIM:(h + 1) * HEAD_DIM].T.astype(BF16)
        vto_ref[h, HEAD_DIM:VT_ROWS, :] = jnp.ones((VT_ROWS - HEAD_DIM, tm), BF16)


def _fox_prep(u32):
    t = u32.shape[0]
    tm = FOX_TK

    def big(name):
        return pl.BlockSpec((tm, 1024), lambda i: (i, BIG[name]))

    row_spec = pl.BlockSpec((tm, BRANCH_WIDTH), lambda i: (i, 0))
    return pl.pallas_call(
        _fox_prep_kernel,
        out_shape=(jax.ShapeDtypeStruct((t, BRANCH_WIDTH), BF16),
                   jax.ShapeDtypeStruct((t, BRANCH_WIDTH), BF16),
                   jax.ShapeDtypeStruct((t // tm, N_HEADS, VT_ROWS, tm), BF16)),
        grid=(t // tm,),
        in_specs=[big("c_q"), big("c_k"), big("c_v")],
        out_specs=(row_spec, row_spec, pl.BlockSpec((None, N_HEADS, VT_ROWS, tm), lambda i: (i, 0, 0, 0))),
        compiler_params=_cparams(("parallel",)),
        name="fox_prep",
    )(u32, u32, u32)


def _fox_kernel(q_ref, k_ref, vt_ref, ccol_ref, crow_ref, g_ref, o_ref, ckb_ref, m_ref, acc_ref, *, seq):
    tq, tk, tc, hp = FOX_TQ, FOX_TK, FOX_TC, FOX_HEADS_PER_STEP
    h0 = pl.program_id(1) * hp
    q0 = pl.program_id(2) * tq
    scale = HEAD_DIM ** -0.5
    log2e = np.log2(np.e)

    @pl.when(pl.program_id(2) == 0)
    def _():
        lane = lax.broadcasted_iota(jnp.int32, (tc, LANES), 1)

        def fill(r, carry):
            r0 = pl.multiple_of(r * tc, tc)
            blk = ccol_ref[pl.ds(r0, tc), :]
            for hh in range(hp):
                col = jnp.sum(jnp.where(lane == h0 + hh, blk, 0.0), axis=1, keepdims=True)
                ckb_ref[hh, pl.ds(r0, tc), :] = jnp.broadcast_to(col * log2e, (tc, LANES))
            return carry

        lax.fori_loop(0, seq // tc, fill, 0)

    sub = lax.broadcasted_iota(jnp.int32, (N_HEADS, tq), 0)
    crow = crow_ref[...]
    cqs = [jnp.sum(jnp.where(sub == h0 + hh, crow, 0.0), axis=0, keepdims=True) * log2e for hh in range(hp)]
    m_ref[...] = jnp.full(m_ref.shape, -jnp.inf, F32)
    acc_ref[...] = jnp.zeros(acc_ref.shape, F32)
    krow = lax.broadcasted_iota(jnp.int32, (tk, LANES), 0)
    qlane = lax.broadcasted_iota(jnp.int32, (tk, LANES), 1)

    def head_step(hh, c, k0, masked):
        hcols = slice(hh * HEAD_DIM, (hh + 1) * HEAD_DIM)
        s = lax.dot_general(k_ref[pl.ds(k0, tk), hcols], q_ref[:, hcols], NT_DIMS, preferred_element_type=F32)
        ckb = ckb_ref[hh, pl.ds(k0, tk), :]
        cq = cqs[hh]
        m_prev = m_ref[hh]
        ps, ms, alphas = [], [], []
        for j in range(tq // LANES):
            cols = slice(j * LANES, (j + 1) * LANES)
            y = s[:, cols] * (scale * log2e) - ckb
            if masked:
                y = jnp.where(k0 + krow <= q0 + j * LANES + qlane, y, -jnp.inf)
            m_new = jnp.maximum(m_prev[:, cols], jnp.max(y, axis=0, keepdims=True) + cq[:, cols])
            ps.append(jnp.exp2(y - (m_new - cq[:, cols])).astype(BF16))
            ms.append(m_new)
            alphas.append(jnp.exp2(m_prev[:, cols] - m_new))
        m_ref[hh] = jnp.concatenate(ms, axis=1)
        acc_ref[hh] = jnp.concatenate(alphas, axis=1) * acc_ref[hh] + jnp.dot(
            vt_ref[c, hh], jnp.concatenate(ps, axis=1), preferred_element_type=F32)

    def make_body(masked):
        def body(c, carry):
            k0 = pl.multiple_of(c * tk, tk)
            for hh in range(hp):
                head_step(hh, c, k0, masked)
            return carry
        return body

    n_full = q0 // tk
    n_all = (q0 + tq + tk - 1) // tk
    lax.fori_loop(0, n_full, make_body(False), 0)
    lax.fori_loop(n_full, n_all, make_body(True), 0)
    for hh in range(hp):
        hcols = slice(hh * HEAD_DIM, (hh + 1) * HEAD_DIM)
        g = g_ref[:, hcols]
        ot = acc_ref[hh, 0:HEAD_DIM, :] / acc_ref[hh, HEAD_DIM:HEAD_DIM + 1, :]
        o_ref[:, hcols] = (ot.T * (g * jax.nn.sigmoid(g))).astype(o_ref.dtype)


def _fox(qb, kb, vt, u32, ccol, crow, batch, seq):
    t = qb.shape[0]
    tq, hp = FOX_TQ, FOX_HEADS_PER_STEP
    wide = hp * HEAD_DIM
    nq = seq // tq
    nkc = seq // FOX_TK
    cg = BIG["c_g"] * 1024 // wide
    kern = functools.partial(_fox_kernel, seq=seq)
    return pl.pallas_call(
        kern,
        out_shape=jax.ShapeDtypeStruct((t, BRANCH_WIDTH), BF16),
        grid=(batch, N_HEADS // hp, nq),
        in_specs=[pl.BlockSpec((tq, wide), lambda b, h, i: (b * nq + i, h)),
                  pl.BlockSpec((seq, wide), lambda b, h, i: (b, h)),
                  pl.BlockSpec((nkc, hp, VT_ROWS, FOX_TK), lambda b, h, i: (b, h, 0, 0)),
                  pl.BlockSpec((seq, LANES), lambda b, h, i: (b, 0)),
                  pl.BlockSpec((None, N_HEADS, tq), lambda b, h, i: (b * nq + i, 0, 0)),
                  pl.BlockSpec((tq, wide), lambda b, h, i: (b * nq + i, cg + h))],
        out_specs=pl.BlockSpec((tq, wide), lambda b, h, i: (b * nq + i, h)),
        scratch_shapes=[pltpu.VMEM((hp, seq, LANES), F32),
                        pltpu.VMEM((hp, 1, tq), F32), pltpu.VMEM((hp, VT_ROWS, tq), F32)],
        compiler_params=_cparams(("parallel", "parallel", "arbitrary")),
        name="fox_attention",
    )(qb, kb, vt, ccol, crow, u32)


def _hgrn_kernel(q_ref, f_ref, i_ref, g_ref, lb_ref, nw_ref, o_ref, st_ref):
    cs, sub = HGRN_CHUNK, HGRN_SUB
    ts = q_ref.shape[0]

    @pl.when(pl.program_id(2) == 0)
    def _():
        st_ref[...] = jnp.zeros_like(st_ref)

    tri = (lax.broadcasted_iota(jnp.int32, (cs, cs), 1) <= lax.broadcasted_iota(jnp.int32, (cs, cs), 0)).astype(F32)
    row_c = lax.broadcasted_iota(jnp.int32, (cs, HEAD_DIM), 0)
    row_h = lax.broadcasted_iota(jnp.int32, (sub // 2, HEAD_DIM), 0)
    lane_h = lax.broadcasted_iota(jnp.int32, (sub // 2, cs), 1)

    def chunk(ci, carry):
        for hh in range(HGRN_HEADS_PER_STEP):
            head_chunk(pl.multiple_of(ci * cs, cs), hh)
        return carry

    def head_chunk(r0, hh):
        cols = slice(hh * HEAD_DIM, (hh + 1) * HEAD_DIM)
        lb = lb_ref[:, cols]
        nw = nw_ref[:, cols]
        f = lb + (1.0 - lb) * jax.nn.sigmoid(f_ref[pl.ds(r0, cs), cols])
        kk = 1.0 - f
        bc = jnp.dot(tri, jnp.log(f), precision=lax.Precision.HIGHEST, preferred_element_type=F32)
        q = q_ref[pl.ds(r0, cs), cols]
        vb = i_ref[pl.ds(r0, cs), cols].astype(BF16)
        st = st_ref[hh]
        o = lax.dot_general((q * jnp.exp(bc)).astype(BF16), st.astype(BF16), NT_DIMS, preferred_element_type=F32)
        att_rows = []
        for si in range(cs // sub):
            lo = si * sub
            bi = bc[lo:lo + sub, :]
            qi = q[lo:lo + sub, :]
            att = jnp.zeros((sub, cs), F32)
            if si > 0:
                b0 = bc[lo - 1:lo, :]
                kt = kk * jnp.exp(jnp.where(row_c < lo, b0 - bc, -jnp.inf))
                qs = qi * jnp.exp(bi - b0)
                att = lax.dot_general(qs.astype(BF16), kt.astype(BF16), NT_DIMS, preferred_element_type=F32)
            half = sub // 2
            diag = [jnp.zeros((half, cs), F32), jnp.zeros((half, cs), F32)]
            for s in range(sub):
                bs = bc[lo + s:lo + s + 1, :]
                ks = kk[lo + s:lo + s + 1, :]
                for part in range(s // half, 2):
                    rows = slice(part * half, (part + 1) * half)
                    arg = bi[rows, :] - bs
                    if part == s // half:
                        arg = jnp.where(row_h >= s - part * half, arg, -jnp.inf)
                    p = qi[rows, :] * ks * jnp.exp(arg)
                    diag[part] = diag[part] + jnp.where(lane_h == lo + s, jnp.sum(p, axis=1, keepdims=True), 0.0)
            att_rows.append(att + jnp.concatenate(diag, axis=0))
        att = jnp.concatenate(att_rows, axis=0)
        o = o + jnp.dot(att.astype(BF16), vb, preferred_element_type=F32)
        bl = bc[cs - 1:cs, :]
        kd = kk * jnp.exp(bl - bc)
        st_ref[hh] = st * jnp.exp(bl) + lax.dot_general(vb, kd.astype(BF16), TN_DIMS, preferred_element_type=F32)
        y = o * lax.rsqrt(jnp.mean(o * o, axis=1, keepdims=True) + NORM_EPS) * nw
        g = g_ref[pl.ds(r0, cs), cols]
        o_ref[pl.ds(r0, cs), cols] = (y * (g * jax.nn.sigmoid(g))).astype(o_ref.dtype)

    lax.fori_loop(0, ts // cs, chunk, 0)


def _hgrn(u32, lb, norm_w, batch, seq, ts=512):
    t = u32.shape[0]
    ts = min(ts, seq)
    ns = seq // ts
    hp = HGRN_HEADS_PER_STEP
    wide = hp * HEAD_DIM
    dq, df, di, dg = (BIG[n] * 1024 // wide for n in ("d_q", "d_f", "d_i", "d_g"))

    def col(c0):
        return pl.BlockSpec((ts, wide), lambda b, h, j: (b * ns + j, c0 + h))

    vec = pl.BlockSpec((1, wide), lambda b, h, j: (0, h))
    return pl.pallas_call(
        _hgrn_kernel,
        out_shape=jax.ShapeDtypeStruct((t, BRANCH_WIDTH), BF16),
        grid=(batch, N_HEADS // hp, ns),
        in_specs=[col(dq), col(df), col(di), col(dg), vec, vec],
        out_specs=pl.BlockSpec((ts, wide), lambda b, h, j: (b * ns + j, h)),
        scratch_shapes=[pltpu.VMEM((hp, HEAD_DIM, HEAD_DIM), F32)],
        compiler_params=_cparams(("parallel", "parallel", "arbitrary")),
        name="hgrn2",
    )(u32, u32, u32, u32, lb.reshape(1, BRANCH_WIDTH), norm_w.reshape(1, BRANCH_WIDTH).astype(F32))


def _merge_kernel(ya_ref, yb_ref, yc_ref, yd_ref, h_ref, wb_ref, wm_ref, bm_ref, o_ref):
    tn = o_ref.shape[1]
    h = h_ref[...]
    merged = None
    for br, y_ref in enumerate((ya_ref, yb_ref, yc_ref, yd_ref)):
        proj = jnp.dot(y_ref[...], wb_ref[br], preferred_element_type=F32)
        gate = jnp.concatenate(
            [jnp.dot(h[:, n * MERGE_BLOCK_DIM:(n + 1) * MERGE_BLOCK_DIM], wm_ref[br, n], preferred_element_type=F32)
             for n in range(tn // MERGE_BLOCK_DIM)], axis=1)
        term = jax.nn.sigmoid(gate + bm_ref[br:br + 1, :]) * proj
        merged = term if merged is None else merged + term
    o_ref[...] = merged.astype(o_ref.dtype)


def _merge(ys, h, wb, wm, bm, layer, tm=1024, tn=512):
    t = h.shape[0]
    tm = min(tm, t)
    y_spec = pl.BlockSpec((tm, BRANCH_WIDTH), lambda i, j: (i, 0))
    nmb = tn // MERGE_BLOCK_DIM
    return pl.pallas_call(
        _merge_kernel,
        out_shape=jax.ShapeDtypeStruct((t, D_MODEL), BF16),
        grid=(t // tm, D_MODEL // tn),
        in_specs=[y_spec, y_spec, y_spec, y_spec,
                  pl.BlockSpec((tm, tn), lambda i, j: (i, j)),
                  pl.BlockSpec((None, 4, BRANCH_WIDTH, tn), lambda i, j: (layer, 0, 0, j)),
                  pl.BlockSpec((None, 4, nmb, MERGE_BLOCK_DIM, MERGE_BLOCK_DIM), lambda i, j: (layer, 0, j, 0, 0)),
                  pl.BlockSpec((None, 4, tn), lambda i, j: (layer, 0, j))],
        out_specs=pl.BlockSpec((tm, tn), lambda i, j: (i, j)),
        compiler_params=_cparams(("parallel", "parallel")),
        name="gated_merge",
    )(*ys, h, wb, wm, bm)


def kernel(x, norm_w, w_in, fox_f_bias, conv_w, hgrn_gamma, hgrn_norm_w, w_branch, w_merge, b_merge, w_out, final_norm_w):
    batch, seq, d = x.shape
    depth = w_in.shape[0]
    assert d == D_MODEL and w_in.shape[-1] == IN_WIDTH and seq % FOX_TC == 0
    xf = x.reshape(batch * seq, d)
    w_in_p = _pad_in_proj(w_in)
    wb = w_branch.astype(BF16)
    wm = w_merge.astype(BF16)
    wo = w_out.astype(BF16)
    lower = _lower_bounds(hgrn_gamma)
    tabs = _rope_tables(seq)
    for layer in range(depth):
        h = _rmsnorm(xf, norm_w[layer], BF16)
        u32 = _in_proj(h, w_in_p, layer)
        qa, ka, vt, iq, ik, iwt = _dsa_prep(u32, tabs, seq)
        y_a = _dsa(iq, iwt, ik, qa, ka, vt, u32, batch, seq)
        y_b = _conv(u32, conv_w[layer].astype(F32), seq)
        ccol, crow = _fox_cum(u32, fox_f_bias[layer], batch, seq)
        fq, fk, fvt = _fox_prep(u32)
        y_c = _fox(fq, fk, fvt, u32, ccol, crow, batch, seq)
        y_d = _hgrn(u32, lower[layer], hgrn_norm_w[layer], batch, seq)
        merged = _merge((y_a, y_b, y_c, y_d), h, wb, wm, b_merge.astype(F32), layer)
        xf = _out_proj(merged, wo, layer, xf)
    return _rmsnorm(xf, final_norm_w, F32).reshape(batch, seq, d)
```

```python
import functools

import numpy as np
import jax
import jax.numpy as jnp
from jax import lax
from jax.experimental import pallas as pl
from jax.experimental.pallas import tpu as pltpu

F32 = jnp.float32
BF16 = jnp.bfloat16

D_MODEL = 4096
HEAD_DIM = 128
ROPE_THETA = 10000.0
NORM_EPS = 1e-6
Q_BLOCK = 256
BRANCH_WIDTH = D_MODEL // 4
N_HEADS = BRANCH_WIDTH // HEAD_DIM
IDX_HEADS = 16
IDX_DIM = 64
DSA_TOPK_MAX = 256
CONV_K = 3
HGRN_CHUNK = 64
HGRN_SUB = 16
HGRN_HEADS_PER_STEP = 8
MERGE_BLOCKS = 16
MERGE_BLOCK_DIM = D_MODEL // MERGE_BLOCKS

LANES = 128
VT_ROWS = HEAD_DIM + 16
VMEM_LIMIT = 56 * 1024 * 1024

_SRC_LAYOUT = (
    ("a_q", 1024), ("a_k", 128), ("a_v", 128), ("a_iq", 1024), ("a_ik", 64), ("a_iw", 16), ("a_g", 1024),
    ("b_b", 1024), ("b_c", 1024), ("b_x", 1024), ("b_g", 1024),
    ("c_q", 1024), ("c_k", 1024), ("c_v", 1024), ("c_f", 8), ("c_g", 1024),
    ("d_q", 1024), ("d_f", 1024), ("d_i", 1024), ("d_g", 1024),
)
_SRC_OFF = {}
_o = 0
for _n, _w in _SRC_LAYOUT:
    _SRC_OFF[_n] = (_o, _w)
    _o += _w
IN_WIDTH = _o

_BIG = ("a_q", "a_iq", "a_g", "b_b", "b_c", "b_x", "b_g", "c_q", "c_k", "c_v", "c_g", "d_q", "d_f", "d_i", "d_g")
BIG = {n: i for i, n in enumerate(_BIG)}
SMALL_BASE = len(_BIG) * 1024 // LANES
SLOT_AK, SLOT_AV, SLOT_AIKW, SLOT_CF = SMALL_BASE, SMALL_BASE + 1, SMALL_BASE + 2, SMALL_BASE + 3
PAD_WIDTH = (SMALL_BASE + 4) * LANES

INT_MIN = np.int32(-2 ** 31)
NT_DIMS = (((1,), (1,)), ((), ()))
TN_DIMS = (((0,), (0,)), ((), ()))


def _cparams(sem):
    return pltpu.CompilerParams(dimension_semantics=sem, vmem_limit_bytes=VMEM_LIMIT)


def _pad_in_proj_kernel(w_ref, o_ref):
    cols = w_ref.shape[1]

    def put(dst, name, width=None):
        o, w = _SRC_OFF[name]
        w = w if width is None else width
        o_ref[dst:dst + w, :] = w_ref[o:o + w, :].astype(BF16)
        return dst + w

    dst = 0
    for name in _BIG:
        dst = put(dst, name)
    dst = put(dst, "a_k")
    dst = put(dst, "a_v")
    dst = put(dst, "a_ik", IDX_DIM + IDX_HEADS)
    o_ref[dst:dst + LANES - IDX_DIM - IDX_HEADS, :] = jnp.zeros((LANES - IDX_DIM - IDX_HEADS, cols), BF16)
    dst = put(dst + LANES - IDX_DIM - IDX_HEADS, "c_f")
    o_ref[dst:dst + LANES - N_HEADS, :] = jnp.zeros((LANES - N_HEADS, cols), BF16)


def _pad_in_proj(w_in, tl=128):
    w_t = jnp.swapaxes(w_in, 1, 2)
    depth, n, d = w_t.shape
    return pl.pallas_call(
        _pad_in_proj_kernel,
        out_shape=jax.ShapeDtypeStruct((depth, PAD_WIDTH, d), BF16),
        grid=(depth, d // tl),
        in_specs=[pl.BlockSpec((None, n, tl), lambda l, i: (l, 0, i))],
        out_specs=pl.BlockSpec((None, PAD_WIDTH, tl), lambda l, i: (l, 0, i)),
        compiler_params=_cparams(("parallel", "parallel")),
        name="pad_in_proj",
    )(w_t)


def _lb_kernel(g_ref, o_ref):
    g = g_ref[...]
    e = jnp.exp(g - jnp.max(g, axis=0, keepdims=True))
    sm = e / jnp.sum(e, axis=0, keepdims=True)
    acc = jnp.zeros_like(sm[0:1])
    rows = []
    for layer in range(g.shape[0]):
        acc = acc + sm[layer:layer + 1]
        rows.append(acc - sm[0:1])
    o_ref[...] = jnp.concatenate(rows, axis=0)


def _lower_bounds(gamma):
    return pl.pallas_call(
        _lb_kernel, out_shape=jax.ShapeDtypeStruct(gamma.shape, F32), name="hgrn_lower_bounds",
    )(gamma.astype(F32))


def _rmsnorm_kernel(x_ref, w_ref, o_ref):
    x = x_ref[...]
    ms = jnp.mean(x * x, axis=-1, keepdims=True)
    o_ref[...] = (x * lax.rsqrt(ms + NORM_EPS) * w_ref[...]).astype(o_ref.dtype)


def _rmsnorm(x, w, out_dtype, tm=256):
    t, d = x.shape
    return pl.pallas_call(
        _rmsnorm_kernel,
        out_shape=jax.ShapeDtypeStruct((t, d), out_dtype),
        grid=(t // tm,),
        in_specs=[pl.BlockSpec((tm, d), lambda i: (i, 0)), pl.BlockSpec((1, d), lambda i: (0, 0))],
        out_specs=pl.BlockSpec((tm, d), lambda i: (i, 0)),
        compiler_params=_cparams(("parallel",)),
        name="rmsnorm",
    )(x, w.reshape(1, d))


def _in_proj_kernel(a_ref, b_ref, o_ref):
    o_ref[...] = lax.dot_general(a_ref[...], b_ref[...], NT_DIMS, preferred_element_type=F32)


def _in_proj(h, w, layer, tm=1024, tn=512):
    t, k = h.shape
    n = w.shape[1]
    tm = min(tm, t)
    return pl.pallas_call(
        _in_proj_kernel,
        out_shape=jax.ShapeDtypeStruct((t, n), F32),
        grid=(t // tm, n // tn),
        in_specs=[pl.BlockSpec((tm, k), lambda i, j: (i, 0)),
                  pl.BlockSpec((None, tn, k), lambda i, j: (layer, j, 0))],
        out_specs=pl.BlockSpec((tm, tn), lambda i, j: (i, j)),
        compiler_params=_cparams(("parallel", "parallel")),
        name="in_proj",
    )(h, w)


def _out_proj_kernel(a_ref, b_ref, r_ref, o_ref):
    o_ref[...] = r_ref[...] + jnp.dot(a_ref[...], b_ref[...], preferred_element_type=F32)


def _out_proj(a, w, layer, resid, tm=1024, tn=512):
    t, k = a.shape
    n = w.shape[2]
    tm = min(tm, t)
    return pl.pallas_call(
        _out_proj_kernel,
        out_shape=jax.ShapeDtypeStruct((t, n), F32),
        grid=(t // tm, n // tn),
        in_specs=[pl.BlockSpec((tm, k), lambda i, j: (i, 0)),
                  pl.BlockSpec((None, k, tn), lambda i, j: (layer, 0, j)),
                  pl.BlockSpec((tm, tn), lambda i, j: (i, j))],
        out_specs=pl.BlockSpec((tm, tn), lambda i, j: (i, j)),
        compiler_params=_cparams(("parallel", "parallel")),
        name="out_proj",
    )(a, w, resid)


def _rope_tables(seq):
    pos = jnp.arange(seq).astype(F32)

    def tables(dim):
        half = dim // 2
        inv = ROPE_THETA ** (-jnp.arange(half, dtype=F32) / half)
        ang = pos[:, None] * inv[None, :]
        cos, sin = jnp.cos(ang), jnp.sin(ang)
        reps = LANES // dim
        cos_t = jnp.tile(jnp.concatenate([cos, cos], axis=1), (1, reps))
        sin_t = jnp.tile(jnp.concatenate([-sin, sin], axis=1), (1, reps))
        return cos_t, sin_t

    return tables(HEAD_DIM) + tables(IDX_DIM)


def _dsa_prep_kernel(q_ref, iq_ref, k_ref, v_ref, ikw_ref, cos_ref, sin_ref, cosi_ref, sini_ref,
                     qo_ref, ko_ref, vto_ref, iqo_ref, iko_ref, iwo_ref):
    cos, sin = cos_ref[...], sin_ref[...]
    cosi, sini = cosi_ref[...], sini_ref[...]
    tm = cos.shape[0]

    def rot_head(x):
        return x * cos + pltpu.roll(x, HEAD_DIM // 2, axis=1) * sin

    lane = lax.broadcasted_iota(jnp.int32, (tm, LANES), 1)
    first_half = (lane % IDX_DIM) < (IDX_DIM // 2)

    def rot_idx(x):
        partner = jnp.where(first_half, pltpu.roll(x, LANES - IDX_DIM // 2, axis=1),
                            pltpu.roll(x, IDX_DIM // 2, axis=1))
        return x * cosi + partner * sini

    for h in range(N_HEADS):
        qo_ref[:, h * HEAD_DIM:(h + 1) * HEAD_DIM] = rot_head(q_ref[:, h * HEAD_DIM:(h + 1) * HEAD_DIM]).astype(BF16)
    ko_ref[...] = rot_head(k_ref[...]).astype(BF16)
    vto_ref[0:HEAD_DIM, :] = v_ref[...].T.astype(BF16)
    vto_ref[HEAD_DIM:VT_ROWS, :] = jnp.ones((VT_ROWS - HEAD_DIM, tm), BF16)
    for p in range(IDX_HEADS // 2):
        r = rot_idx(iq_ref[:, p * LANES:(p + 1) * LANES]).astype(BF16)
        iqo_ref[2 * p] = r[:, :IDX_DIM]
        iqo_ref[2 * p + 1] = r[:, IDX_DIM:]
    ikw = ikw_ref[...]
    iko_ref[...] = rot_idx(ikw)[:, :IDX_DIM].astype(BF16)
    iwo_ref[...] = ikw.T[IDX_DIM:IDX_DIM + IDX_HEADS, :] * (IDX_HEADS ** -0.5 * IDX_DIM ** -0.5)


def _dsa_prep(u32, tabs, seq):
    t = u32.shape[0]
    tm = DSA_TK
    nseq = seq // tm
    cos_a, sin_a, cos_i, sin_i = tabs
    tab_spec = pl.BlockSpec((tm, LANES), lambda i: (i % nseq, 0))
    return pl.pallas_call(
        _dsa_prep_kernel,
        out_shape=(jax.ShapeDtypeStruct((t, BRANCH_WIDTH), BF16),
                   jax.ShapeDtypeStruct((t, HEAD_DIM), BF16),
                   jax.ShapeDtypeStruct((t // tm, VT_ROWS, tm), BF16),
                   jax.ShapeDtypeStruct((IDX_HEADS, t, IDX_DIM), BF16),
                   jax.ShapeDtypeStruct((t, IDX_DIM), BF16),
                   jax.ShapeDtypeStruct((IDX_HEADS, t), F32)),
        grid=(t // tm,),
        in_specs=[pl.BlockSpec((tm, 1024), lambda i: (i, BIG["a_q"])),
                  pl.BlockSpec((tm, 1024), lambda i: (i, BIG["a_iq"])),
                  pl.BlockSpec((tm, LANES), lambda i: (i, SLOT_AK)),
                  pl.BlockSpec((tm, LANES), lambda i: (i, SLOT_AV)),
                  pl.BlockSpec((tm, LANES), lambda i: (i, SLOT_AIKW)),
                  tab_spec, tab_spec, tab_spec, tab_spec],
        out_specs=(pl.BlockSpec((tm, BRANCH_WIDTH), lambda i: (i, 0)),
                   pl.BlockSpec((tm, HEAD_DIM), lambda i: (i, 0)),
                   pl.BlockSpec((None, VT_ROWS, tm), lambda i: (i, 0, 0)),
                   pl.BlockSpec((IDX_HEADS, tm, IDX_DIM), lambda i: (0, i, 0)),
                   pl.BlockSpec((tm, IDX_DIM), lambda i: (i, 0)),
                   pl.BlockSpec((IDX_HEADS, tm), lambda i: (0, i))),
        compiler_params=_cparams(("parallel",)),
        name="dsa_prep",
    )(u32, u32, u32, u32, u32, cos_a, sin_a, cos_i, sin_i)


DSA_TK = 256


def _bit_transpose32(words):
    a = list(words)
    j, m = 16, 0x0000FFFF
    while j:
        k = 0
        while k < 32:
            t = (a[k] ^ lax.shift_right_logical(a[k + j], jnp.int32(j))) & jnp.int32(m)
            a[k] = a[k] ^ t
            a[k + j] = a[k + j] ^ (t << j)
            k = (k + j + 1) & ~j
        j >>= 1
        m = (m ^ (m << j)) & 0xFFFFFFFF if j else m
    return a


def _dsa_kernel(iq_ref, iw_ref, ik_ref, q_ref, k_ref, vt_ref, g_ref, o_ref,
                keys_ref, planes_ref, m_ref, acc_ref, *, topk, seq):
    qb = Q_BLOCK
    tk = DSA_TK
    i = pl.program_id(1)
    q0 = i * qb
    nch = (q0 + qb + tk - 1) // tk
    prow = planes_ref.shape[1]

    @pl.when((pl.program_id(0) == 0) & (i == 0))
    def _():
        planes_ref[...] = jnp.zeros(planes_ref.shape, jnp.int32)

    iw = iw_ref[...]
    iq2 = iq_ref[...].reshape(IDX_HEADS * qb, IDX_DIM)
    krow = lax.broadcasted_iota(jnp.int32, (tk, qb), 0)
    qcol = q0 + lax.broadcasted_iota(jnp.int32, (tk, qb), 1)

    def score_chunk(c):
        k0 = pl.multiple_of(c * tk, tk)
        logit = lax.dot_general(ik_ref[pl.ds(k0, tk), :], iq2, NT_DIMS, preferred_element_type=F32)
        s = jnp.zeros((tk, qb), F32)
        for h in range(IDX_HEADS):
            s = s + iw[h:h + 1, :] * jnp.maximum(logit[:, h * qb:(h + 1) * qb], 0.0)
        bits = lax.bitcast_convert_type(s, jnp.int32)
        key = bits ^ ((bits >> 31) & jnp.int32(0x7FFFFFFF))
        key = jnp.where(k0 + krow <= qcol, key, INT_MIN)
        keys_ref[c] = key
        ukey = key ^ INT_MIN
        planes = _bit_transpose32([ukey[8 * j:8 * j + 8, :] for j in range(32)])
        r0 = pl.multiple_of(c * 8, 8)
        for b in range(32):
            planes_ref[b, pl.ds(r0, 8), :] = planes[b]

    npair = (nch + 1) // 2

    def score_pair(cp, carry):
        score_chunk(2 * cp)
        score_chunk(2 * cp + 1)
        return carry

    lax.fori_loop(0, npair, score_pair, 0)

    zero = jnp.zeros((1, qb), jnp.int32)

    def popcount_rows(x):
        cnt = lax.population_count(x)
        return jnp.sum(jnp.sum(cnt.reshape(x.shape[0] // 8, 8, LANES), axis=0), axis=0, keepdims=True)

    def radix_select(lanes, rows):
        def select_bit(b, carry):
            live, n_above, prefix = carry
            plane = planes_ref[b, 0:rows, lanes]
            ones = live & plane
            c1 = popcount_rows(ones)
            take = n_above + c1 >= topk
            n_above = jnp.where(take, n_above, n_above + c1)
            live = jnp.where(take, ones, live & ~plane)
            prefix = jnp.where(take, prefix | jnp.left_shift(jnp.int32(1), 31 - b), prefix)
            return live, n_above, prefix

        live0 = jnp.where(lax.broadcasted_iota(jnp.int32, (rows, LANES), 0) < 16 * npair,
                          jnp.int32(-1), jnp.int32(0))
        z = jnp.zeros((1, LANES), jnp.int32)
        live, n_above, prefix = lax.fori_loop(0, 32, select_bit, (live0, z, z))
        return prefix ^ INT_MIN, n_above, popcount_rows(live)

    row_steps = sorted({r for r in (prow // 4, prow // 2, 3 * prow // 4, prow) if r % 8 == 0 and r > 0})
    which = sum((16 * npair > r).astype(jnp.int32) for r in row_steps[:-1])

    def branch(lanes, rows):
        return lambda _: radix_select(lanes, rows)

    groups = []
    for g in range(qb // LANES):
        lanes = slice(g * LANES, (g + 1) * LANES)
        groups.append(lax.switch(which, [branch(lanes, r) for r in row_steps], 0))
    tau, n_gt, n_eq = (jnp.concatenate(parts, axis=1) for parts in zip(*groups))
    need = topk - n_gt

    def count(pred):
        def body(c, acc):
            hit = jnp.where(pred(keys_ref[c], c * tk + krow), 1, 0)
            return acc + jnp.sum(hit.reshape(tk // 8, 8, qb), axis=0)

        acc = lax.fori_loop(0, nch, body, jnp.zeros((8, qb), jnp.int32))
        return jnp.sum(acc, axis=0, keepdims=True)

    def tie_search(_):
        def body(it, p):
            cand = p + jnp.left_shift(jnp.int32(1), 14 - it)
            below = count(lambda k, idx: (k == tau) & (idx < cand))
            return jnp.where(below < need, cand, p)
        return lax.fori_loop(0, 15, body, zero) + 1

    has_tie = jnp.max(jnp.where((n_eq > need) & (tau > INT_MIN), 1, 0)) > 0
    jlim = lax.cond(has_tie, tie_search, lambda _: jnp.full((1, qb), seq, jnp.int32), 0)

    q = q_ref[...]
    q2 = jnp.concatenate([q[:, h * HEAD_DIM:(h + 1) * HEAD_DIM] for h in range(N_HEADS)], axis=0)
    m_ref[...] = jnp.full(m_ref.shape, -jnp.inf, F32)
    acc_ref[...] = jnp.zeros(acc_ref.shape, F32)
    exp2_scale = HEAD_DIM ** -0.5 * np.log2(np.e)

    def attend(cp, carry):
        k0 = pl.multiple_of(cp * 2 * tk, 2 * tk)
        bias = []
        for c in (2 * cp, 2 * cp + 1):
            kc = keys_ref[c]
            sel = (kc > INT_MIN) & ((kc > tau) | ((kc == tau) & (c * tk + krow < jlim)))
            bias.append(jnp.where(sel, 0.0, -jnp.inf))
        bias = jnp.concatenate(bias, axis=0)
        s = lax.dot_general(k_ref[pl.ds(k0, 2 * tk), :], q2, NT_DIMS, preferred_element_type=F32)
        vt = jnp.concatenate([vt_ref[2 * cp], vt_ref[2 * cp + 1]], axis=1)
        m_prev = m_ref[...]
        ps, ms, alphas = [], [], []
        for h in range(N_HEADS):
            cols = slice(h * qb, (h + 1) * qb)
            z = s[:, cols] + bias
            m_new = jnp.maximum(m_prev[:, cols], jnp.max(z, axis=0, keepdims=True))
            m_safe = jnp.where(m_new == -jnp.inf, 0.0, m_new)
            ps.append(jnp.exp2((z - m_safe) * exp2_scale).astype(BF16))
            ms.append(m_new)
            alphas.append(jnp.exp2((m_prev[:, cols] - m_safe) * exp2_scale))
        m_ref[...] = jnp.concatenate(ms, axis=1)
        acc_ref[...] = jnp.concatenate(alphas, axis=1) * acc_ref[...] + jnp.dot(
            vt, jnp.concatenate(ps, axis=1), preferred_element_type=F32)
        return carry

    lax.fori_loop(0, npair, attend, 0)
    ot = acc_ref[0:HEAD_DIM, :] / acc_ref[HEAD_DIM:HEAD_DIM + 1, :]
    o = jnp.concatenate([ot[:, h * qb:(h + 1) * qb].T for h in range(N_HEADS)], axis=1)
    g = g_ref[...]
    o_ref[...] = (o * (g * jax.nn.sigmoid(g))).astype(o_ref.dtype)


def _dsa(iq, iwt, ik, qa, ka, vt, u32, batch, seq):
    t = qa.shape[0]
    nb = seq // Q_BLOCK
    topk = min(DSA_TOPK_MAX, seq // 4)
    kern = functools.partial(_dsa_kernel, topk=topk, seq=seq)
    return pl.pallas_call(
        kern,
        out_shape=jax.ShapeDtypeStruct((t, BRANCH_WIDTH), BF16),
        grid=(batch, nb),
        in_specs=[pl.BlockSpec((IDX_HEADS, Q_BLOCK, IDX_DIM), lambda b, i: (0, b * nb + i, 0)),
                  pl.BlockSpec((IDX_HEADS, Q_BLOCK), lambda b, i: (0, b * nb + i)),
                  pl.BlockSpec((seq, IDX_DIM), lambda b, i: (b, 0)),
                  pl.BlockSpec((Q_BLOCK, BRANCH_WIDTH), lambda b, i: (b * nb + i, 0)),
                  pl.BlockSpec((seq, HEAD_DIM), lambda b, i: (b, 0)),
                  pl.BlockSpec((seq // DSA_TK, VT_ROWS, DSA_TK), lambda b, i: (b, 0, 0)),
                  pl.BlockSpec((Q_BLOCK, 1024), lambda b, i: (b * nb + i, BIG["a_g"]))],
        out_specs=pl.BlockSpec((Q_BLOCK, BRANCH_WIDTH), lambda b, i: (b * nb + i, 0)),
        scratch_shapes=[pltpu.VMEM((seq // DSA_TK, DSA_TK, Q_BLOCK), jnp.int32),
                        pltpu.VMEM((32, seq // DSA_TK * 8, Q_BLOCK), jnp.int32),
                        pltpu.VMEM((1, N_HEADS * Q_BLOCK), F32),
                        pltpu.VMEM((VT_ROWS, N_HEADS * Q_BLOCK), F32)],
        compiler_params=_cparams(("arbitrary", "arbitrary")),
        name="dsa_attention",
    )(iq, iwt, ik, qa, ka, vt, u32)


def _conv_kernel(b_ref, c_ref, x_ref, g_ref, cp_ref, xp_ref, w_ref, o_ref, *, tiles_per_seq):
    i = pl.program_id(0)
    tm = b_ref.shape[0]
    u = c_ref[...] * x_ref[...]
    prev = cp_ref[...] * xp_ref[...]
    prev = jnp.where(i % tiles_per_seq == 0, 0.0, prev)
    row = lax.broadcasted_iota(jnp.int32, u.shape, 0)
    u1 = jnp.where(row == 0, prev[7:8, :], pltpu.roll(u, 1, axis=0))
    u2 = jnp.where(row == 0, prev[6:7, :], jnp.where(row == 1, prev[7:8, :], pltpu.roll(u, 2, axis=0)))
    w = w_ref[...]
    y = w[0:1, :] * u2 + w[1:2, :] * u1 + w[2:3, :] * u
    g = g_ref[...]
    o_ref[...] = (b_ref[...] * y * (g * jax.nn.sigmoid(g))).astype(o_ref.dtype)
    del tm


def _conv(u32, conv_w, seq, tm=512):
    t = u32.shape[0]
    tps = seq // tm
    kern = functools.partial(_conv_kernel, tiles_per_seq=tps)

    def big(name):
        return pl.BlockSpec((tm, 1024), lambda i: (i, BIG[name]))

    def halo(name):
        return pl.BlockSpec((8, 1024), lambda i: (jnp.maximum(i * (tm // 8) - 1, 0), BIG[name]))

    return pl.pallas_call(
        kern,
        out_shape=jax.ShapeDtypeStruct((t, BRANCH_WIDTH), BF16),
        grid=(t // tm,),
        in_specs=[big("b_b"), big("b_c"), big("b_x"), big("b_g"), halo("b_c"), halo("b_x"),
                  pl.BlockSpec((CONV_K, BRANCH_WIDTH), lambda i: (0, 0))],
        out_specs=pl.BlockSpec((tm, BRANCH_WIDTH), lambda i: (i, 0)),
        compiler_params=_cparams(("parallel",)),
        name="short_conv",
    )(u32, u32, u32, u32, u32, u32, conv_w)


FOX_TK = 512
FOX_TQ = 1024
FOX_TC = 1024
FOX_HEADS_PER_STEP = 2


def _fox_cum_kernel(cf_ref, bias_ref, ccol_ref, crow_ref, carry_ref):
    j = pl.program_id(1)

    @pl.when(j == 0)
    def _():
        carry_ref[...] = jnp.zeros_like(carry_ref)

    x = cf_ref[...] + bias_ref[...]
    log_f = jnp.minimum(x, 0.0) - jnp.log1p(jnp.exp(-jnp.abs(x)))
    tc = x.shape[0]
    tri = (lax.broadcasted_iota(jnp.int32, (tc, tc), 1) <= lax.broadcasted_iota(jnp.int32, (tc, tc), 0)).astype(F32)
    cs = jnp.dot(tri, log_f, precision=lax.Precision.HIGHEST, preferred_element_type=F32) + carry_ref[...]
    ccol_ref[...] = cs
    carry_ref[...] = cs[tc - 1:tc, :]
    cst = cs.T
    for part in range(tc // FOX_TQ):
        crow_ref[part] = cst[0:N_HEADS, part * FOX_TQ:(part + 1) * FOX_TQ]


def _fox_cum(u32, bias, batch, seq):
    t = u32.shape[0]
    tc = FOX_TC
    ns = seq // tc
    per = tc // FOX_TQ
    bias_p = jnp.zeros((1, LANES), F32).at[0, :N_HEADS].set(bias.astype(F32))
    return pl.pallas_call(
        _fox_cum_kernel,
        out_shape=(jax.ShapeDtypeStruct((t, LANES), F32),
                   jax.ShapeDtypeStruct((t // FOX_TQ, N_HEADS, FOX_TQ), F32)),
        grid=(batch, ns),
        in_specs=[pl.BlockSpec((tc, LANES), lambda b, j: (b * ns + j, SLOT_CF)),
                  pl.BlockSpec((1, LANES), lambda b, j: (0, 0))],
        out_specs=(pl.BlockSpec((tc, LANES), lambda b, j: (b * ns + j, 0)),
                   pl.BlockSpec((per, N_HEADS, FOX_TQ), lambda b, j: (b * ns + j, 0, 0))),
        scratch_shapes=[pltpu.VMEM((1, LANES), F32)],
        compiler_params=_cparams(("parallel", "arbitrary")),
        name="fox_cumsum",
    )(u32, bias_p)


def _fox_prep_kernel(q_ref, k_ref, v_ref, qo_ref, ko_ref, vto_ref):
    tm = q_ref.shape[0]
    qo_ref[...] = q_ref[...].astype(BF16)
    ko_ref[...] = k_ref[...].astype(BF16)
    for h in range(N_HEADS):
        vto_ref[h, 0:HEAD_DIM, :] = v_ref[:, h * HEAD_DIM:(h + 1) * HEAD_DIM].T.astype(BF16)
        vto_ref[h, HEAD_DIM:VT_ROWS, :] = jnp.ones((VT_ROWS - HEAD_DIM, tm), BF16)


def _fox_prep(u32):
    t = u32.shape[0]
    tm = FOX_TK

    def big(name):
        return pl.BlockSpec((tm, 1024), lambda i: (i, BIG[name]))

    row_spec = pl.BlockSpec((tm, BRANCH_WIDTH), lambda i: (i, 0))
    return pl.pallas_call(
        _fox_prep_kernel,
        out_shape=(jax.ShapeDtypeStruct((t, BRANCH_WIDTH), BF16),
                   jax.ShapeDtypeStruct((t, BRANCH_WIDTH), BF16),
                   jax.ShapeDtypeStruct((t // tm, N_HEADS, VT_ROWS, tm), BF16)),
        grid=(t // tm,),
        in_specs=[big("c_q"), big("c_k"), big("c_v")],
        out_specs=(row_spec, row_spec, pl.BlockSpec((None, N_HEADS, VT_ROWS, tm), lambda i: (i, 0, 0, 0))),
        compiler_params=_cparams(("parallel",)),
        name="fox_prep",
    )(u32, u32, u32)


def _fox_kernel(q_ref, k_ref, vt_ref, ccol_ref, crow_ref, g_ref, o_ref, ckb_ref, m_ref, acc_ref, *, seq):
    tq, tk, tc, hp = FOX_TQ, FOX_TK, FOX_TC, FOX_HEADS_PER_STEP
    h0 = pl.program_id(1) * hp
    q0 = pl.program_id(2) * tq
    scale = HEAD_DIM ** -0.5
    log2e = np.log2(np.e)

    @pl.when(pl.program_id(2) == 0)
    def _():
        lane = lax.broadcasted_iota(jnp.int32, (tc, LANES), 1)

        def fill(r, carry):
            r0 = pl.multiple_of(r * tc, tc)
            blk = ccol_ref[pl.ds(r0, tc), :]
            for hh in range(hp):
                col = jnp.sum(jnp.where(lane == h0 + hh, blk, 0.0), axis=1, keepdims=True)
                ckb_ref[hh, pl.ds(r0, tc), :] = jnp.broadcast_to(col * log2e, (tc, LANES))
            return carry

        lax.fori_loop(0, seq // tc, fill, 0)

    sub = lax.broadcasted_iota(jnp.int32, (N_HEADS, tq), 0)
    crow = crow_ref[...]
    cqs = [jnp.sum(jnp.where(sub == h0 + hh, crow, 0.0), axis=0, keepdims=True) * log2e for hh in range(hp)]
    m_ref[...] = jnp.full(m_ref.shape, -jnp.inf, F32)
    acc_ref[...] = jnp.zeros(acc_ref.shape, F32)
    krow = lax.broadcasted_iota(jnp.int32, (tk, LANES), 0)
    qlane = lax.broadcasted_iota(jnp.int32, (tk, LANES), 1)

    def head_step(hh, c, k0, masked):
        hcols = slice(hh * HEAD_DIM, (hh + 1) * HEAD_DIM)
        s = lax.dot_general(k_ref[pl.ds(k0, tk), hcols], q_ref[:, hcols], NT_DIMS, preferred_element_type=F32)
        ckb = ckb_ref[hh, pl.ds(k0, tk), :]
        cq = cqs[hh]
        m_prev = m_ref[hh]
        ps, ms, alphas = [], [], []
        for j in range(tq // LANES):
            cols = slice(j * LANES, (j + 1) * LANES)
            y = s[:, cols] * (scale * log2e) - ckb
            if masked:
                y = jnp.where(k0 + krow <= q0 + j * LANES + qlane, y, -jnp.inf)
            m_new = jnp.maximum(m_prev[:, cols], jnp.max(y, axis=0, keepdims=True) + cq[:, cols])
            ps.append(jnp.exp2(y - (m_new - cq[:, cols])).astype(BF16))
            ms.append(m_new)
            alphas.append(jnp.exp2(m_prev[:, cols] - m_new))
        m_ref[hh] = jnp.concatenate(ms, axis=1)
        acc_ref[hh] = jnp.concatenate(alphas, axis=1) * acc_ref[hh] + jnp.dot(
            vt_ref[c, hh], jnp.concatenate(ps, axis=1), preferred_element_type=F32)

    def make_body(masked):
        def body(c, carry):
            k0 = pl.multiple_of(c * tk, tk)
            for hh in range(hp):
                head_step(hh, c, k0, masked)
            return carry
        return body

    n_full = q0 // tk
    n_all = (q0 + tq + tk - 1) // tk
    lax.fori_loop(0, n_full, make_body(False), 0)
    lax.fori_loop(n_full, n_all, make_body(True), 0)
    for hh in range(hp):
        hcols = slice(hh * HEAD_DIM, (hh + 1) * HEAD_DIM)
        g = g_ref[:, hcols]
        ot = acc_ref[hh, 0:HEAD_DIM, :] / acc_ref[hh, HEAD_DIM:HEAD_DIM + 1, :]
        o_ref[:, hcols] = (ot.T * (g * jax.nn.sigmoid(g))).astype(o_ref.dtype)


def _fox(qb, kb, vt, u32, ccol, crow, batch, seq):
    t = qb.shape[0]
    tq, hp = FOX_TQ, FOX_HEADS_PER_STEP
    wide = hp * HEAD_DIM
    nq = seq // tq
    nkc = seq // FOX_TK
    cg = BIG["c_g"] * 1024 // wide
    kern = functools.partial(_fox_kernel, seq=seq)
    return pl.pallas_call(
        kern,
        out_shape=jax.ShapeDtypeStruct((t, BRANCH_WIDTH), BF16),
        grid=(batch, N_HEADS // hp, nq),
        in_specs=[pl.BlockSpec((tq, wide), lambda b, h, i: (b * nq + i, h)),
                  pl.BlockSpec((seq, wide), lambda b, h, i: (b, h)),
                  pl.BlockSpec((nkc, hp, VT_ROWS, FOX_TK), lambda b, h, i: (b, h, 0, 0)),
                  pl.BlockSpec((seq, LANES), lambda b, h, i: (b, 0)),
                  pl.BlockSpec((None, N_HEADS, tq), lambda b, h, i: (b * nq + i, 0, 0)),
                  pl.BlockSpec((tq, wide), lambda b, h, i: (b * nq + i, cg + h))],
        out_specs=pl.BlockSpec((tq, wide), lambda b, h, i: (b * nq + i, h)),
        scratch_shapes=[pltpu.VMEM((hp, seq, LANES), F32),
                        pltpu.VMEM((hp, 1, tq), F32), pltpu.VMEM((hp, VT_ROWS, tq), F32)],
        compiler_params=_cparams(("parallel", "parallel", "arbitrary")),
        name="fox_attention",
    )(qb, kb, vt, ccol, crow, u32)


def _hgrn_kernel(q_ref, f_ref, i_ref, g_ref, lb_ref, nw_ref, o_ref, st_ref):
    cs, sub = HGRN_CHUNK, HGRN_SUB
    ts = q_ref.shape[0]

    @pl.when(pl.program_id(2) == 0)
    def _():
        st_ref[...] = jnp.zeros_like(st_ref)

    tri = (lax.broadcasted_iota(jnp.int32, (cs, cs), 1) <= lax.broadcasted_iota(jnp.int32, (cs, cs), 0)).astype(F32)
    row_c = lax.broadcasted_iota(jnp.int32, (cs, HEAD_DIM), 0)
    row_h = lax.broadcasted_iota(jnp.int32, (sub // 2, HEAD_DIM), 0)
    lane_h = lax.broadcasted_iota(jnp.int32, (sub // 2, cs), 1)

    def chunk(ci, carry):
        for hh in range(HGRN_HEADS_PER_STEP):
            head_chunk(pl.multiple_of(ci * cs, cs), hh)
        return carry

    def head_chunk(r0, hh):
        cols = slice(hh * HEAD_DIM, (hh + 1) * HEAD_DIM)
        lb = lb_ref[:, cols]
        nw = nw_ref[:, cols]
        f = lb + (1.0 - lb) * jax.nn.sigmoid(f_ref[pl.ds(r0, cs), cols])
        kk = 1.0 - f
        bc = jnp.dot(tri, jnp.log(f), precision=lax.Precision.HIGHEST, preferred_element_type=F32)
        q = q_ref[pl.ds(r0, cs), cols]
        vb = i_ref[pl.ds(r0, cs), cols].astype(BF16)
        st = st_ref[hh]
        o = lax.dot_general((q * jnp.exp(bc)).astype(BF16), st.astype(BF16), NT_DIMS, preferred_element_type=F32)
        att_rows = []
        for si in range(cs // sub):
            lo = si * sub
            bi = bc[lo:lo + sub, :]
            qi = q[lo:lo + sub, :]
            att = jnp.zeros((sub, cs), F32)
            if si > 0:
                b0 = bc[lo - 1:lo, :]
                kt = kk * jnp.exp(jnp.where(row_c < lo, b0 - bc, -jnp.inf))
                qs = qi * jnp.exp(bi - b0)
                att = lax.dot_general(qs.astype(BF16), kt.astype(BF16), NT_DIMS, preferred_element_type=F32)
            half = sub // 2
            diag = [jnp.zeros((half, cs), F32), jnp.zeros((half, cs), F32)]
            for s in range(sub):
                bs = bc[lo + s:lo + s + 1, :]
                ks = kk[lo + s:lo + s + 1, :]
                for part in range(s // half, 2):
                    rows = slice(part * half, (part + 1) * half)
                    arg = bi[rows, :] - bs
                    if part == s // half:
                        arg = jnp.where(row_h >= s - part * half, arg, -jnp.inf)
                    p = qi[rows, :] * ks * jnp.exp(arg)
                    diag[part] = diag[part] + jnp.where(lane_h == lo + s, jnp.sum(p, axis=1, keepdims=True), 0.0)
            att_rows.append(att + jnp.concatenate(diag, axis=0))
        att = jnp.concatenate(att_rows, axis=0)
        o = o + jnp.dot(att.astype(BF16), vb, preferred_element_type=F32)
        bl = bc[cs - 1:cs, :]
        kd = kk * jnp.exp(bl - bc)
        st_ref[hh] = st * jnp.exp(bl) + lax.dot_general(vb, kd.astype(BF16), TN_DIMS, preferred_element_type=F32)
        y = o * lax.rsqrt(jnp.mean(o * o, axis=1, keepdims=True) + NORM_EPS) * nw
        g = g_ref[pl.ds(r0, cs), cols]
        o_ref[pl.ds(r0, cs), cols] = (y * (g * jax.nn.sigmoid(g))).astype(o_ref.dtype)

    lax.fori_loop(0, ts // cs, chunk, 0)


def _hgrn(u32, lb, norm_w, batch, seq, ts=512):
    t = u32.shape[0]
    ts = min(ts, seq)
    ns = seq // ts
    hp = HGRN_HEADS_PER_STEP
    wide = hp * HEAD_DIM
    dq, df, di, dg = (BIG[n] * 1024 // wide for n in ("d_q", "d_f", "d_i", "d_g"))

    def col(c0):
        return pl.BlockSpec((ts, wide), lambda b, h, j: (b * ns + j, c0 + h))

    vec = pl.BlockSpec((1, wide), lambda b, h, j: (0, h))
    return pl.pallas_call(
        _hgrn_kernel,
        out_shape=jax.ShapeDtypeStruct((t, BRANCH_WIDTH), BF16),
        grid=(batch, N_HEADS // hp, ns),
        in_specs=[col(dq), col(df), col(di), col(dg), vec, vec],
        out_specs=pl.BlockSpec((ts, wide), lambda b, h, j: (b * ns + j, h)),
        scratch_shapes=[pltpu.VMEM((hp, HEAD_DIM, HEAD_DIM), F32)],
        compiler_params=_cparams(("parallel", "parallel", "arbitrary")),
        name="hgrn2",
    )(u32, u32, u32, u32, lb.reshape(1, BRANCH_WIDTH), norm_w.reshape(1, BRANCH_WIDTH).astype(F32))


def _merge_kernel(ya_ref, yb_ref, yc_ref, yd_ref, h_ref, wb_ref, wm_ref, bm_ref, o_ref):
    tn = o_ref.shape[1]
    h = h_ref[...]
    merged = None
    for br, y_ref in enumerate((ya_ref, yb_ref, yc_ref, yd_ref)):
        proj = jnp.dot(y_ref[...], wb_ref[br], preferred_element_type=F32)
        gate = jnp.concatenate(
            [jnp.dot(h[:, n * MERGE_BLOCK_DIM:(n + 1) * MERGE_BLOCK_DIM], wm_ref[br, n], preferred_element_type=F32)
             for n in range(tn // MERGE_BLOCK_DIM)], axis=1)
        term = jax.nn.sigmoid(gate + bm_ref[br:br + 1, :]) * proj
        merged = term if merged is None else merged + term
    o_ref[...] = merged.astype(o_ref.dtype)


def _merge(ys, h, wb, wm, bm, layer, tm=1024, tn=512):
    t = h.shape[0]
    tm = min(tm, t)
    y_spec = pl.BlockSpec((tm, BRANCH_WIDTH), lambda i, j: (i, 0))
    nmb = tn // MERGE_BLOCK_DIM
    return pl.pallas_call(
        _merge_kernel,
        out_shape=jax.ShapeDtypeStruct((t, D_MODEL), BF16),
        grid=(t // tm, D_MODEL // tn),
        in_specs=[y_spec, y_spec, y_spec, y_spec,
                  pl.BlockSpec((tm, tn), lambda i, j: (i, j)),
                  pl.BlockSpec((None, 4, BRANCH_WIDTH, tn), lambda i, j: (layer, 0, 0, j)),
                  pl.BlockSpec((None, 4, nmb, MERGE_BLOCK_DIM, MERGE_BLOCK_DIM), lambda i, j: (layer, 0, j, 0, 0)),
                  pl.BlockSpec((None, 4, tn), lambda i, j: (layer, 0, j))],
        out_specs=pl.BlockSpec((tm, tn), lambda i, j: (i, j)),
        compiler_params=_cparams(("parallel", "parallel")),
        name="gated_merge",
    )(*ys, h, wb, wm, bm)


def kernel(x, norm_w, w_in, fox_f_bias, conv_w, hgrn_gamma, hgrn_norm_w, w_branch, w_merge, b_merge, w_out, final_norm_w):
    batch, seq, d = x.shape
    depth = w_in.shape[0]
    assert d == D_MODEL and w_in.shape[-1] == IN_WIDTH and seq % FOX_TC == 0
    xf = x.reshape(batch * seq, d)
    w_in_p = _pad_in_proj(w_in)
    wb = w_branch.astype(BF16)
    wm = w_merge.astype(BF16)
    wo = w_out.astype(BF16)
    lower = _lower_bounds(hgrn_gamma)
    tabs = _rope_tables(seq)
    for layer in range(depth):
        h = _rmsnorm(xf, norm_w[layer], BF16)
        u32 = _in_proj(h, w_in_p, layer)
        qa, ka, vt, iq, ik, iwt = _dsa_prep(u32, tabs, seq)
        y_a = _dsa(iq, iwt, ik, qa, ka, vt, u32, batch, seq)
        y_b = _conv(u32, conv_w[layer].astype(F32), seq)
        ccol, crow = _fox_cum(u32, fox_f_bias[layer], batch, seq)
        fq, fk, fvt = _fox_prep(u32)
        y_c = _fox(fq, fk, fvt, u32, ccol, crow, batch, seq)
        y_d = _hgrn(u32, lower[layer], hgrn_norm_w[layer], batch, seq)
        merged = _merge((y_a, y_b, y_c, y_d), h, wb, wm, b_merge.astype(F32), layer)
        xf = _out_proj(merged, wo, layer, xf)
    return _rmsnorm(xf, final_norm_w, F32).reshape(batch, seq, d)
```

```python
import functools

import numpy as np
import jax
import jax.numpy as jnp
from jax import lax
from jax.experimental import pallas as pl
from jax.experimental.pallas import tpu as pltpu

F32 = jnp.float32
BF16 = jnp.bfloat16

D_MODEL = 4096
HEAD_DIM = 128
ROPE_THETA = 10000.0
NORM_EPS = 1e-6
Q_BLOCK = 256
BRANCH_WIDTH = D_MODEL // 4
N_HEADS = BRANCH_WIDTH // HEAD_DIM
IDX_HEADS = 16
IDX_DIM = 64
DSA_TOPK_MAX = 256
CONV_K = 3
HGRN_CHUNK = 64
HGRN_SUB = 16
HGRN_HEADS_PER_STEP = 8
MERGE_BLOCKS = 16
MERGE_BLOCK_DIM = D_MODEL // MERGE_BLOCKS

LANES = 128
VT_ROWS = HEAD_DIM + 16
VMEM_LIMIT = 56 * 1024 * 1024

_SRC_LAYOUT = (
    ("a_q", 1024), ("a_k", 128), ("a_v", 128), ("a_iq", 1024), ("a_ik", 64), ("a_iw", 16), ("a_g", 1024),
    ("b_b", 1024), ("b_c", 1024), ("b_x", 1024), ("b_g", 1024),
    ("c_q", 1024), ("c_k", 1024), ("c_v", 1024), ("c_f", 8), ("c_g", 1024),
    ("d_q", 1024), ("d_f", 1024), ("d_i", 1024), ("d_g", 1024),
)
_SRC_OFF = {}
_o = 0
for _n, _w in _SRC_LAYOUT:
    _SRC_OFF[_n] = (_o, _w)
    _o += _w
IN_WIDTH = _o

_BIG = ("a_q", "a_iq", "a_g", "b_b", "b_c", "b_x", "b_g", "c_q", "c_k", "c_v", "c_g", "d_q", "d_f", "d_i", "d_g")
BIG = {n: i for i, n in enumerate(_BIG)}
SMALL_BASE = len(_BIG) * 1024 // LANES
SLOT_AK, SLOT_AV, SLOT_AIKW, SLOT_CF = SMALL_BASE, SMALL_BASE + 1, SMALL_BASE + 2, SMALL_BASE + 3
PAD_WIDTH = (SMALL_BASE + 4) * LANES

INT_MIN = np.int32(-2 ** 31)
NT_DIMS = (((1,), (1,)), ((), ()))
TN_DIMS = (((0,), (0,)), ((), ()))


def _cparams(sem):
    return pltpu.CompilerParams(dimension_semantics=sem, vmem_limit_bytes=VMEM_LIMIT)


def _pad_in_proj_kernel(w_ref, o_ref):
    cols = w_ref.shape[1]

    def put(dst, name, width=None):
        o, w = _SRC_OFF[name]
        w = w if width is None else width
        o_ref[dst:dst + w, :] = w_ref[o:o + w, :].astype(BF16)
        return dst + w

    dst = 0
    for name in _BIG:
        dst = put(dst, name)
    dst = put(dst, "a_k")
    dst = put(dst, "a_v")
    dst = put(dst, "a_ik", IDX_DIM + IDX_HEADS)
    o_ref[dst:dst + LANES - IDX_DIM - IDX_HEADS, :] = jnp.zeros((LANES - IDX_DIM - IDX_HEADS, cols), BF16)
    dst = put(dst + LANES - IDX_DIM - IDX_HEADS, "c_f")
    o_ref[dst:dst + LANES - N_HEADS, :] = jnp.zeros((LANES - N_HEADS, cols), BF16)


def _pad_in_proj(w_in, tl=128):
    w_t = jnp.swapaxes(w_in, 1, 2)
    depth, n, d = w_t.shape
    return pl.pallas_call(
        _pad_in_proj_kernel,
        out_shape=jax.ShapeDtypeStruct((depth, PAD_WIDTH, d), BF16),
        grid=(depth, d // tl),
        in_specs=[pl.BlockSpec((None, n, tl), lambda l, i: (l, 0, i))],
        out_specs=pl.BlockSpec((None, PAD_WIDTH, tl), lambda l, i: (l, 0, i)),
        compiler_params=_cparams(("parallel", "parallel")),
        name="pad_in_proj",
    )(w_t)


def _lb_kernel(g_ref, o_ref):
    g = g_ref[...]
    e = jnp.exp(g - jnp.max(g, axis=0, keepdims=True))
    sm = e / jnp.sum(e, axis=0, keepdims=True)
    acc = jnp.zeros_like(sm[0:1])
    rows = []
    for layer in range(g.shape[0]):
        acc = acc + sm[layer:layer + 1]
        rows.append(acc - sm[0:1])
    o_ref[...] = jnp.concatenate(rows, axis=0)


def _lower_bounds(gamma):
    return pl.pallas_call(
        _lb_kernel, out_shape=jax.ShapeDtypeStruct(gamma.shape, F32), name="hgrn_lower_bounds",
    )(gamma.astype(F32))


def _rmsnorm_kernel(x_ref, w_ref, o_ref):
    x = x_ref[...]
    ms = jnp.mean(x * x, axis=-1, keepdims=True)
    o_ref[...] = (x * lax.rsqrt(ms + NORM_EPS) * w_ref[...]).astype(o_ref.dtype)


def _rmsnorm(x, w, out_dtype, tm=256):
    t, d = x.shape
    return pl.pallas_call(
        _rmsnorm_kernel,
        out_shape=jax.ShapeDtypeStruct((t, d), out_dtype),
        grid=(t // tm,),
        in_specs=[pl.BlockSpec((tm, d), lambda i: (i, 0)), pl.BlockSpec((1, d), lambda i: (0, 0))],
        out_specs=pl.BlockSpec((tm, d), lambda i: (i, 0)),
        compiler_params=_cparams(("parallel",)),
        name="rmsnorm",
    )(x, w.reshape(1, d))


def _in_proj_kernel(a_ref, b_ref, o_ref):
    o_ref[...] = lax.dot_general(a_ref[...], b_ref[...], NT_DIMS, preferred_element_type=F32)


def _in_proj(h, w, layer, tm=1024, tn=512):
    t, k = h.shape
    n = w.shape[1]
    tm = min(tm, t)
    return pl.pallas_call(
        _in_proj_kernel,
        out_shape=jax.ShapeDtypeStruct((t, n), F32),
        grid=(t // tm, n // tn),
        in_specs=[pl.BlockSpec((tm, k), lambda i, j: (i, 0)),
                  pl.BlockSpec((None, tn, k), lambda i, j: (layer, j, 0))],
        out_specs=pl.BlockSpec((tm, tn), lambda i, j: (i, j)),
        compiler_params=_cparams(("parallel", "parallel")),
        name="in_proj",
    )(h, w)


def _out_proj_kernel(a_ref, b_ref, r_ref, o_ref):
    o_ref[...] = r_ref[...] + jnp.dot(a_ref[...], b_ref[...], preferred_element_type=F32)


def _out_proj(a, w, layer, resid, tm=1024, tn=512):
    t, k = a.shape
    n = w.shape[2]
    tm = min(tm, t)
    return pl.pallas_call(
        _out_proj_kernel,
        out_shape=jax.ShapeDtypeStruct((t, n), F32),
        grid=(t // tm, n // tn),
        in_specs=[pl.BlockSpec((tm, k), lambda i, j: (i, 0)),
                  pl.BlockSpec((None, k, tn), lambda i, j: (layer, 0, j)),
                  pl.BlockSpec((tm, tn), lambda i, j: (i, j))],
        out_specs=pl.BlockSpec((tm, tn), lambda i, j: (i, j)),
        compiler_params=_cparams(("parallel", "parallel")),
        name="out_proj",
    )(a, w, resid)


def _rope_tables(seq):
    pos = jnp.arange(seq).astype(F32)

    def tables(dim):
        half = dim // 2
        inv = ROPE_THETA ** (-jnp.arange(half, dtype=F32) / half)
        ang = pos[:, None] * inv[None, :]
        cos, sin = jnp.cos(ang), jnp.sin(ang)
        reps = LANES // dim
        cos_t = jnp.tile(jnp.concatenate([cos, cos], axis=1), (1, reps))
        sin_t = jnp.tile(jnp.concatenate([-sin, sin], axis=1), (1, reps))
        return cos_t, sin_t

    return tables(HEAD_DIM) + tables(IDX_DIM)


def _dsa_prep_kernel(q_ref, iq_ref, k_ref, v_ref, ikw_ref, cos_ref, sin_ref, cosi_ref, sini_ref,
                     qo_ref, ko_ref, vto_ref, iqo_ref, iko_ref, iwo_ref):
    cos, sin = cos_ref[...], sin_ref[...]
    cosi, sini = cosi_ref[...], sini_ref[...]
    tm = cos.shape[0]

    def rot_head(x):
        return x * cos + pltpu.roll(x, HEAD_DIM // 2, axis=1) * sin

    lane = lax.broadcasted_iota(jnp.int32, (tm, LANES), 1)
    first_half = (lane % IDX_DIM) < (IDX_DIM // 2)

    def rot_idx(x):
        partner = jnp.where(first_half, pltpu.roll(x, LANES - IDX_DIM // 2, axis=1),
                            pltpu.roll(x, IDX_DIM // 2, axis=1))
        return x * cosi + partner * sini

    for h in range(N_HEADS):
        qo_ref[:, h * HEAD_DIM:(h + 1) * HEAD_DIM] = rot_head(q_ref[:, h * HEAD_DIM:(h + 1) * HEAD_DIM]).astype(BF16)
    ko_ref[...] = rot_head(k_ref[...]).astype(BF16)
    vto_ref[0:HEAD_DIM, :] = v_ref[...].T.astype(BF16)
    vto_ref[HEAD_DIM:VT_ROWS, :] = jnp.ones((VT_ROWS - HEAD_DIM, tm), BF16)
    for p in range(IDX_HEADS // 2):
        r = rot_idx(iq_ref[:, p * LANES:(p + 1) * LANES]).astype(BF16)
        iqo_ref[2 * p] = r[:, :IDX_DIM]
        iqo_ref[2 * p + 1] = r[:, IDX_DIM:]
    ikw = ikw_ref[...]
    iko_ref[...] = rot_idx(ikw)[:, :IDX_DIM].astype(BF16)
    iwo_ref[...] = ikw.T[IDX_DIM:IDX_DIM + IDX_HEADS, :] * (IDX_HEADS ** -0.5 * IDX_DIM ** -0.5)


def _dsa_prep(u32, tabs, seq):
    t = u32.shape[0]
    tm = DSA_TK
    nseq = seq // tm
    cos_a, sin_a, cos_i, sin_i = tabs
    tab_spec = pl.BlockSpec((tm, LANES), lambda i: (i % nseq, 0))
    return pl.pallas_call(
        _dsa_prep_kernel,
        out_shape=(jax.ShapeDtypeStruct((t, BRANCH_WIDTH), BF16),
                   jax.ShapeDtypeStruct((t, HEAD_DIM), BF16),
                   jax.ShapeDtypeStruct((t // tm, VT_ROWS, tm), BF16),
                   jax.ShapeDtypeStruct((IDX_HEADS, t, IDX_DIM), BF16),
                   jax.ShapeDtypeStruct((t, IDX_DIM), BF16),
                   jax.ShapeDtypeStruct((IDX_HEADS, t), F32)),
        grid=(t // tm,),
        in_specs=[pl.BlockSpec((tm, 1024), lambda i: (i, BIG["a_q"])),
                  pl.BlockSpec((tm, 1024), lambda i: (i, BIG["a_iq"])),
                  pl.BlockSpec((tm, LANES), lambda i: (i, SLOT_AK)),
                  pl.BlockSpec((tm, LANES), lambda i: (i, SLOT_AV)),
                  pl.BlockSpec((tm, LANES), lambda i: (i, SLOT_AIKW)),
                  tab_spec, tab_spec, tab_spec, tab_spec],
        out_specs=(pl.BlockSpec((tm, BRANCH_WIDTH), lambda i: (i, 0)),
                   pl.BlockSpec((tm, HEAD_DIM), lambda i: (i, 0)),
                   pl.BlockSpec((None, VT_ROWS, tm), lambda i: (i, 0, 0)),
                   pl.BlockSpec((IDX_HEADS, tm, IDX_DIM), lambda i: (0, i, 0)),
                   pl.BlockSpec((tm, IDX_DIM), lambda i: (i, 0)),
                   pl.BlockSpec((IDX_HEADS, tm), lambda i: (0, i))),
        compiler_params=_cparams(("parallel",)),
        name="dsa_prep",
    )(u32, u32, u32, u32, u32, cos_a, sin_a, cos_i, sin_i)


DSA_TK = 256


def _bit_transpose32(words):
    a = list(words)
    j, m = 16, 0x0000FFFF
    while j:
        k = 0
        while k < 32:
            t = (a[k] ^ lax.shift_right_logical(a[k + j], jnp.int32(j))) & jnp.int32(m)
            a[k] = a[k] ^ t
            a[k + j] = a[k + j] ^ (t << j)
            k = (k + j + 1) & ~j
        j >>= 1
        m = (m ^ (m << j)) & 0xFFFFFFFF if j else m
    return a


def _dsa_kernel(iq_ref, iw_ref, ik_ref, q_ref, k_ref, vt_ref, g_ref, o_ref,
                keys_ref, planes_ref, m_ref, acc_ref, *, topk, seq):
    qb = Q_BLOCK
    tk = DSA_TK
    i = pl.program_id(1)
    q0 = i * qb
    nch = (q0 + qb + tk - 1) // tk
    prow = planes_ref.shape[1]

    @pl.when((pl.program_id(0) == 0) & (i == 0))
    def _():
        planes_ref[...] = jnp.zeros(planes_ref.shape, jnp.int32)

    iw = iw_ref[...]
    iq2 = iq_ref[...].reshape(IDX_HEADS * qb, IDX_DIM)
    krow = lax.broadcasted_iota(jnp.int32, (tk, qb), 0)
    qcol = q0 + lax.broadcasted_iota(jnp.int32, (tk, qb), 1)

    def score_chunk(c):
        k0 = pl.multiple_of(c * tk, tk)
        logit = lax.dot_general(ik_ref[pl.ds(k0, tk), :], iq2, NT_DIMS, preferred_element_type=F32)
        s = jnp.zeros((tk, qb), F32)
        for h in range(IDX_HEADS):
            s = s + iw[h:h + 1, :] * jnp.maximum(logit[:, h * qb:(h + 1) * qb], 0.0)
        bits = lax.bitcast_convert_type(s, jnp.int32)
        key = bits ^ ((bits >> 31) & jnp.int32(0x7FFFFFFF))
        key = jnp.where(k0 + krow <= qcol, key, INT_MIN)
        keys_ref[c] = key
        ukey = key ^ INT_MIN
        planes = _bit_transpose32([ukey[8 * j:8 * j + 8, :] for j in range(32)])
        r0 = pl.multiple_of(c * 8, 8)
        for b in range(32):
            planes_ref[b, pl.ds(r0, 8), :] = planes[b]

    npair = (nch + 1) // 2

    def score_pair(cp, carry):
        score_chunk(2 * cp)
        score_chunk(2 * cp + 1)
        return carry

    lax.fori_loop(0, npair, score_pair, 0)

    zero = jnp.zeros((1, qb), jnp.int32)

    def popcount_rows(x):
        cnt = lax.population_count(x)
        return jnp.sum(jnp.sum(cnt.reshape(x.shape[0] // 8, 8, LANES), axis=0), axis=0, keepdims=True)

    def radix_select(lanes, rows):
        def select_bit(b, carry):
            live, n_above, prefix = carry
            plane = planes_ref[b, 0:rows, lanes]
            ones = live & plane
            c1 = popcount_rows(ones)
            take = n_above + c1 >= topk
            n_above = jnp.where(take, n_above, n_above + c1)
            live = jnp.where(take, ones, live & ~plane)
            prefix = jnp.where(take, prefix | jnp.left_shift(jnp.int32(1), 31 - b), prefix)
            return live, n_above, prefix

        live0 = jnp.where(lax.broadcasted_iota(jnp.int32, (rows, LANES), 0) < 16 * npair,
                          jnp.int32(-1), jnp.int32(0))
        z = jnp.zeros((1, LANES), jnp.int32)
        live, n_above, prefix = lax.fori_loop(0, 32, select_bit, (live0, z, z))
        return prefix ^ INT_MIN, n_above, popcount_rows(live)

    row_steps = sorted({r for r in (prow // 4, prow // 2, 3 * prow // 4, prow) if r % 8 == 0 and r > 0})
    which = sum((16 * npair > r).astype(jnp.int32) for r in row_steps[:-1])

    def branch(lanes, rows):
        return lambda _: radix_select(lanes, rows)

    groups = []
    for g in range(qb // LANES):
        lanes = slice(g * LANES, (g + 1) * LANES)
        groups.append(lax.switch(which, [branch(lanes, r) for r in row_steps], 0))
    tau, n_gt, n_eq = (jnp.concatenate(parts, axis=1) for parts in zip(*groups))
    need = topk - n_gt

    def count(pred):
        def body(c, acc):
            hit = jnp.where(pred(keys_ref[c], c * tk + krow), 1, 0)
            return acc + jnp.sum(hit.reshape(tk // 8, 8, qb), axis=0)

        acc = lax.fori_loop(0, nch, body, jnp.zeros((8, qb), jnp.int32))
        return jnp.sum(acc, axis=0, keepdims=True)

    def tie_search(_):
        def body(it, p):
            cand = p + jnp.left_shift(jnp.int32(1), 14 - it)
            below = count(lambda k, idx: (k == tau) & (idx < cand))
            return jnp.where(below < need, cand, p)
        return lax.fori_loop(0, 15, body, zero) + 1

    has_tie = jnp.max(jnp.where((n_eq > need) & (tau > INT_MIN), 1, 0)) > 0
    jlim = lax.cond(has_tie, tie_search, lambda _: jnp.full((1, qb), seq, jnp.int32), 0)

    q = q_ref[...]
    q2 = jnp.concatenate([q[:, h * HEAD_DIM:(h + 1) * HEAD_DIM] for h in range(N_HEADS)], axis=0)
    m_ref[...] = jnp.full(m_ref.shape, -jnp.inf, F32)
    acc_ref[...] = jnp.zeros(acc_ref.shape, F32)
    exp2_scale = HEAD_DIM ** -0.5 * np.log2(np.e)

    def attend(cp, carry):
        k0 = pl.multiple_of(cp * 2 * tk, 2 * tk)
        bias = []
        for c in (2 * cp, 2 * cp + 1):
            kc = keys_ref[c]
            sel = (kc > INT_MIN) & ((kc > tau) | ((kc == tau) & (c * tk + krow < jlim)))
            bias.append(jnp.where(sel, 0.0, -jnp.inf))
        bias = jnp.concatenate(bias, axis=0)
        s = lax.dot_general(k_ref[pl.ds(k0, 2 * tk), :], q2, NT_DIMS, preferred_element_type=F32)
        vt = jnp.concatenate([vt_ref[2 * cp], vt_ref[2 * cp + 1]], axis=1)
        m_prev = m_ref[...]
        ps, ms, alphas = [], [], []
        for h in range(N_HEADS):
            cols = slice(h * qb, (h + 1) * qb)
            z = s[:, cols] + bias
            m_new = jnp.maximum(m_prev[:, cols], jnp.max(z, axis=0, keepdims=True))
            m_safe = jnp.where(m_new == -jnp.inf, 0.0, m_new)
            ps.append(jnp.exp2((z - m_safe) * exp2_scale).astype(BF16))
            ms.append(m_new)
            alphas.append(jnp.exp2((m_prev[:, cols] - m_safe) * exp2_scale))
        m_ref[...] = jnp.concatenate(ms, axis=1)
        acc_ref[...] = jnp.concatenate(alphas, axis=1) * acc_ref[...] + jnp.dot(
            vt, jnp.concatenate(ps, axis=1), preferred_element_type=F32)
        return carry

    lax.fori_loop(0, npair, attend, 0)
    ot = acc_ref[0:HEAD_DIM, :] / acc_ref[HEAD_DIM:HEAD_DIM + 1, :]
    o = jnp.concatenate([ot[:, h * qb:(h + 1) * qb].T for h in range(N_HEADS)], axis=1)
    g = g_ref[...]
    o_ref[...] = (o * (g * jax.nn.sigmoid(g))).astype(o_ref.dtype)


def _dsa(iq, iwt, ik, qa, ka, vt, u32, batch, seq):
    t = qa.shape[0]
    nb = seq // Q_BLOCK
    topk = min(DSA_TOPK_MAX, seq // 4)
    kern = functools.partial(_dsa_kernel, topk=topk, seq=seq)
    return pl.pallas_call(
        kern,
        out_shape=jax.ShapeDtypeStruct((t, BRANCH_WIDTH), BF16),
        grid=(batch, nb),
        in_specs=[pl.BlockSpec((IDX_HEADS, Q_BLOCK, IDX_DIM), lambda b, i: (0, b * nb + i, 0)),
                  pl.BlockSpec((IDX_HEADS, Q_BLOCK), lambda b, i: (0, b * nb + i)),
                  pl.BlockSpec((seq, IDX_DIM), lambda b, i: (b, 0)),
                  pl.BlockSpec((Q_BLOCK, BRANCH_WIDTH), lambda b, i: (b * nb + i, 0)),
                  pl.BlockSpec((seq, HEAD_DIM), lambda b, i: (b, 0)),
                  pl.BlockSpec((seq // DSA_TK, VT_ROWS, DSA_TK), lambda b, i: (b, 0, 0)),
                  pl.BlockSpec((Q_BLOCK, 1024), lambda b, i: (b * nb + i, BIG["a_g"]))],
        out_specs=pl.BlockSpec((Q_BLOCK, BRANCH_WIDTH), lambda b, i: (b * nb + i, 0)),
        scratch_shapes=[pltpu.VMEM((seq // DSA_TK, DSA_TK, Q_BLOCK), jnp.int32),
                        pltpu.VMEM((32, seq // DSA_TK * 8, Q_BLOCK), jnp.int32),
                        pltpu.VMEM((1, N_HEADS * Q_BLOCK), F32),
                        pltpu.VMEM((VT_ROWS, N_HEADS * Q_BLOCK), F32)],
        compiler_params=_cparams(("arbitrary", "arbitrary")),
        name="dsa_attention",
    )(iq, iwt, ik, qa, ka, vt, u32)


def _conv_kernel(b_ref, c_ref, x_ref, g_ref, cp_ref, xp_ref, w_ref, o_ref, *, tiles_per_seq):
    i = pl.program_id(0)
    tm = b_ref.shape[0]
    u = c_ref[...] * x_ref[...]
    prev = cp_ref[...] * xp_ref[...]
    prev = jnp.where(i % tiles_per_seq == 0, 0.0, prev)
    row = lax.broadcasted_iota(jnp.int32, u.shape, 0)
    u1 = jnp.where(row == 0, prev[7:8, :], pltpu.roll(u, 1, axis=0))
    u2 = jnp.where(row == 0, prev[6:7, :], jnp.where(row == 1, prev[7:8, :], pltpu.roll(u, 2, axis=0)))
    w = w_ref[...]
    y = w[0:1, :] * u2 + w[1:2, :] * u1 + w[2:3, :] * u
    g = g_ref[...]
    o_ref[...] = (b_ref[...] * y * (g * jax.nn.sigmoid(g))).astype(o_ref.dtype)
    del tm


def _conv(u32, conv_w, seq, tm=512):
    t = u32.shape[0]
    tps = seq // tm
    kern = functools.partial(_conv_kernel, tiles_per_seq=tps)

    def big(name):
        return pl.BlockSpec((tm, 1024), lambda i: (i, BIG[name]))

    def halo(name):
        return pl.BlockSpec((8, 1024), lambda i: (jnp.maximum(i * (tm // 8) - 1, 0), BIG[name]))

    return pl.pallas_call(
        kern,
        out_shape=jax.ShapeDtypeStruct((t, BRANCH_WIDTH), BF16),
        grid=(t // tm,),
        in_specs=[big("b_b"), big("b_c"), big("b_x"), big("b_g"), halo("b_c"), halo("b_x"),
                  pl.BlockSpec((CONV_K, BRANCH_WIDTH), lambda i: (0, 0))],
        out_specs=pl.BlockSpec((tm, BRANCH_WIDTH), lambda i: (i, 0)),
        compiler_params=_cparams(("parallel",)),
        name="short_conv",
    )(u32, u32, u32, u32, u32, u32, conv_w)


FOX_TK = 512
FOX_TQ = 1024
FOX_TC = 1024
FOX_HEADS_PER_STEP = 4


def _fox_cum_kernel(cf_ref, bias_ref, ccol_ref, crow_ref, carry_ref):
    j = pl.program_id(1)

    @pl.when(j == 0)
    def _():
        carry_ref[...] = jnp.zeros_like(carry_ref)

    x = cf_ref[...] + bias_ref[...]
    log_f = jnp.minimum(x, 0.0) - jnp.log1p(jnp.exp(-jnp.abs(x)))
    tc = x.shape[0]
    tri = (lax.broadcasted_iota(jnp.int32, (tc, tc), 1) <= lax.broadcasted_iota(jnp.int32, (tc, tc), 0)).astype(F32)
    cs = jnp.dot(tri, log_f, precision=lax.Precision.HIGHEST, preferred_element_type=F32) + carry_ref[...]
    ccol_ref[...] = cs
    carry_ref[...] = cs[tc - 1:tc, :]
    cst = cs.T
    for part in range(tc // FOX_TQ):
        crow_ref[part] = cst[0:N_HEADS, part * FOX_TQ:(part + 1) * FOX_TQ]


def _fox_cum(u32, bias, batch, seq):
    t = u32.shape[0]
    tc = FOX_TC
    ns = seq // tc
    per = tc // FOX_TQ
    bias_p = jnp.zeros((1, LANES), F32).at[0, :N_HEADS].set(bias.astype(F32))
    return pl.pallas_call(
        _fox_cum_kernel,
        out_shape=(jax.ShapeDtypeStruct((t, LANES), F32),
                   jax.ShapeDtypeStruct((t // FOX_TQ, N_HEADS, FOX_TQ), F32)),
        grid=(batch, ns),
        in_specs=[pl.BlockSpec((tc, LANES), lambda b, j: (b * ns + j, SLOT_CF)),
                  pl.BlockSpec((1, LANES), lambda b, j: (0, 0))],
        out_specs=(pl.BlockSpec((tc, LANES), lambda b, j: (b * ns + j, 0)),
                   pl.BlockSpec((per, N_HEADS, FOX_TQ), lambda b, j: (b * ns + j, 0, 0))),
        scratch_shapes=[pltpu.VMEM((1, LANES), F32)],
        compiler_params=_cparams(("parallel", "arbitrary")),
        name="fox_cumsum",
    )(u32, bias_p)


def _fox_prep_kernel(q_ref, k_ref, v_ref, qo_ref, ko_ref, vto_ref):
    tm = q_ref.shape[0]
    qo_ref[...] = q_ref[...].astype(BF16)
    ko_ref[...] = k_ref[...].astype(BF16)
    for h in range(N_HEADS):
        vto_ref[h, 0:HEAD_DIM, :] = v_ref[:, h * HEAD_DIM:(h + 1) * HEAD_DIM].T.astype(BF16)
        vto_ref[h, HEAD_DIM:VT_ROWS, :] = jnp.ones((VT_ROWS - HEAD_DIM, tm), BF16)


def _fox_prep(u32):
    t = u32.shape[0]
    tm = FOX_TK

    def big(name):
        return pl.BlockSpec((tm, 1024), lambda i: (i, BIG[name]))

    row_spec = pl.BlockSpec((tm, BRANCH_WIDTH), lambda i: (i, 0))
    return pl.pallas_call(
        _fox_prep_kernel,
        out_shape=(jax.ShapeDtypeStruct((t, BRANCH_WIDTH), BF16),
                   jax.ShapeDtypeStruct((t, BRANCH_WIDTH), BF16),
                   jax.ShapeDtypeStruct((t // tm, N_HEADS, VT_ROWS, tm), BF16)),
        grid=(t // tm,),
        in_specs=[big("c_q"), big("c_k"), big("c_v")],
        out_specs=(row_spec, row_spec, pl.BlockSpec((None, N_HEADS, VT_ROWS, tm), lambda i: (i, 0, 0, 0))),
        compiler_params=_cparams(("parallel",)),
        name="fox_prep",
    )(u32, u32, u32)


def _fox_kernel(q_ref, k_ref, vt_ref, ccol_ref, crow_ref, g_ref, o_ref, m_ref, acc_ref):
    tq, tk, hp = FOX_TQ, FOX_TK, FOX_HEADS_PER_STEP
    h0 = pl.program_id(1) * hp
    q0 = pl.program_id(2) * tq
    scale = HEAD_DIM ** -0.5
    log2e = np.log2(np.e)
    head_lane = lax.broadcasted_iota(jnp.int32, (tk, LANES), 1)

    sub = lax.broadcasted_iota(jnp.int32, (N_HEADS, tq), 0)
    crow = crow_ref[...]
    cqs = [jnp.sum(jnp.where(sub == h0 + hh, crow, 0.0), axis=0, keepdims=True) * log2e for hh in range(hp)]
    m_ref[...] = jnp.full(m_ref.shape, -jnp.inf, F32)
    acc_ref[...] = jnp.zeros(acc_ref.shape, F32)
    krow = lax.broadcasted_iota(jnp.int32, (tk, LANES), 0)
    qlane = lax.broadcasted_iota(jnp.int32, (tk, LANES), 1)

    def head_step(hh, c, k0, masked):
        hcols = slice(hh * HEAD_DIM, (hh + 1) * HEAD_DIM)
        s = lax.dot_general(k_ref[pl.ds(k0, tk), hcols], q_ref[:, hcols], NT_DIMS, preferred_element_type=F32)
        ckb = jnp.sum(jnp.where(head_lane == h0 + hh, ccol_ref[pl.ds(k0, tk), :], 0.0),
                      axis=1, keepdims=True) * log2e
        cq = cqs[hh]
        m_prev = m_ref[hh]
        ps, ms, alphas = [], [], []
        for j in range(tq // LANES):
            cols = slice(j * LANES, (j + 1) * LANES)
            y = s[:, cols] * (scale * log2e) - ckb
            if masked:
                y = jnp.where(k0 + krow <= q0 + j * LANES + qlane, y, -jnp.inf)
            m_new = jnp.maximum(m_prev[:, cols], jnp.max(y, axis=0, keepdims=True) + cq[:, cols])
            ps.append(jnp.exp2(y - (m_new - cq[:, cols])).astype(BF16))
            ms.append(m_new)
            alphas.append(jnp.exp2(m_prev[:, cols] - m_new))
        m_ref[hh] = jnp.concatenate(ms, axis=1)
        acc_ref[hh] = jnp.concatenate(alphas, axis=1) * acc_ref[hh] + jnp.dot(
            vt_ref[c, hh], jnp.concatenate(ps, axis=1), preferred_element_type=F32)

    def make_body(masked):
        def body(c, carry):
            k0 = pl.multiple_of(c * tk, tk)
            for hh in range(hp):
                head_step(hh, c, k0, masked)
            return carry
        return body

    n_full = q0 // tk
    n_all = (q0 + tq + tk - 1) // tk
    lax.fori_loop(0, n_full, make_body(False), 0)
    lax.fori_loop(n_full, n_all, make_body(True), 0)
    for hh in range(hp):
        hcols = slice(hh * HEAD_DIM, (hh + 1) * HEAD_DIM)
        g = g_ref[:, hcols]
        ot = acc_ref[hh, 0:HEAD_DIM, :] / acc_ref[hh, HEAD_DIM:HEAD_DIM + 1, :]
        o_ref[:, hcols] = (ot.T * (g * jax.nn.sigmoid(g))).astype(o_ref.dtype)


def _fox(qb, kb, vt, u32, ccol, crow, batch, seq):
    t = qb.shape[0]
    tq, hp = FOX_TQ, FOX_HEADS_PER_STEP
    wide = hp * HEAD_DIM
    nq = seq // tq
    nkc = seq // FOX_TK
    cg = BIG["c_g"] * 1024 // wide
    once = pl.Buffered(1)
    return pl.pallas_call(
        _fox_kernel,
        out_shape=jax.ShapeDtypeStruct((t, BRANCH_WIDTH), BF16),
        grid=(batch, N_HEADS // hp, nq),
        in_specs=[pl.BlockSpec((tq, wide), lambda b, h, i: (b * nq + i, h)),
                  pl.BlockSpec((seq, wide), lambda b, h, i: (b, h), pipeline_mode=once),
                  pl.BlockSpec((nkc, hp, VT_ROWS, FOX_TK), lambda b, h, i: (b, h, 0, 0), pipeline_mode=once),
                  pl.BlockSpec((seq, LANES), lambda b, h, i: (b, 0), pipeline_mode=once),
                  pl.BlockSpec((None, N_HEADS, tq), lambda b, h, i: (b * nq + i, 0, 0)),
                  pl.BlockSpec((tq, wide), lambda b, h, i: (b * nq + i, cg + h))],
        out_specs=pl.BlockSpec((tq, wide), lambda b, h, i: (b * nq + i, h)),
        scratch_shapes=[pltpu.VMEM((hp, 1, tq), F32), pltpu.VMEM((hp, VT_ROWS, tq), F32)],
        compiler_params=_cparams(("parallel", "parallel", "arbitrary")),
        name="fox_attention",
    )(qb, kb, vt, ccol, crow, u32)


def _hgrn_kernel(q_ref, f_ref, i_ref, g_ref, lb_ref, nw_ref, o_ref, st_ref):
    cs, sub = HGRN_CHUNK, HGRN_SUB
    ts = q_ref.shape[0]

    @pl.when(pl.program_id(2) == 0)
    def _():
        st_ref[...] = jnp.zeros_like(st_ref)

    tri = (lax.broadcasted_iota(jnp.int32, (cs, cs), 1) <= lax.broadcasted_iota(jnp.int32, (cs, cs), 0)).astype(F32)
    row_c = lax.broadcasted_iota(jnp.int32, (cs, HEAD_DIM), 0)
    row_h = lax.broadcasted_iota(jnp.int32, (sub // 2, HEAD_DIM), 0)
    lane_h = lax.broadcasted_iota(jnp.int32, (sub // 2, cs), 1)

    def chunk(ci, carry):
        for hh in range(HGRN_HEADS_PER_STEP):
            head_chunk(pl.multiple_of(ci * cs, cs), hh)
        return carry

    def head_chunk(r0, hh):
        cols = slice(hh * HEAD_DIM, (hh + 1) * HEAD_DIM)
        lb = lb_ref[:, cols]
        nw = nw_ref[:, cols]
        f = lb + (1.0 - lb) * jax.nn.sigmoid(f_ref[pl.ds(r0, cs), cols])
        kk = 1.0 - f
        bc = jnp.dot(tri, jnp.log(f), precision=lax.Precision.HIGHEST, preferred_element_type=F32)
        q = q_ref[pl.ds(r0, cs), cols]
        vb = i_ref[pl.ds(r0, cs), cols].astype(BF16)
        st = st_ref[hh]
        o = lax.dot_general((q * jnp.exp(bc)).astype(BF16), st.astype(BF16), NT_DIMS, preferred_element_type=F32)
        att_rows = []
        for si in range(cs // sub):
            lo = si * sub
            bi = bc[lo:lo + sub, :]
            qi = q[lo:lo + sub, :]
            att = jnp.zeros((sub, cs), F32)
            if si > 0:
                b0 = bc[lo - 1:lo, :]
                kt = kk * jnp.exp(jnp.where(row_c < lo, b0 - bc, -jnp.inf))
                qs = qi * jnp.exp(bi - b0)
                att = lax.dot_general(qs.astype(BF16), kt.astype(BF16), NT_DIMS, preferred_element_type=F32)
            half = sub // 2
            diag = [jnp.zeros((half, cs), F32), jnp.zeros((half, cs), F32)]
            for s in range(sub):
                bs = bc[lo + s:lo + s + 1, :]
                ks = kk[lo + s:lo + s + 1, :]
                for part in range(s // half, 2):
                    rows = slice(part * half, (part + 1) * half)
                    arg = bi[rows, :] - bs
                    if part == s // half:
                        arg = jnp.where(row_h >= s - part * half, arg, -jnp.inf)
                    p = qi[rows, :] * ks * jnp.exp(arg)
                    diag[part] = diag[part] + jnp.where(lane_h == lo + s, jnp.sum(p, axis=1, keepdims=True), 0.0)
            att_rows.append(att + jnp.concatenate(diag, axis=0))
        att = jnp.concatenate(att_rows, axis=0)
        o = o + jnp.dot(att.astype(BF16), vb, preferred_element_type=F32)
        bl = bc[cs - 1:cs, :]
        kd = kk * jnp.exp(bl - bc)
        st_ref[hh] = st * jnp.exp(bl) + lax.dot_general(vb, kd.astype(BF16), TN_DIMS, preferred_element_type=F32)
        y = o * lax.rsqrt(jnp.mean(o * o, axis=1, keepdims=True) + NORM_EPS) * nw
        g = g_ref[pl.ds(r0, cs), cols]
        o_ref[pl.ds(r0, cs), cols] = (y * (g * jax.nn.sigmoid(g))).astype(o_ref.dtype)

    lax.fori_loop(0, ts // cs, chunk, 0)


def _hgrn(u32, lb, norm_w, batch, seq, ts=512):
    t = u32.shape[0]
    ts = min(ts, seq)
    ns = seq // ts
    hp = HGRN_HEADS_PER_STEP
    wide = hp * HEAD_DIM
    dq, df, di, dg = (BIG[n] * 1024 // wide for n in ("d_q", "d_f", "d_i", "d_g"))

    def col(c0):
        return pl.BlockSpec((ts, wide), lambda b, h, j: (b * ns + j, c0 + h))

    vec = pl.BlockSpec((1, wide), lambda b, h, j: (0, h))
    return pl.pallas_call(
        _hgrn_kernel,
        out_shape=jax.ShapeDtypeStruct((t, BRANCH_WIDTH), BF16),
        grid=(batch, N_HEADS // hp, ns),
        in_specs=[col(dq), col(df), col(di), col(dg), vec, vec],
        out_specs=pl.BlockSpec((ts, wide), lambda b, h, j: (b * ns + j, h)),
        scratch_shapes=[pltpu.VMEM((hp, HEAD_DIM, HEAD_DIM), F32)],
        compiler_params=_cparams(("parallel", "parallel", "arbitrary")),
        name="hgrn2",
    )(u32, u32, u32, u32, lb.reshape(1, BRANCH_WIDTH), norm_w.reshape(1, BRANCH_WIDTH).astype(F32))


def _merge_kernel(ya_ref, yb_ref, yc_ref, yd_ref, h_ref, wb_ref, wm_ref, bm_ref, o_ref):
    tn = o_ref.shape[1]
    h = h_ref[...]
    merged = None
    for br, y_ref in enumerate((ya_ref, yb_ref, yc_ref, yd_ref)):
        proj = jnp.dot(y_ref[...], wb_ref[br], preferred_element_type=F32)
        gate = jnp.concatenate(
            [jnp.dot(h[:, n * MERGE_BLOCK_DIM:(n + 1) * MERGE_BLOCK_DIM], wm_ref[br, n], preferred_element_type=F32)
             for n in range(tn // MERGE_BLOCK_DIM)], axis=1)
        term = jax.nn.sigmoid(gate + bm_ref[br:br + 1, :]) * proj
        merged = term if merged is None else merged + term
    o_ref[...] = merged.astype(o_ref.dtype)


def _merge(ys, h, wb, wm, bm, layer, tm=1024, tn=512):
    t = h.shape[0]
    tm = min(tm, t)
    y_spec = pl.BlockSpec((tm, BRANCH_WIDTH), lambda i, j: (i, 0))
    nmb = tn // MERGE_BLOCK_DIM
    return pl.pallas_call(
        _merge_kernel,
        out_shape=jax.ShapeDtypeStruct((t, D_MODEL), BF16),
        grid=(t // tm, D_MODEL // tn),
        in_specs=[y_spec, y_spec, y_spec, y_spec,
                  pl.BlockSpec((tm, tn), lambda i, j: (i, j)),
                  pl.BlockSpec((None, 4, BRANCH_WIDTH, tn), lambda i, j: (layer, 0, 0, j)),
                  pl.BlockSpec((None, 4, nmb, MERGE_BLOCK_DIM, MERGE_BLOCK_DIM), lambda i, j: (layer, 0, j, 0, 0)),
                  pl.BlockSpec((None, 4, tn), lambda i, j: (layer, 0, j))],
        out_specs=pl.BlockSpec((tm, tn), lambda i, j: (i, j)),
        compiler_params=_cparams(("parallel", "parallel")),
        name="gated_merge",
    )(*ys, h, wb, wm, bm)


def kernel(x, norm_w, w_in, fox_f_bias, conv_w, hgrn_gamma, hgrn_norm_w, w_branch, w_merge, b_merge, w_out, final_norm_w):
    batch, seq, d = x.shape
    depth = w_in.shape[0]
    assert d == D_MODEL and w_in.shape[-1] == IN_WIDTH and seq % FOX_TC == 0
    xf = x.reshape(batch * seq, d)
    w_in_p = _pad_in_proj(w_in)
    wb = w_branch.astype(BF16)
    wm = w_merge.astype(BF16)
    wo = w_out.astype(BF16)
    lower = _lower_bounds(hgrn_gamma)
    tabs = _rope_tables(seq)
    for layer in range(depth):
        h = _rmsnorm(xf, norm_w[layer], BF16)
        u32 = _in_proj(h, w_in_p, layer)
        qa, ka, vt, iq, ik, iwt = _dsa_prep(u32, tabs, seq)
        y_a = _dsa(iq, iwt, ik, qa, ka, vt, u32, batch, seq)
        y_b = _conv(u32, conv_w[layer].astype(F32), seq)
        ccol, crow = _fox_cum(u32, fox_f_bias[layer], batch, seq)
        fq, fk, fvt = _fox_prep(u32)
        y_c = _fox(fq, fk, fvt, u32, ccol, crow, batch, seq)
        y_d = _hgrn(u32, lower[layer], hgrn_norm_w[layer], batch, seq)
        merged = _merge((y_a, y_b, y_c, y_d), h, wb, wm, b_merge.astype(F32), layer)
        xf = _out_proj(merged, wo, layer, xf)
    return _rmsnorm(xf, final_norm_w, F32).reshape(batch, seq, d)
```

```python
import functools

import numpy as np
import jax
import jax.numpy as jnp
from jax import lax
from jax.experimental import pallas as pl
from jax.experimental.pallas import tpu as pltpu

F32 = jnp.float32
BF16 = jnp.bfloat16

D_MODEL = 4096
HEAD_DIM = 128
ROPE_THETA = 10000.0
NORM_EPS = 1e-6
Q_BLOCK = 256
BRANCH_WIDTH = D_MODEL // 4
N_HEADS = BRANCH_WIDTH // HEAD_DIM
IDX_HEADS = 16
IDX_DIM = 64
DSA_TOPK_MAX = 256
CONV_K = 3
HGRN_CHUNK = 64
HGRN_SUB = 16
HGRN_HEADS_PER_STEP = 8
MERGE_BLOCKS = 16
MERGE_BLOCK_DIM = D_MODEL // MERGE_BLOCKS

LANES = 128
VT_ROWS = HEAD_DIM + 16
VMEM_LIMIT = 56 * 1024 * 1024

_SRC_LAYOUT = (
    ("a_q", 1024), ("a_k", 128), ("a_v", 128), ("a_iq", 1024), ("a_ik", 64), ("a_iw", 16), ("a_g", 1024),
    ("b_b", 1024), ("b_c", 1024), ("b_x", 1024), ("b_g", 1024),
    ("c_q", 1024), ("c_k", 1024), ("c_v", 1024), ("c_f", 8), ("c_g", 1024),
    ("d_q", 1024), ("d_f", 1024), ("d_i", 1024), ("d_g", 1024),
)
_SRC_OFF = {}
_o = 0
for _n, _w in _SRC_LAYOUT:
    _SRC_OFF[_n] = (_o, _w)
    _o += _w
IN_WIDTH = _o

_BIG = ("a_q", "a_iq", "a_g", "b_b", "b_c", "b_x", "b_g", "c_q", "c_k", "c_v", "c_g", "d_q", "d_f", "d_i", "d_g")
BIG = {n: i for i, n in enumerate(_BIG)}
SMALL_BASE = len(_BIG) * 1024 // LANES
SLOT_AK, SLOT_AV, SLOT_AIKW, SLOT_CF = SMALL_BASE, SMALL_BASE + 1, SMALL_BASE + 2, SMALL_BASE + 3
PAD_WIDTH = (SMALL_BASE + 4) * LANES

INT_MIN = np.int32(-2 ** 31)
NT_DIMS = (((1,), (1,)), ((), ()))
TN_DIMS = (((0,), (0,)), ((), ()))


def _cparams(sem):
    return pltpu.CompilerParams(dimension_semantics=sem, vmem_limit_bytes=VMEM_LIMIT)


def _pad_in_proj_kernel(w_ref, o_ref):
    cols = w_ref.shape[1]

    def put(dst, name, width=None):
        o, w = _SRC_OFF[name]
        w = w if width is None else width
        o_ref[dst:dst + w, :] = w_ref[o:o + w, :].astype(BF16)
        return dst + w

    dst = 0
    for name in _BIG:
        dst = put(dst, name)
    dst = put(dst, "a_k")
    dst = put(dst, "a_v")
    dst = put(dst, "a_ik", IDX_DIM + IDX_HEADS)
    o_ref[dst:dst + LANES - IDX_DIM - IDX_HEADS, :] = jnp.zeros((LANES - IDX_DIM - IDX_HEADS, cols), BF16)
    dst = put(dst + LANES - IDX_DIM - IDX_HEADS, "c_f")
    o_ref[dst:dst + LANES - N_HEADS, :] = jnp.zeros((LANES - N_HEADS, cols), BF16)


def _pad_in_proj(w_in, tl=128):
    w_t = jnp.swapaxes(w_in, 1, 2)
    depth, n, d = w_t.shape
    return pl.pallas_call(
        _pad_in_proj_kernel,
        out_shape=jax.ShapeDtypeStruct((depth, PAD_WIDTH, d), BF16),
        grid=(depth, d // tl),
        in_specs=[pl.BlockSpec((None, n, tl), lambda l, i: (l, 0, i))],
        out_specs=pl.BlockSpec((None, PAD_WIDTH, tl), lambda l, i: (l, 0, i)),
        compiler_params=_cparams(("parallel", "parallel")),
        name="pad_in_proj",
    )(w_t)


def _lb_kernel(g_ref, o_ref):
    g = g_ref[...]
    e = jnp.exp(g - jnp.max(g, axis=0, keepdims=True))
    sm = e / jnp.sum(e, axis=0, keepdims=True)
    acc = jnp.zeros_like(sm[0:1])
    rows = []
    for layer in range(g.shape[0]):
        acc = acc + sm[layer:layer + 1]
        rows.append(acc - sm[0:1])
    o_ref[...] = jnp.concatenate(rows, axis=0)


def _lower_bounds(gamma):
    return pl.pallas_call(
        _lb_kernel, out_shape=jax.ShapeDtypeStruct(gamma.shape, F32), name="hgrn_lower_bounds",
    )(gamma.astype(F32))


def _rmsnorm_kernel(x_ref, w_ref, o_ref):
    x = x_ref[...]
    ms = jnp.mean(x * x, axis=-1, keepdims=True)
    o_ref[...] = (x * lax.rsqrt(ms + NORM_EPS) * w_ref[...]).astype(o_ref.dtype)


def _rmsnorm(x, w, out_dtype, tm=256):
    t, d = x.shape
    return pl.pallas_call(
        _rmsnorm_kernel,
        out_shape=jax.ShapeDtypeStruct((t, d), out_dtype),
        grid=(t // tm,),
        in_specs=[pl.BlockSpec((tm, d), lambda i: (i, 0)), pl.BlockSpec((1, d), lambda i: (0, 0))],
        out_specs=pl.BlockSpec((tm, d), lambda i: (i, 0)),
        compiler_params=_cparams(("parallel",)),
        name="rmsnorm",
    )(x, w.reshape(1, d))


def _in_proj_kernel(a_ref, b_ref, o_ref):
    o_ref[...] = lax.dot_general(a_ref[...], b_ref[...], NT_DIMS, preferred_element_type=F32)


def _in_proj(h, w, layer, tm=1024, tn=512):
    t, k = h.shape
    n = w.shape[1]
    tm = min(tm, t)
    return pl.pallas_call(
        _in_proj_kernel,
        out_shape=jax.ShapeDtypeStruct((t, n), F32),
        grid=(t // tm, n // tn),
        in_specs=[pl.BlockSpec((tm, k), lambda i, j: (i, 0)),
                  pl.BlockSpec((None, tn, k), lambda i, j: (layer, j, 0))],
        out_specs=pl.BlockSpec((tm, tn), lambda i, j: (i, j)),
        compiler_params=_cparams(("parallel", "parallel")),
        name="in_proj",
    )(h, w)


def _out_proj_kernel(a_ref, b_ref, r_ref, o_ref):
    o_ref[...] = r_ref[...] + jnp.dot(a_ref[...], b_ref[...], preferred_element_type=F32)


def _out_proj(a, w, layer, resid, tm=1024, tn=512):
    t, k = a.shape
    n = w.shape[2]
    tm = min(tm, t)
    return pl.pallas_call(
        _out_proj_kernel,
        out_shape=jax.ShapeDtypeStruct((t, n), F32),
        grid=(t // tm, n // tn),
        in_specs=[pl.BlockSpec((tm, k), lambda i, j: (i, 0)),
                  pl.BlockSpec((None, k, tn), lambda i, j: (layer, 0, j)),
                  pl.BlockSpec((tm, tn), lambda i, j: (i, j))],
        out_specs=pl.BlockSpec((tm, tn), lambda i, j: (i, j)),
        compiler_params=_cparams(("parallel", "parallel")),
        name="out_proj",
    )(a, w, resid)


def _rope_tables(seq):
    pos = jnp.arange(seq).astype(F32)

    def tables(dim):
        half = dim // 2
        inv = ROPE_THETA ** (-jnp.arange(half, dtype=F32) / half)
        ang = pos[:, None] * inv[None, :]
        cos, sin = jnp.cos(ang), jnp.sin(ang)
        reps = LANES // dim
        cos_t = jnp.tile(jnp.concatenate([cos, cos], axis=1), (1, reps))
        sin_t = jnp.tile(jnp.concatenate([-sin, sin], axis=1), (1, reps))
        return cos_t, sin_t

    return tables(HEAD_DIM) + tables(IDX_DIM)


def _dsa_prep_kernel(q_ref, iq_ref, k_ref, v_ref, ikw_ref, cos_ref, sin_ref, cosi_ref, sini_ref,
                     qo_ref, ko_ref, vto_ref, iqo_ref, iko_ref, iwo_ref):
    cos, sin = cos_ref[...], sin_ref[...]
    cosi, sini = cosi_ref[...], sini_ref[...]
    tm = cos.shape[0]

    def rot_head(x):
        return x * cos + pltpu.roll(x, HEAD_DIM // 2, axis=1) * sin

    lane = lax.broadcasted_iota(jnp.int32, (tm, LANES), 1)
    first_half = (lane % IDX_DIM) < (IDX_DIM // 2)

    def rot_idx(x):
        partner = jnp.where(first_half, pltpu.roll(x, LANES - IDX_DIM // 2, axis=1),
                            pltpu.roll(x, IDX_DIM // 2, axis=1))
        return x * cosi + partner * sini

    for h in range(N_HEADS):
        qo_ref[:, h * HEAD_DIM:(h + 1) * HEAD_DIM] = rot_head(q_ref[:, h * HEAD_DIM:(h + 1) * HEAD_DIM]).astype(BF16)
    ko_ref[...] = rot_head(k_ref[...]).astype(BF16)
    vto_ref[0:HEAD_DIM, :] = v_ref[...].T.astype(BF16)
    vto_ref[HEAD_DIM:VT_ROWS, :] = jnp.ones((VT_ROWS - HEAD_DIM, tm), BF16)
    for p in range(IDX_HEADS // 2):
        r = rot_idx(iq_ref[:, p * LANES:(p + 1) * LANES]).astype(BF16)
        iqo_ref[2 * p] = r[:, :IDX_DIM]
        iqo_ref[2 * p + 1] = r[:, IDX_DIM:]
    ikw = ikw_ref[...]
    iko_ref[...] = rot_idx(ikw)[:, :IDX_DIM].astype(BF16)
    iwo_ref[...] = ikw.T[IDX_DIM:IDX_DIM + IDX_HEADS, :] * (IDX_HEADS ** -0.5 * IDX_DIM ** -0.5)


def _dsa_prep(u32, tabs, seq):
    t = u32.shape[0]
    tm = DSA_TK
    nseq = seq // tm
    cos_a, sin_a, cos_i, sin_i = tabs
    tab_spec = pl.BlockSpec((tm, LANES), lambda i: (i % nseq, 0))
    return pl.pallas_call(
        _dsa_prep_kernel,
        out_shape=(jax.ShapeDtypeStruct((t, BRANCH_WIDTH), BF16),
                   jax.ShapeDtypeStruct((t, HEAD_DIM), BF16),
                   jax.ShapeDtypeStruct((t // tm, VT_ROWS, tm), BF16),
                   jax.ShapeDtypeStruct((IDX_HEADS, t, IDX_DIM), BF16),
                   jax.ShapeDtypeStruct((t, IDX_DIM), BF16),
                   jax.ShapeDtypeStruct((IDX_HEADS, t), F32)),
        grid=(t // tm,),
        in_specs=[pl.BlockSpec((tm, 1024), lambda i: (i, BIG["a_q"])),
                  pl.BlockSpec((tm, 1024), lambda i: (i, BIG["a_iq"])),
                  pl.BlockSpec((tm, LANES), lambda i: (i, SLOT_AK)),
                  pl.BlockSpec((tm, LANES), lambda i: (i, SLOT_AV)),
                  pl.BlockSpec((tm, LANES), lambda i: (i, SLOT_AIKW)),
                  tab_spec, tab_spec, tab_spec, tab_spec],
        out_specs=(pl.BlockSpec((tm, BRANCH_WIDTH), lambda i: (i, 0)),
                   pl.BlockSpec((tm, HEAD_DIM), lambda i: (i, 0)),
                   pl.BlockSpec((None, VT_ROWS, tm), lambda i: (i, 0, 0)),
                   pl.BlockSpec((IDX_HEADS, tm, IDX_DIM), lambda i: (0, i, 0)),
                   pl.BlockSpec((tm, IDX_DIM), lambda i: (i, 0)),
                   pl.BlockSpec((IDX_HEADS, tm), lambda i: (0, i))),
        compiler_params=_cparams(("parallel",)),
        name="dsa_prep",
    )(u32, u32, u32, u32, u32, cos_a, sin_a, cos_i, sin_i)


DSA_TK = 256


def _bit_transpose32(words):
    a = list(words)
    j, m = 16, 0x0000FFFF
    while j:
        k = 0
        while k < 32:
            t = (a[k] ^ lax.shift_right_logical(a[k + j], jnp.int32(j))) & jnp.int32(m)
            a[k] = a[k] ^ t
            a[k + j] = a[k + j] ^ (t << j)
            k = (k + j + 1) & ~j
        j >>= 1
        m = (m ^ (m << j)) & 0xFFFFFFFF if j else m
    return a


def _dsa_kernel(iq_ref, iw_ref, ik_ref, q_ref, k_ref, vt_ref, g_ref, o_ref,
                keys_ref, planes_ref, m_ref, acc_ref, *, topk):
    qb = Q_BLOCK
    tk = DSA_TK
    i = pl.program_id(1)
    q0 = i * qb
    nch = (q0 + qb + tk - 1) // tk
    prow = planes_ref.shape[1]

    @pl.when((pl.program_id(0) == 0) & (i == 0))
    def _():
        planes_ref[...] = jnp.zeros(planes_ref.shape, jnp.int32)

    iw = iw_ref[...]
    iq2 = iq_ref[...].reshape(IDX_HEADS * qb, IDX_DIM)
    krow = lax.broadcasted_iota(jnp.int32, (tk, qb), 0)
    qcol = q0 + lax.broadcasted_iota(jnp.int32, (tk, qb), 1)

    def score_chunk(c):
        k0 = pl.multiple_of(c * tk, tk)
        logit = lax.dot_general(ik_ref[pl.ds(k0, tk), :], iq2, NT_DIMS, preferred_element_type=F32)
        s = jnp.zeros((tk, qb), F32)
        for h in range(IDX_HEADS):
            s = s + iw[h:h + 1, :] * jnp.maximum(logit[:, h * qb:(h + 1) * qb], 0.0)
        bits = lax.bitcast_convert_type(s, jnp.int32)
        key = bits ^ ((bits >> 31) & jnp.int32(0x7FFFFFFF))
        key = jnp.where(k0 + krow <= qcol, key, INT_MIN)
        keys_ref[c] = key
        ukey = key ^ INT_MIN
        planes = _bit_transpose32([ukey[8 * j:8 * j + 8, :] for j in range(32)])
        r0 = pl.multiple_of(c * 8, 8)
        for b in range(32):
            planes_ref[b, pl.ds(r0, 8), :] = planes[b]

    npair = (nch + 1) // 2

    def score_pair(cp, carry):
        score_chunk(2 * cp)
        score_chunk(2 * cp + 1)
        return carry

    lax.fori_loop(0, npair, score_pair, 0)

    zero = jnp.zeros((1, qb), jnp.int32)

    def popcount_rows(x):
        cnt = lax.population_count(x)
        return jnp.sum(jnp.sum(cnt.reshape(x.shape[0] // 8, 8, LANES), axis=0), axis=0, keepdims=True)

    def radix_select(lanes, rows):
        def select_bit(b, carry):
            live, n_above, prefix = carry
            plane = planes_ref[b, 0:rows, lanes]
            ones = live & plane
            c1 = popcount_rows(ones)
            take = n_above + c1 >= topk
            n_above = jnp.where(take, n_above, n_above + c1)
            live = jnp.where(take, ones, live & ~plane)
            prefix = jnp.where(take, prefix | jnp.left_shift(jnp.int32(1), 31 - b), prefix)
            return live, n_above, prefix

        live0 = jnp.where(lax.broadcasted_iota(jnp.int32, (rows, LANES), 0) < 16 * npair,
                          jnp.int32(-1), jnp.int32(0))
        z = jnp.zeros((1, LANES), jnp.int32)
        live, n_above, prefix = lax.fori_loop(0, 32, select_bit, (live0, z, z))
        return prefix ^ INT_MIN, n_above, popcount_rows(live)

    row_steps = sorted({r for r in (prow // 4, prow // 2, 3 * prow // 4, prow) if r % 8 == 0 and r > 0})
    which = sum((16 * npair > r).astype(jnp.int32) for r in row_steps[:-1])

    def branch(lanes, rows):
        return lambda _: radix_select(lanes, rows)

    groups = []
    for g in range(qb // LANES):
        lanes = slice(g * LANES, (g + 1) * LANES)
        groups.append(lax.switch(which, [branch(lanes, r) for r in row_steps], 0))
    tau, n_gt, n_eq = (jnp.concatenate(parts, axis=1) for parts in zip(*groups))
    need = topk - n_gt

    def count(pred):
        def body(c, acc):
            hit = jnp.where(pred(keys_ref[c], c * tk + krow), 1, 0)
            return acc + jnp.sum(hit.reshape(tk // 8, 8, qb), axis=0)

        acc = lax.fori_loop(0, nch, body, jnp.zeros((8, qb), jnp.int32))
        return jnp.sum(acc, axis=0, keepdims=True)

    def drop_surplus_ties(_):
        def body(it, p):
            cand = p + jnp.left_shift(jnp.int32(1), 14 - it)
            below = count(lambda k, idx: (k == tau) & (idx < cand))
            return jnp.where(below < need, cand, p)
        bound = lax.fori_loop(0, 15, body, zero) + 1

        def drop(c, carry):
            k = keys_ref[c]
            keys_ref[c] = jnp.where((k == tau) & (c * tk + krow >= bound), INT_MIN, k)
            return carry

        lax.fori_loop(0, nch, drop, 0)
        return 0

    has_tie = jnp.max(jnp.where((n_eq > need) & (tau > INT_MIN), 1, 0)) > 0
    lax.cond(has_tie, drop_surplus_ties, lambda _: 0, 0)
    threshold = jnp.maximum(tau, INT_MIN + 1)

    q = q_ref[...]
    q2 = jnp.concatenate([q[:, h * HEAD_DIM:(h + 1) * HEAD_DIM] for h in range(N_HEADS)], axis=0)
    m_ref[...] = jnp.full(m_ref.shape, -jnp.inf, F32)
    acc_ref[...] = jnp.zeros(acc_ref.shape, F32)
    exp2_scale = HEAD_DIM ** -0.5 * np.log2(np.e)

    def attend(cp, carry):
        k0 = pl.multiple_of(cp * 2 * tk, 2 * tk)
        bias = []
        for c in (2 * cp, 2 * cp + 1):
            bias.append(jnp.where(keys_ref[c] >= threshold, 0.0, -jnp.inf))
        bias = jnp.concatenate(bias, axis=0)
        s = lax.dot_general(k_ref[pl.ds(k0, 2 * tk), :], q2, NT_DIMS, preferred_element_type=F32)
        vt = jnp.concatenate([vt_ref[2 * cp], vt_ref[2 * cp + 1]], axis=1)
        m_prev = m_ref[...]
        ps, ms, alphas = [], [], []
        for h in range(N_HEADS):
            cols = slice(h * qb, (h + 1) * qb)
            z = s[:, cols] + bias
            m_new = jnp.maximum(m_prev[:, cols], jnp.max(z, axis=0, keepdims=True))
            m_safe = jnp.where(m_new == -jnp.inf, 0.0, m_new)
            ps.append(jnp.exp2((z - m_safe) * exp2_scale).astype(BF16))
            ms.append(m_new)
            alphas.append(jnp.exp2((m_prev[:, cols] - m_safe) * exp2_scale))
        m_ref[...] = jnp.concatenate(ms, axis=1)
        acc_ref[...] = jnp.concatenate(alphas, axis=1) * acc_ref[...] + jnp.dot(
            vt, jnp.concatenate(ps, axis=1), preferred_element_type=F32)
        return carry

    lax.fori_loop(0, npair, attend, 0)
    ot = acc_ref[0:HEAD_DIM, :] / acc_ref[HEAD_DIM:HEAD_DIM + 1, :]
    o = jnp.concatenate([ot[:, h * qb:(h + 1) * qb].T for h in range(N_HEADS)], axis=1)
    g = g_ref[...]
    o_ref[...] = (o * (g * jax.nn.sigmoid(g))).astype(o_ref.dtype)


def _dsa(iq, iwt, ik, qa, ka, vt, u32, batch, seq):
    t = qa.shape[0]
    nb = seq // Q_BLOCK
    topk = min(DSA_TOPK_MAX, seq // 4)
    kern = functools.partial(_dsa_kernel, topk=topk)
    return pl.pallas_call(
        kern,
        out_shape=jax.ShapeDtypeStruct((t, BRANCH_WIDTH), BF16),
        grid=(batch, nb),
        in_specs=[pl.BlockSpec((IDX_HEADS, Q_BLOCK, IDX_DIM), lambda b, i: (0, b * nb + i, 0)),
                  pl.BlockSpec((IDX_HEADS, Q_BLOCK), lambda b, i: (0, b * nb + i)),
                  pl.BlockSpec((seq, IDX_DIM), lambda b, i: (b, 0)),
                  pl.BlockSpec((Q_BLOCK, BRANCH_WIDTH), lambda b, i: (b * nb + i, 0)),
                  pl.BlockSpec((seq, HEAD_DIM), lambda b, i: (b, 0)),
                  pl.BlockSpec((seq // DSA_TK, VT_ROWS, DSA_TK), lambda b, i: (b, 0, 0)),
                  pl.BlockSpec((Q_BLOCK, 1024), lambda b, i: (b * nb + i, BIG["a_g"]))],
        out_specs=pl.BlockSpec((Q_BLOCK, BRANCH_WIDTH), lambda b, i: (b * nb + i, 0)),
        scratch_shapes=[pltpu.VMEM((seq // DSA_TK, DSA_TK, Q_BLOCK), jnp.int32),
                        pltpu.VMEM((32, seq // DSA_TK * 8, Q_BLOCK), jnp.int32),
                        pltpu.VMEM((1, N_HEADS * Q_BLOCK), F32),
                        pltpu.VMEM((VT_ROWS, N_HEADS * Q_BLOCK), F32)],
        compiler_params=_cparams(("arbitrary", "arbitrary")),
        name="dsa_attention",
    )(iq, iwt, ik, qa, ka, vt, u32)


def _conv_kernel(b_ref, c_ref, x_ref, g_ref, cp_ref, xp_ref, w_ref, o_ref, *, tiles_per_seq):
    i = pl.program_id(0)
    tm = b_ref.shape[0]
    u = c_ref[...] * x_ref[...]
    prev = cp_ref[...] * xp_ref[...]
    prev = jnp.where(i % tiles_per_seq == 0, 0.0, prev)
    row = lax.broadcasted_iota(jnp.int32, u.shape, 0)
    u1 = jnp.where(row == 0, prev[7:8, :], pltpu.roll(u, 1, axis=0))
    u2 = jnp.where(row == 0, prev[6:7, :], jnp.where(row == 1, prev[7:8, :], pltpu.roll(u, 2, axis=0)))
    w = w_ref[...]
    y = w[0:1, :] * u2 + w[1:2, :] * u1 + w[2:3, :] * u
    g = g_ref[...]
    o_ref[...] = (b_ref[...] * y * (g * jax.nn.sigmoid(g))).astype(o_ref.dtype)
    del tm


def _conv(u32, conv_w, seq, tm=512):
    t = u32.shape[0]
    tps = seq // tm
    kern = functools.partial(_conv_kernel, tiles_per_seq=tps)

    def big(name):
        return pl.BlockSpec((tm, 1024), lambda i: (i, BIG[name]))

    def halo(name):
        return pl.BlockSpec((8, 1024), lambda i: (jnp.maximum(i * (tm // 8) - 1, 0), BIG[name]))

    return pl.pallas_call(
        kern,
        out_shape=jax.ShapeDtypeStruct((t, BRANCH_WIDTH), BF16),
        grid=(t // tm,),
        in_specs=[big("b_b"), big("b_c"), big("b_x"), big("b_g"), halo("b_c"), halo("b_x"),
                  pl.BlockSpec((CONV_K, BRANCH_WIDTH), lambda i: (0, 0))],
        out_specs=pl.BlockSpec((tm, BRANCH_WIDTH), lambda i: (i, 0)),
        compiler_params=_cparams(("parallel",)),
        name="short_conv",
    )(u32, u32, u32, u32, u32, u32, conv_w)


FOX_TK = 512
FOX_TQ = 1024
FOX_TC = 1024
FOX_HEADS_PER_STEP = 4


def _fox_cum_kernel(cf_ref, bias_ref, ccol_ref, crow_ref, carry_ref):
    j = pl.program_id(1)

    @pl.when(j == 0)
    def _():
        carry_ref[...] = jnp.zeros_like(carry_ref)

    x = cf_ref[...] + bias_ref[...]
    log_f = jnp.minimum(x, 0.0) - jnp.log1p(jnp.exp(-jnp.abs(x)))
    tc = x.shape[0]
    tri = (lax.broadcasted_iota(jnp.int32, (tc, tc), 1) <= lax.broadcasted_iota(jnp.int32, (tc, tc), 0)).astype(F32)
    cs = jnp.dot(tri, log_f, precision=lax.Precision.HIGHEST, preferred_element_type=F32) + carry_ref[...]
    ccol_ref[...] = cs
    carry_ref[...] = cs[tc - 1:tc, :]
    cst = cs.T
    for part in range(tc // FOX_TQ):
        crow_ref[part] = cst[0:N_HEADS, part * FOX_TQ:(part + 1) * FOX_TQ]


def _fox_cum(u32, bias, batch, seq):
    t = u32.shape[0]
    tc = FOX_TC
    ns = seq // tc
    per = tc // FOX_TQ
    bias_p = jnp.zeros((1, LANES), F32).at[0, :N_HEADS].set(bias.astype(F32))
    return pl.pallas_call(
        _fox_cum_kernel,
        out_shape=(jax.ShapeDtypeStruct((t, LANES), F32),
                   jax.ShapeDtypeStruct((t // FOX_TQ, N_HEADS, FOX_TQ), F32)),
        grid=(batch, ns),
        in_specs=[pl.BlockSpec((tc, LANES), lambda b, j: (b * ns + j, SLOT_CF)),
                  pl.BlockSpec((1, LANES), lambda b, j: (0, 0))],
        out_specs=(pl.BlockSpec((tc, LANES), lambda b, j: (b * ns + j, 0)),
                   pl.BlockSpec((per, N_HEADS, FOX_TQ), lambda b, j: (b * ns + j, 0, 0))),
        scratch_shapes=[pltpu.VMEM((1, LANES), F32)],
        compiler_params=_cparams(("parallel", "arbitrary")),
        name="fox_cumsum",
    )(u32, bias_p)


def _fox_prep_kernel(q_ref, k_ref, v_ref, qo_ref, ko_ref, vto_ref):
    tm = q_ref.shape[0]
    qo_ref[...] = q_ref[...].astype(BF16)
    ko_ref[...] = k_ref[...].astype(BF16)
    for h in range(N_HEADS):
        vto_ref[h, 0:HEAD_DIM, :] = v_ref[:, h * HEAD_DIM:(h + 1) * HEAD_DIM].T.astype(BF16)
        vto_ref[h, HEAD_DIM:VT_ROWS, :] = jnp.ones((VT_ROWS - HEAD_DIM, tm), BF16)


def _fox_prep(u32):
    t = u32.shape[0]
    tm = FOX_TK

    def big(name):
        return pl.BlockSpec((tm, 1024), lambda i: (i, BIG[name]))

    row_spec = pl.BlockSpec((tm, BRANCH_WIDTH), lambda i: (i, 0))
    return pl.pallas_call(
        _fox_prep_kernel,
        out_shape=(jax.ShapeDtypeStruct((t, BRANCH_WIDTH), BF16),
                   jax.ShapeDtypeStruct((t, BRANCH_WIDTH), BF16),
                   jax.ShapeDtypeStruct((t // tm, N_HEADS, VT_ROWS, tm), BF16)),
        grid=(t // tm,),
        in_specs=[big("c_q"), big("c_k"), big("c_v")],
        out_specs=(row_spec, row_spec, pl.BlockSpec((None, N_HEADS, VT_ROWS, tm), lambda i: (i, 0, 0, 0))),
        compiler_params=_cparams(("parallel",)),
        name="fox_prep",
    )(u32, u32, u32)


def _fox_kernel(q_ref, k_ref, vt_ref, ccol_ref, crow_ref, g_ref, o_ref, m_ref, acc_ref):
    tq, tk, hp = FOX_TQ, FOX_TK, FOX_HEADS_PER_STEP
    h0 = pl.program_id(1) * hp
    q0 = pl.program_id(2) * tq
    scale = HEAD_DIM ** -0.5
    log2e = np.log2(np.e)
    head_lane = lax.broadcasted_iota(jnp.int32, (tk, LANES), 1)

    sub = lax.broadcasted_iota(jnp.int32, (N_HEADS, tq), 0)
    crow = crow_ref[...]
    cqs = [jnp.sum(jnp.where(sub == h0 + hh, crow, 0.0), axis=0, keepdims=True) * log2e for hh in range(hp)]
    m_ref[...] = jnp.full(m_ref.shape, -jnp.inf, F32)
    acc_ref[...] = jnp.zeros(acc_ref.shape, F32)
    krow = lax.broadcasted_iota(jnp.int32, (tk, LANES), 0)
    qlane = lax.broadcasted_iota(jnp.int32, (tk, LANES), 1)

    def head_step(hh, c, k0, masked):
        hcols = slice(hh * HEAD_DIM, (hh + 1) * HEAD_DIM)
        s = lax.dot_general(k_ref[pl.ds(k0, tk), hcols], q_ref[:, hcols], NT_DIMS, preferred_element_type=F32)
        ckb = jnp.sum(jnp.where(head_lane == h0 + hh, ccol_ref[pl.ds(k0, tk), :], 0.0),
                      axis=1, keepdims=True) * log2e
        cq = cqs[hh]
        m_prev = m_ref[hh]
        ps, ms, alphas = [], [], []
        for j in range(tq // LANES):
            cols = slice(j * LANES, (j + 1) * LANES)
            y = s[:, cols] * (scale * log2e) - ckb
            if masked:
                y = jnp.where(k0 + krow <= q0 + j * LANES + qlane, y, -jnp.inf)
            m_new = jnp.maximum(m_prev[:, cols], jnp.max(y, axis=0, keepdims=True) + cq[:, cols])
            ps.append(jnp.exp2(y - (m_new - cq[:, cols])).astype(BF16))
            ms.append(m_new)
            alphas.append(jnp.exp2(m_prev[:, cols] - m_new))
        m_ref[hh] = jnp.concatenate(ms, axis=1)
        acc_ref[hh] = jnp.concatenate(alphas, axis=1) * acc_ref[hh] + jnp.dot(
            vt_ref[c, hh], jnp.concatenate(ps, axis=1), preferred_element_type=F32)

    def make_body(masked):
        def body(c, carry):
            k0 = pl.multiple_of(c * tk, tk)
            for hh in range(hp):
                head_step(hh, c, k0, masked)
            return carry
        return body

    n_full = q0 // tk
    n_all = (q0 + tq + tk - 1) // tk
    lax.fori_loop(0, n_full, make_body(False), 0)
    lax.fori_loop(n_full, n_all, make_body(True), 0)
    for hh in range(hp):
        hcols = slice(hh * HEAD_DIM, (hh + 1) * HEAD_DIM)
        g = g_ref[:, hcols]
        ot = acc_ref[hh, 0:HEAD_DIM, :] / acc_ref[hh, HEAD_DIM:HEAD_DIM + 1, :]
        o_ref[:, hcols] = (ot.T * (g * jax.nn.sigmoid(g))).astype(o_ref.dtype)


def _fox(qb, kb, vt, u32, ccol, crow, batch, seq):
    t = qb.shape[0]
    tq, hp = FOX_TQ, FOX_HEADS_PER_STEP
    wide = hp * HEAD_DIM
    nq = seq // tq
    nkc = seq // FOX_TK
    cg = BIG["c_g"] * 1024 // wide
    once = pl.Buffered(1)
    return pl.pallas_call(
        _fox_kernel,
        out_shape=jax.ShapeDtypeStruct((t, BRANCH_WIDTH), BF16),
        grid=(batch, N_HEADS // hp, nq),
        in_specs=[pl.BlockSpec((tq, wide), lambda b, h, i: (b * nq + i, h)),
                  pl.BlockSpec((seq, wide), lambda b, h, i: (b, h), pipeline_mode=once),
                  pl.BlockSpec((nkc, hp, VT_ROWS, FOX_TK), lambda b, h, i: (b, h, 0, 0), pipeline_mode=once),
                  pl.BlockSpec((seq, LANES), lambda b, h, i: (b, 0), pipeline_mode=once),
                  pl.BlockSpec((None, N_HEADS, tq), lambda b, h, i: (b * nq + i, 0, 0)),
                  pl.BlockSpec((tq, wide), lambda b, h, i: (b * nq + i, cg + h))],
        out_specs=pl.BlockSpec((tq, wide), lambda b, h, i: (b * nq + i, h)),
        scratch_shapes=[pltpu.VMEM((hp, 1, tq), F32), pltpu.VMEM((hp, VT_ROWS, tq), F32)],
        compiler_params=_cparams(("parallel", "parallel", "arbitrary")),
        name="fox_attention",
    )(qb, kb, vt, ccol, crow, u32)


def _hgrn_kernel(q_ref, f_ref, i_ref, g_ref, lb_ref, nw_ref, o_ref, st_ref):
    cs, sub = HGRN_CHUNK, HGRN_SUB
    ts = q_ref.shape[0]

    @pl.when(pl.program_id(2) == 0)
    def _():
        st_ref[...] = jnp.zeros_like(st_ref)

    tri = (lax.broadcasted_iota(jnp.int32, (cs, cs), 1) <= lax.broadcasted_iota(jnp.int32, (cs, cs), 0)).astype(F32)
    row_c = lax.broadcasted_iota(jnp.int32, (cs, HEAD_DIM), 0)
    row_h = lax.broadcasted_iota(jnp.int32, (sub // 2, HEAD_DIM), 0)
    lane_h = lax.broadcasted_iota(jnp.int32, (sub // 2, cs), 1)

    def chunk(ci, carry):
        for hh in range(HGRN_HEADS_PER_STEP):
            head_chunk(pl.multiple_of(ci * cs, cs), hh)
        return carry

    def head_chunk(r0, hh):
        cols = slice(hh * HEAD_DIM, (hh + 1) * HEAD_DIM)
        lb = lb_ref[:, cols]
        nw = nw_ref[:, cols]
        f = lb + (1.0 - lb) * jax.nn.sigmoid(f_ref[pl.ds(r0, cs), cols])
        kk = 1.0 - f
        bc = jnp.dot(tri, jnp.log(f), precision=lax.Precision.HIGHEST, preferred_element_type=F32)
        bk = bc - jnp.log(kk)
        q = q_ref[pl.ds(r0, cs), cols]
        vb = i_ref[pl.ds(r0, cs), cols].astype(BF16)
        st = st_ref[hh]
        o = lax.dot_general((q * jnp.exp(bc)).astype(BF16), st.astype(BF16), NT_DIMS, preferred_element_type=F32)
        att_rows = []
        for si in range(cs // sub):
            lo = si * sub
            bi = bc[lo:lo + sub, :]
            qi = q[lo:lo + sub, :]
            att = jnp.zeros((sub, cs), F32)
            if si > 0:
                b0 = bc[lo - 1:lo, :]
                kt = kk * jnp.exp(jnp.where(row_c < lo, b0 - bc, -jnp.inf))
                qs = qi * jnp.exp(bi - b0)
                att = lax.dot_general(qs.astype(BF16), kt.astype(BF16), NT_DIMS, preferred_element_type=F32)
            half = sub // 2
            diag = [jnp.zeros((half, cs), F32), jnp.zeros((half, cs), F32)]
            for s in range(sub):
                bs = bk[lo + s:lo + s + 1, :]
                for part in range(s // half, 2):
                    rows = slice(part * half, (part + 1) * half)
                    arg = bi[rows, :] - bs
                    if part == s // half:
                        arg = jnp.where(row_h >= s - part * half, arg, -jnp.inf)
                    p = qi[rows, :] * jnp.exp(arg)
                    diag[part] = diag[part] + jnp.where(lane_h == lo + s, jnp.sum(p, axis=1, keepdims=True), 0.0)
            att_rows.append(att + jnp.concatenate(diag, axis=0))
        att = jnp.concatenate(att_rows, axis=0)
        o = o + jnp.dot(att.astype(BF16), vb, preferred_element_type=F32)
        bl = bc[cs - 1:cs, :]
        kd = kk * jnp.exp(bl - bc)
        st_ref[hh] = st * jnp.exp(bl) + lax.dot_general(vb, kd.astype(BF16), TN_DIMS, preferred_element_type=F32)
        y = o * lax.rsqrt(jnp.mean(o * o, axis=1, keepdims=True) + NORM_EPS) * nw
        g = g_ref[pl.ds(r0, cs), cols]
        o_ref[pl.ds(r0, cs), cols] = (y * (g * jax.nn.sigmoid(g))).astype(o_ref.dtype)

    lax.fori_loop(0, ts // cs, chunk, 0)


def _hgrn(u32, lb, norm_w, batch, seq, ts=512):
    t = u32.shape[0]
    ts = min(ts, seq)
    ns = seq // ts
    hp = HGRN_HEADS_PER_STEP
    wide = hp * HEAD_DIM
    dq, df, di, dg = (BIG[n] * 1024 // wide for n in ("d_q", "d_f", "d_i", "d_g"))

    def col(c0):
        return pl.BlockSpec((ts, wide), lambda b, h, j: (b * ns + j, c0 + h))

    vec = pl.BlockSpec((1, wide), lambda b, h, j: (0, h))
    return pl.pallas_call(
        _hgrn_kernel,
        out_shape=jax.ShapeDtypeStruct((t, BRANCH_WIDTH), BF16),
        grid=(batch, N_HEADS // hp, ns),
        in_specs=[col(dq), col(df), col(di), col(dg), vec, vec],
        out_specs=pl.BlockSpec((ts, wide), lambda b, h, j: (b * ns + j, h)),
        scratch_shapes=[pltpu.VMEM((hp, HEAD_DIM, HEAD_DIM), F32)],
        compiler_params=_cparams(("parallel", "parallel", "arbitrary")),
        name="hgrn2",
    )(u32, u32, u32, u32, lb.reshape(1, BRANCH_WIDTH), norm_w.reshape(1, BRANCH_WIDTH).astype(F32))


def _merge_kernel(ya_ref, yb_ref, yc_ref, yd_ref, h_ref, wb_ref, wm_ref, bm_ref, o_ref):
    tn = o_ref.shape[1]
    h = h_ref[...]
    merged = None
    for br, y_ref in enumerate((ya_ref, yb_ref, yc_ref, yd_ref)):
        proj = jnp.dot(y_ref[...], wb_ref[br], preferred_element_type=F32)
        gate = jnp.concatenate(
            [jnp.dot(h[:, n * MERGE_BLOCK_DIM:(n + 1) * MERGE_BLOCK_DIM], wm_ref[br, n], preferred_element_type=F32)
             for n in range(tn // MERGE_BLOCK_DIM)], axis=1)
        term = jax.nn.sigmoid(gate + bm_ref[br:br + 1, :]) * proj
        merged = term if merged is None else merged + term
    o_ref[...] = merged.astype(o_ref.dtype)


def _merge(ys, h, wb, wm, bm, layer, tm=1024, tn=512):
    t = h.shape[0]
    tm = min(tm, t)
    y_spec = pl.BlockSpec((tm, BRANCH_WIDTH), lambda i, j: (i, 0))
    nmb = tn // MERGE_BLOCK_DIM
    return pl.pallas_call(
        _merge_kernel,
        out_shape=jax.ShapeDtypeStruct((t, D_MODEL), BF16),
        grid=(t // tm, D_MODEL // tn),
        in_specs=[y_spec, y_spec, y_spec, y_spec,
                  pl.BlockSpec((tm, tn), lambda i, j: (i, j)),
                  pl.BlockSpec((None, 4, BRANCH_WIDTH, tn), lambda i, j: (layer, 0, 0, j)),
                  pl.BlockSpec((None, 4, nmb, MERGE_BLOCK_DIM, MERGE_BLOCK_DIM), lambda i, j: (layer, 0, j, 0, 0)),
                  pl.BlockSpec((None, 4, tn), lambda i, j: (layer, 0, j))],
        out_specs=pl.BlockSpec((tm, tn), lambda i, j: (i, j)),
        compiler_params=_cparams(("parallel", "parallel")),
        name="gated_merge",
    )(*ys, h, wb, wm, bm)


def kernel(x, norm_w, w_in, fox_f_bias, conv_w, hgrn_gamma, hgrn_norm_w, w_branch, w_merge, b_merge, w_out, final_norm_w):
    batch, seq, d = x.shape
    depth = w_in.shape[0]
    assert d == D_MODEL and w_in.shape[-1] == IN_WIDTH and seq % FOX_TC == 0
    xf = x.reshape(batch * seq, d)
    w_in_p = _pad_in_proj(w_in)
    wb = w_branch.astype(BF16)
    wm = w_merge.astype(BF16)
    wo = w_out.astype(BF16)
    lower = _lower_bounds(hgrn_gamma)
    tabs = _rope_tables(seq)
    for layer in range(depth):
        h = _rmsnorm(xf, norm_w[layer], BF16)
        u32 = _in_proj(h, w_in_p, layer)
        qa, ka, vt, iq, ik, iwt = _dsa_prep(u32, tabs, seq)
        y_a = _dsa(iq, iwt, ik, qa, ka, vt, u32, batch, seq)
        y_b = _conv(u32, conv_w[layer].astype(F32), seq)
        ccol, crow = _fox_cum(u32, fox_f_bias[layer], batch, seq)
        fq, fk, fvt = _fox_prep(u32)
        y_c = _fox(fq, fk, fvt, u32, ccol, crow, batch, seq)
        y_d = _hgrn(u32, lower[layer], hgrn_norm_w[layer], batch, seq)
        merged = _merge((y_a, y_b, y_c, y_d), h, wb, wm, b_merge.astype(F32), layer)
        xf = _out_proj(merged, wo, layer, xf)
    return _rmsnorm(xf, final_norm_w, F32).reshape(batch, seq, d)
```

```python
import functools

import numpy as np
import jax
import jax.numpy as jnp
from jax import lax
from jax.experimental import pallas as pl
from jax.experimental.pallas import tpu as pltpu

F32 = jnp.float32
BF16 = jnp.bfloat16

D_MODEL = 4096
HEAD_DIM = 128
ROPE_THETA = 10000.0
NORM_EPS = 1e-6
Q_BLOCK = 256
BRANCH_WIDTH = D_MODEL // 4
N_HEADS = BRANCH_WIDTH // HEAD_DIM
IDX_HEADS = 16
IDX_DIM = 64
DSA_TOPK_MAX = 256
CONV_K = 3
HGRN_CHUNK = 64
HGRN_SUB = 16
HGRN_HEADS_PER_STEP = 8
MERGE_BLOCKS = 16
MERGE_BLOCK_DIM = D_MODEL // MERGE_BLOCKS

LANES = 128
SUBLANES = 8
VT_ROWS = HEAD_DIM + 16
VMEM_LIMIT = 56 * 1024 * 1024

_SRC_LAYOUT = (
    ("a_q", 1024), ("a_k", 128), ("a_v", 128), ("a_iq", 1024), ("a_ik", 64), ("a_iw", 16), ("a_g", 1024),
    ("b_b", 1024), ("b_c", 1024), ("b_x", 1024), ("b_g", 1024),
    ("c_q", 1024), ("c_k", 1024), ("c_v", 1024), ("c_f", 8), ("c_g", 1024),
    ("d_q", 1024), ("d_f", 1024), ("d_i", 1024), ("d_g", 1024),
)
_SRC_OFF = {}
_o = 0
for _n, _w in _SRC_LAYOUT:
    _SRC_OFF[_n] = (_o, _w)
    _o += _w
IN_WIDTH = _o

_BIG = ("a_q", "a_iq", "a_g", "b_b", "b_c", "b_x", "b_g", "c_q", "c_k", "c_v", "c_g", "d_q", "d_f", "d_i", "d_g")
BIG = {n: i for i, n in enumerate(_BIG)}
SMALL_BASE = len(_BIG) * 1024 // LANES
SLOT_AK, SLOT_AV, SLOT_AIKW, SLOT_CF = SMALL_BASE, SMALL_BASE + 1, SMALL_BASE + 2, SMALL_BASE + 3
PAD_WIDTH = (SMALL_BASE + 4) * LANES

INT_MIN = np.int32(-2 ** 31)
NT_DIMS = (((1,), (1,)), ((), ()))
TN_DIMS = (((0,), (0,)), ((), ()))


def _cparams(sem):
    return pltpu.CompilerParams(dimension_semantics=sem, vmem_limit_bytes=VMEM_LIMIT)


def _pad_in_proj_kernel(w_ref, o_ref):
    cols = w_ref.shape[1]

    def put(dst, name, width=None):
        o, w = _SRC_OFF[name]
        w = w if width is None else width
        o_ref[dst:dst + w, :] = w_ref[o:o + w, :].astype(BF16)
        return dst + w

    dst = 0
    for name in _BIG:
        dst = put(dst, name)
    dst = put(dst, "a_k")
    dst = put(dst, "a_v")
    dst = put(dst, "a_ik", IDX_DIM + IDX_HEADS)
    o_ref[dst:dst + LANES - IDX_DIM - IDX_HEADS, :] = jnp.zeros((LANES - IDX_DIM - IDX_HEADS, cols), BF16)
    dst = put(dst + LANES - IDX_DIM - IDX_HEADS, "c_f")
    o_ref[dst:dst + LANES - N_HEADS, :] = jnp.zeros((LANES - N_HEADS, cols), BF16)


def _pad_in_proj(w_in, tl=128):
    w_t = jnp.swapaxes(w_in, 1, 2)
    depth, n, d = w_t.shape
    return pl.pallas_call(
        _pad_in_proj_kernel,
        out_shape=jax.ShapeDtypeStruct((depth, PAD_WIDTH, d), BF16),
        grid=(depth, d // tl),
        in_specs=[pl.BlockSpec((None, n, tl), lambda l, i: (l, 0, i))],
        out_specs=pl.BlockSpec((None, PAD_WIDTH, tl), lambda l, i: (l, 0, i)),
        compiler_params=_cparams(("parallel", "parallel")),
        name="pad_in_proj",
    )(w_t)


def _lb_kernel(g_ref, o_ref):
    g = g_ref[...]
    e = jnp.exp(g - jnp.max(g, axis=0, keepdims=True))
    sm = e / jnp.sum(e, axis=0, keepdims=True)
    acc = jnp.zeros_like(sm[0:1])
    rows = []
    for layer in range(g.shape[0]):
        acc = acc + sm[layer:layer + 1]
        rows.append(acc - sm[0:1])
    o_ref[...] = jnp.concatenate(rows, axis=0)


def _lower_bounds(gamma):
    return pl.pallas_call(
        _lb_kernel, out_shape=jax.ShapeDtypeStruct(gamma.shape, F32), name="hgrn_lower_bounds",
    )(gamma.astype(F32))


def _rmsnorm_kernel(x_ref, w_ref, o_ref):
    x = x_ref[...]
    ms = jnp.mean(x * x, axis=-1, keepdims=True)
    o_ref[...] = (x * lax.rsqrt(ms + NORM_EPS) * w_ref[...]).astype(o_ref.dtype)


def _rmsnorm(x, w, out_dtype, tm=256):
    t, d = x.shape
    return pl.pallas_call(
        _rmsnorm_kernel,
        out_shape=jax.ShapeDtypeStruct((t, d), out_dtype),
        grid=(t // tm,),
        in_specs=[pl.BlockSpec((tm, d), lambda i: (i, 0)), pl.BlockSpec((1, d), lambda i: (0, 0))],
        out_specs=pl.BlockSpec((tm, d), lambda i: (i, 0)),
        compiler_params=_cparams(("parallel",)),
        name="rmsnorm",
    )(x, w.reshape(1, d))


def _in_proj_kernel(a_ref, b_ref, o_ref):
    o_ref[...] = lax.dot_general(a_ref[...], b_ref[...], NT_DIMS, preferred_element_type=F32)


def _in_proj(h, w, layer, tm=1024, tn=512):
    t, k = h.shape
    n = w.shape[1]
    tm = min(tm, t)
    return pl.pallas_call(
        _in_proj_kernel,
        out_shape=jax.ShapeDtypeStruct((t, n), F32),
        grid=(t // tm, n // tn),
        in_specs=[pl.BlockSpec((tm, k), lambda i, j: (i, 0)),
                  pl.BlockSpec((None, tn, k), lambda i, j: (layer, j, 0))],
        out_specs=pl.BlockSpec((tm, tn), lambda i, j: (i, j)),
        compiler_params=_cparams(("parallel", "parallel")),
        name="in_proj",
    )(h, w)


def _out_proj_kernel(a_ref, b_ref, r_ref, o_ref):
    o_ref[...] = r_ref[...] + jnp.dot(a_ref[...], b_ref[...], preferred_element_type=F32)


def _out_proj(a, w, layer, resid, tm=1024, tn=512):
    t, k = a.shape
    n = w.shape[2]
    tm = min(tm, t)
    return pl.pallas_call(
        _out_proj_kernel,
        out_shape=jax.ShapeDtypeStruct((t, n), F32),
        grid=(t // tm, n // tn),
        in_specs=[pl.BlockSpec((tm, k), lambda i, j: (i, 0)),
                  pl.BlockSpec((None, k, tn), lambda i, j: (layer, 0, j)),
                  pl.BlockSpec((tm, tn), lambda i, j: (i, j))],
        out_specs=pl.BlockSpec((tm, tn), lambda i, j: (i, j)),
        compiler_params=_cparams(("parallel", "parallel")),
        name="out_proj",
    )(a, w, resid)


def _rope_tables(seq):
    pos = jnp.arange(seq).astype(F32)

    def tables(dim):
        half = dim // 2
        inv = ROPE_THETA ** (-jnp.arange(half, dtype=F32) / half)
        ang = pos[:, None] * inv[None, :]
        cos, sin = jnp.cos(ang), jnp.sin(ang)
        reps = LANES // dim
        cos_t = jnp.tile(jnp.concatenate([cos, cos], axis=1), (1, reps))
        sin_t = jnp.tile(jnp.concatenate([-sin, sin], axis=1), (1, reps))
        return cos_t, sin_t

    return tables(HEAD_DIM) + tables(IDX_DIM)


def _dsa_prep_kernel(q_ref, iq_ref, k_ref, v_ref, ikw_ref, cos_ref, sin_ref, cosi_ref, sini_ref,
                     qo_ref, ko_ref, vto_ref, iqo_ref, iko_ref, iwo_ref):
    cos, sin = cos_ref[...], sin_ref[...]
    cosi, sini = cosi_ref[...], sini_ref[...]
    tm = cos.shape[0]

    def rot_head(x):
        return x * cos + pltpu.roll(x, HEAD_DIM // 2, axis=1) * sin

    lane = lax.broadcasted_iota(jnp.int32, (tm, LANES), 1)
    first_half = (lane % IDX_DIM) < (IDX_DIM // 2)

    def rot_idx(x):
        partner = jnp.where(first_half, pltpu.roll(x, LANES - IDX_DIM // 2, axis=1),
                            pltpu.roll(x, IDX_DIM // 2, axis=1))
        return x * cosi + partner * sini

    for h in range(N_HEADS):
        qo_ref[:, h * HEAD_DIM:(h + 1) * HEAD_DIM] = rot_head(q_ref[:, h * HEAD_DIM:(h + 1) * HEAD_DIM]).astype(BF16)
    ko_ref[...] = rot_head(k_ref[...]).astype(BF16)
    vto_ref[0:HEAD_DIM, :] = v_ref[...].T.astype(BF16)
    vto_ref[HEAD_DIM:VT_ROWS, :] = jnp.ones((VT_ROWS - HEAD_DIM, tm), BF16)
    for p in range(IDX_HEADS // 2):
        r = rot_idx(iq_ref[:, p * LANES:(p + 1) * LANES]).astype(BF16)
        iqo_ref[2 * p] = r[:, :IDX_DIM]
        iqo_ref[2 * p + 1] = r[:, IDX_DIM:]
    ikw = ikw_ref[...]
    iko_ref[...] = rot_idx(ikw)[:, :IDX_DIM].astype(BF16)
    iwo_ref[...] = ikw.T[IDX_DIM:IDX_DIM + IDX_HEADS, :] * (IDX_HEADS ** -0.5 * IDX_DIM ** -0.5)


def _dsa_prep(u32, tabs, seq):
    t = u32.shape[0]
    tm = DSA_TK
    nseq = seq // tm
    cos_a, sin_a, cos_i, sin_i = tabs
    tab_spec = pl.BlockSpec((tm, LANES), lambda i: (i % nseq, 0))
    return pl.pallas_call(
        _dsa_prep_kernel,
        out_shape=(jax.ShapeDtypeStruct((t, BRANCH_WIDTH), BF16),
                   jax.ShapeDtypeStruct((t, HEAD_DIM), BF16),
                   jax.ShapeDtypeStruct((t // tm, VT_ROWS, tm), BF16),
                   jax.ShapeDtypeStruct((IDX_HEADS, t, IDX_DIM), BF16),
                   jax.ShapeDtypeStruct((t, IDX_DIM), BF16),
                   jax.ShapeDtypeStruct((IDX_HEADS, t), F32)),
        grid=(t // tm,),
        in_specs=[pl.BlockSpec((tm, 1024), lambda i: (i, BIG["a_q"])),
                  pl.BlockSpec((tm, 1024), lambda i: (i, BIG["a_iq"])),
                  pl.BlockSpec((tm, LANES), lambda i: (i, SLOT_AK)),
                  pl.BlockSpec((tm, LANES), lambda i: (i, SLOT_AV)),
                  pl.BlockSpec((tm, LANES), lambda i: (i, SLOT_AIKW)),
                  tab_spec, tab_spec, tab_spec, tab_spec],
        out_specs=(pl.BlockSpec((tm, BRANCH_WIDTH), lambda i: (i, 0)),
                   pl.BlockSpec((tm, HEAD_DIM), lambda i: (i, 0)),
                   pl.BlockSpec((None, VT_ROWS, tm), lambda i: (i, 0, 0)),
                   pl.BlockSpec((IDX_HEADS, tm, IDX_DIM), lambda i: (0, i, 0)),
                   pl.BlockSpec((tm, IDX_DIM), lambda i: (i, 0)),
                   pl.BlockSpec((IDX_HEADS, tm), lambda i: (0, i))),
        compiler_params=_cparams(("parallel",)),
        name="dsa_prep",
    )(u32, u32, u32, u32, u32, cos_a, sin_a, cos_i, sin_i)


WORD_BITS = 32
DSA_TK = 256
CHUNK_ROWS = DSA_TK // WORD_BITS


def _bit_transpose32(words):
    a = list(words)
    j, m = WORD_BITS // 2, 0x0000FFFF
    while j:
        k = 0
        while k < WORD_BITS:
            t = (a[k] ^ lax.shift_right_logical(a[k + j], jnp.int32(j))) & jnp.int32(m)
            a[k] = a[k] ^ t
            a[k + j] = a[k + j] ^ (t << j)
            k = (k + j + 1) & ~j
        j >>= 1
        m = (m ^ (m << j)) & 0xFFFFFFFF if j else m
    return a


def _dsa_kernel(iq_ref, iw_ref, ik_ref, q_ref, k_ref, vt_ref, g_ref, o_ref,
                keys_ref, planes_ref, m_ref, acc_ref, *, topk):
    qb = Q_BLOCK
    tk = DSA_TK
    i = pl.program_id(1)
    q0 = i * qb
    nch = (q0 + qb + tk - 1) // tk
    prow = planes_ref.shape[1]
    index_bits = (keys_ref.shape[0] * tk).bit_length()

    @pl.when((pl.program_id(0) == 0) & (i == 0))
    def _():
        planes_ref[...] = jnp.zeros(planes_ref.shape, jnp.int32)

    iw = iw_ref[...]
    iq2 = iq_ref[...].reshape(IDX_HEADS * qb, IDX_DIM)
    krow = lax.broadcasted_iota(jnp.int32, (tk, qb), 0)
    qcol = q0 + lax.broadcasted_iota(jnp.int32, (tk, qb), 1)

    def score_chunk(c):
        k0 = pl.multiple_of(c * tk, tk)
        logit = lax.dot_general(ik_ref[pl.ds(k0, tk), :], iq2, NT_DIMS, preferred_element_type=F32)
        s = jnp.zeros((tk, qb), F32)
        for h in range(IDX_HEADS):
            s = s + iw[h:h + 1, :] * jnp.maximum(logit[:, h * qb:(h + 1) * qb], 0.0)
        bits = lax.bitcast_convert_type(s, jnp.int32)
        key = bits ^ ((bits >> 31) & jnp.int32(0x7FFFFFFF))
        key = jnp.where(k0 + krow <= qcol, key, INT_MIN)
        keys_ref[c] = key
        ukey = key ^ INT_MIN
        planes = _bit_transpose32([ukey[CHUNK_ROWS * j:CHUNK_ROWS * (j + 1), :] for j in range(WORD_BITS)])
        r0 = pl.multiple_of(c * CHUNK_ROWS, CHUNK_ROWS)
        for b in range(WORD_BITS):
            planes_ref[b, pl.ds(r0, CHUNK_ROWS), :] = planes[b]

    npair = (nch + 1) // 2

    def score_pair(cp, carry):
        score_chunk(2 * cp)
        score_chunk(2 * cp + 1)
        return carry

    lax.fori_loop(0, npair, score_pair, 0)

    zero = jnp.zeros((1, qb), jnp.int32)

    def popcount_rows(x):
        cnt = lax.population_count(x)
        return jnp.sum(jnp.sum(cnt.reshape(x.shape[0] // SUBLANES, SUBLANES, LANES), axis=0), axis=0, keepdims=True)

    def radix_select(lanes, rows):
        def select_bit(b, carry):
            live, n_above, prefix = carry
            plane = planes_ref[b, 0:rows, lanes]
            ones = live & plane
            c1 = popcount_rows(ones)
            take = n_above + c1 >= topk
            n_above = jnp.where(take, n_above, n_above + c1)
            live = jnp.where(take, ones, live & ~plane)
            prefix = jnp.where(take, prefix | jnp.left_shift(jnp.int32(1), WORD_BITS - 1 - b), prefix)
            return live, n_above, prefix

        live0 = jnp.where(lax.broadcasted_iota(jnp.int32, (rows, LANES), 0) < pair_rows,
                          jnp.int32(-1), jnp.int32(0))
        z = jnp.zeros((1, LANES), jnp.int32)
        live, n_above, prefix = lax.fori_loop(0, WORD_BITS, select_bit, (live0, z, z))
        return prefix ^ INT_MIN, n_above, popcount_rows(live)

    pair_rows = 2 * CHUNK_ROWS * npair
    row_steps = sorted({r for r in (prow // 4, prow // 2, 3 * prow // 4, prow) if r % SUBLANES == 0 and r > 0})
    which = sum((pair_rows > r).astype(jnp.int32) for r in row_steps[:-1])

    def branch(lanes, rows):
        return lambda _: radix_select(lanes, rows)

    groups = []
    for g in range(qb // LANES):
        lanes = slice(g * LANES, (g + 1) * LANES)
        groups.append(lax.switch(which, [branch(lanes, r) for r in row_steps], 0))
    tau, n_gt, n_eq = (jnp.concatenate(parts, axis=1) for parts in zip(*groups))
    need = topk - n_gt

    def count(pred):
        def body(c, acc):
            hit = jnp.where(pred(keys_ref[c], c * tk + krow), 1, 0)
            return acc + jnp.sum(hit.reshape(tk // SUBLANES, SUBLANES, qb), axis=0)

        acc = lax.fori_loop(0, nch, body, jnp.zeros((SUBLANES, qb), jnp.int32))
        return jnp.sum(acc, axis=0, keepdims=True)

    def drop_surplus_ties(_):
        def body(it, p):
            cand = p + jnp.left_shift(jnp.int32(1), index_bits - 1 - it)
            below = count(lambda k, idx: (k == tau) & (idx < cand))
            return jnp.where(below < need, cand, p)
        bound = lax.fori_loop(0, index_bits, body, zero) + 1

        def drop(c, carry):
            k = keys_ref[c]
            keys_ref[c] = jnp.where((k == tau) & (c * tk + krow >= bound), INT_MIN, k)
            return carry

        lax.fori_loop(0, nch, drop, 0)
        return 0

    has_tie = jnp.max(jnp.where((n_eq > need) & (tau > INT_MIN), 1, 0)) > 0
    lax.cond(has_tie, drop_surplus_ties, lambda _: 0, 0)
    threshold = jnp.maximum(tau, INT_MIN + 1)

    q = q_ref[...]
    q2 = jnp.concatenate([q[:, h * HEAD_DIM:(h + 1) * HEAD_DIM] for h in range(N_HEADS)], axis=0)
    m_ref[...] = jnp.full(m_ref.shape, -jnp.inf, F32)
    acc_ref[...] = jnp.zeros(acc_ref.shape, F32)
    exp2_scale = HEAD_DIM ** -0.5 * np.log2(np.e)

    def attend(cp, carry):
        k0 = pl.multiple_of(cp * 2 * tk, 2 * tk)
        bias = []
        for c in (2 * cp, 2 * cp + 1):
            bias.append(jnp.where(keys_ref[c] >= threshold, 0.0, -jnp.inf))
        bias = jnp.concatenate(bias, axis=0)
        s = lax.dot_general(k_ref[pl.ds(k0, 2 * tk), :], q2, NT_DIMS, preferred_element_type=F32)
        vt = jnp.concatenate([vt_ref[2 * cp], vt_ref[2 * cp + 1]], axis=1)
        m_prev = m_ref[...]
        ps, ms, alphas = [], [], []
        for h in range(N_HEADS):
            cols = slice(h * qb, (h + 1) * qb)
            z = s[:, cols] + bias
            m_new = jnp.maximum(m_prev[:, cols], jnp.max(z, axis=0, keepdims=True))
            m_safe = jnp.where(m_new == -jnp.inf, 0.0, m_new)
            ps.append(jnp.exp2((z - m_safe) * exp2_scale).astype(BF16))
            ms.append(m_new)
            alphas.append(jnp.exp2((m_prev[:, cols] - m_safe) * exp2_scale))
        m_ref[...] = jnp.concatenate(ms, axis=1)
        acc_ref[...] = jnp.concatenate(alphas, axis=1) * acc_ref[...] + jnp.dot(
            vt, jnp.concatenate(ps, axis=1), preferred_element_type=F32)
        return carry

    lax.fori_loop(0, npair, attend, 0)
    ot = acc_ref[0:HEAD_DIM, :] / acc_ref[HEAD_DIM:HEAD_DIM + 1, :]
    o = jnp.concatenate([ot[:, h * qb:(h + 1) * qb].T for h in range(N_HEADS)], axis=1)
    g = g_ref[...]
    o_ref[...] = (o * (g * jax.nn.sigmoid(g))).astype(o_ref.dtype)


def _dsa(iq, iwt, ik, qa, ka, vt, u32, batch, seq):
    t = qa.shape[0]
    nb = seq // Q_BLOCK
    topk = min(DSA_TOPK_MAX, seq // 4)
    kern = functools.partial(_dsa_kernel, topk=topk)
    return pl.pallas_call(
        kern,
        out_shape=jax.ShapeDtypeStruct((t, BRANCH_WIDTH), BF16),
        grid=(batch, nb),
        in_specs=[pl.BlockSpec((IDX_HEADS, Q_BLOCK, IDX_DIM), lambda b, i: (0, b * nb + i, 0)),
                  pl.BlockSpec((IDX_HEADS, Q_BLOCK), lambda b, i: (0, b * nb + i)),
                  pl.BlockSpec((seq, IDX_DIM), lambda b, i: (b, 0)),
                  pl.BlockSpec((Q_BLOCK, BRANCH_WIDTH), lambda b, i: (b * nb + i, 0)),
                  pl.BlockSpec((seq, HEAD_DIM), lambda b, i: (b, 0)),
                  pl.BlockSpec((seq // DSA_TK, VT_ROWS, DSA_TK), lambda b, i: (b, 0, 0)),
                  pl.BlockSpec((Q_BLOCK, 1024), lambda b, i: (b * nb + i, BIG["a_g"]))],
        out_specs=pl.BlockSpec((Q_BLOCK, BRANCH_WIDTH), lambda b, i: (b * nb + i, 0)),
        scratch_shapes=[pltpu.VMEM((seq // DSA_TK, DSA_TK, Q_BLOCK), jnp.int32),
                        pltpu.VMEM((WORD_BITS, seq // DSA_TK * CHUNK_ROWS, Q_BLOCK), jnp.int32),
                        pltpu.VMEM((1, N_HEADS * Q_BLOCK), F32),
                        pltpu.VMEM((VT_ROWS, N_HEADS * Q_BLOCK), F32)],
        compiler_params=_cparams(("arbitrary", "arbitrary")),
        name="dsa_attention",
    )(iq, iwt, ik, qa, ka, vt, u32)


def _conv_kernel(b_ref, c_ref, x_ref, g_ref, cp_ref, xp_ref, w_ref, o_ref, *, tiles_per_seq):
    i = pl.program_id(0)
    u = c_ref[...] * x_ref[...]
    prev = cp_ref[...] * xp_ref[...]
    prev = jnp.where(i % tiles_per_seq == 0, 0.0, prev)
    row = lax.broadcasted_iota(jnp.int32, u.shape, 0)
    u1 = jnp.where(row == 0, prev[7:8, :], pltpu.roll(u, 1, axis=0))
    u2 = jnp.where(row == 0, prev[6:7, :], jnp.where(row == 1, prev[7:8, :], pltpu.roll(u, 2, axis=0)))
    w = w_ref[...]
    y = w[0:1, :] * u2 + w[1:2, :] * u1 + w[2:3, :] * u
    g = g_ref[...]
    o_ref[...] = (b_ref[...] * y * (g * jax.nn.sigmoid(g))).astype(o_ref.dtype)


def _conv(u32, conv_w, seq, tm=512):
    t = u32.shape[0]
    tps = seq // tm
    kern = functools.partial(_conv_kernel, tiles_per_seq=tps)

    def big(name):
        return pl.BlockSpec((tm, 1024), lambda i: (i, BIG[name]))

    def halo(name):
        return pl.BlockSpec((8, 1024), lambda i: (jnp.maximum(i * (tm // 8) - 1, 0), BIG[name]))

    return pl.pallas_call(
        kern,
        out_shape=jax.ShapeDtypeStruct((t, BRANCH_WIDTH), BF16),
        grid=(t // tm,),
        in_specs=[big("b_b"), big("b_c"), big("b_x"), big("b_g"), halo("b_c"), halo("b_x"),
                  pl.BlockSpec((CONV_K, BRANCH_WIDTH), lambda i: (0, 0))],
        out_specs=pl.BlockSpec((tm, BRANCH_WIDTH), lambda i: (i, 0)),
        compiler_params=_cparams(("parallel",)),
        name="short_conv",
    )(u32, u32, u32, u32, u32, u32, conv_w)


FOX_TK = 512
FOX_TQ = 1024
FOX_TC = 1024
FOX_HEADS_PER_STEP = 4


def _fox_cum_kernel(cf_ref, bias_ref, ccol_ref, crow_ref, carry_ref):
    j = pl.program_id(1)

    @pl.when(j == 0)
    def _():
        carry_ref[...] = jnp.zeros_like(carry_ref)

    x = cf_ref[...] + bias_ref[...]
    log_f = jnp.minimum(x, 0.0) - jnp.log1p(jnp.exp(-jnp.abs(x)))
    tc = x.shape[0]
    tri = (lax.broadcasted_iota(jnp.int32, (tc, tc), 1) <= lax.broadcasted_iota(jnp.int32, (tc, tc), 0)).astype(F32)
    cs = jnp.dot(tri, log_f, precision=lax.Precision.HIGHEST, preferred_element_type=F32) + carry_ref[...]
    ccol_ref[...] = cs
    carry_ref[...] = cs[tc - 1:tc, :]
    cst = cs.T
    for part in range(tc // FOX_TQ):
        crow_ref[part] = cst[0:N_HEADS, part * FOX_TQ:(part + 1) * FOX_TQ]


def _fox_cum(u32, bias, batch, seq):
    t = u32.shape[0]
    tc = FOX_TC
    ns = seq // tc
    per = tc // FOX_TQ
    bias_p = jnp.zeros((1, LANES), F32).at[0, :N_HEADS].set(bias.astype(F32))
    return pl.pallas_call(
        _fox_cum_kernel,
        out_shape=(jax.ShapeDtypeStruct((t, LANES), F32),
                   jax.ShapeDtypeStruct((t // FOX_TQ, N_HEADS, FOX_TQ), F32)),
        grid=(batch, ns),
        in_specs=[pl.BlockSpec((tc, LANES), lambda b, j: (b * ns + j, SLOT_CF)),
                  pl.BlockSpec((1, LANES), lambda b, j: (0, 0))],
        out_specs=(pl.BlockSpec((tc, LANES), lambda b, j: (b * ns + j, 0)),
                   pl.BlockSpec((per, N_HEADS, FOX_TQ), lambda b, j: (b * ns + j, 0, 0))),
        scratch_shapes=[pltpu.VMEM((1, LANES), F32)],
        compiler_params=_cparams(("parallel", "arbitrary")),
        name="fox_cumsum",
    )(u32, bias_p)


def _fox_prep_kernel(q_ref, k_ref, v_ref, qo_ref, ko_ref, vto_ref):
    tm = q_ref.shape[0]
    qo_ref[...] = q_ref[...].astype(BF16)
    ko_ref[...] = k_ref[...].astype(BF16)
    for h in range(N_HEADS):
        vto_ref[h, 0:HEAD_DIM, :] = v_ref[:, h * HEAD_DIM:(h + 1) * HEAD_DIM].T.astype(BF16)
        vto_ref[h, HEAD_DIM:VT_ROWS, :] = jnp.ones((VT_ROWS - HEAD_DIM, tm), BF16)


def _fox_prep(u32):
    t = u32.shape[0]
    tm = FOX_TK

    def big(name):
        return pl.BlockSpec((tm, 1024), lambda i: (i, BIG[name]))

    row_spec = pl.BlockSpec((tm, BRANCH_WIDTH), lambda i: (i, 0))
    return pl.pallas_call(
        _fox_prep_kernel,
        out_shape=(jax.ShapeDtypeStruct((t, BRANCH_WIDTH), BF16),
                   jax.ShapeDtypeStruct((t, BRANCH_WIDTH), BF16),
                   jax.ShapeDtypeStruct((t // tm, N_HEADS, VT_ROWS, tm), BF16)),
        grid=(t // tm,),
        in_specs=[big("c_q"), big("c_k"), big("c_v")],
        out_specs=(row_spec, row_spec, pl.BlockSpec((None, N_HEADS, VT_ROWS, tm), lambda i: (i, 0, 0, 0))),
        compiler_params=_cparams(("parallel",)),
        name="fox_prep",
    )(u32, u32, u32)


def _fox_kernel(q_ref, k_ref, vt_ref, ccol_ref, crow_ref, g_ref, o_ref, m_ref, acc_ref):
    tq, tk, hp = FOX_TQ, FOX_TK, FOX_HEADS_PER_STEP
    h0 = pl.program_id(1) * hp
    q0 = pl.program_id(2) * tq
    scale = HEAD_DIM ** -0.5
    log2e = np.log2(np.e)
    head_lane = lax.broadcasted_iota(jnp.int32, (tk, LANES), 1)

    sub = lax.broadcasted_iota(jnp.int32, (N_HEADS, tq), 0)
    crow = crow_ref[...]
    cqs = [jnp.sum(jnp.where(sub == h0 + hh, crow, 0.0), axis=0, keepdims=True) * log2e for hh in range(hp)]
    m_ref[...] = jnp.full(m_ref.shape, -jnp.inf, F32)
    acc_ref[...] = jnp.zeros(acc_ref.shape, F32)
    krow = lax.broadcasted_iota(jnp.int32, (tk, LANES), 0)
    qlane = lax.broadcasted_iota(jnp.int32, (tk, LANES), 1)

    def head_step(hh, c, k0, masked):
        hcols = slice(hh * HEAD_DIM, (hh + 1) * HEAD_DIM)
        s = lax.dot_general(k_ref[pl.ds(k0, tk), hcols], q_ref[:, hcols], NT_DIMS, preferred_element_type=F32)
        ckb = jnp.sum(jnp.where(head_lane == h0 + hh, ccol_ref[pl.ds(k0, tk), :], 0.0),
                      axis=1, keepdims=True) * log2e
        cq = cqs[hh]
        m_prev = m_ref[hh]
        ps, ms, alphas = [], [], []
        for j in range(tq // LANES):
            cols = slice(j * LANES, (j + 1) * LANES)
            y = s[:, cols] * (scale * log2e) - ckb
            if masked:
                y = jnp.where(k0 + krow <= q0 + j * LANES + qlane, y, -jnp.inf)
            m_new = jnp.maximum(m_prev[:, cols], jnp.max(y, axis=0, keepdims=True) + cq[:, cols])
            ps.append(jnp.exp2(y - (m_new - cq[:, cols])).astype(BF16))
            ms.append(m_new)
            alphas.append(jnp.exp2(m_prev[:, cols] - m_new))
        m_ref[hh] = jnp.concatenate(ms, axis=1)
        acc_ref[hh] = jnp.concatenate(alphas, axis=1) * acc_ref[hh] + jnp.dot(
            vt_ref[c, hh], jnp.concatenate(ps, axis=1), preferred_element_type=F32)

    def make_body(masked):
        def body(c, carry):
            k0 = pl.multiple_of(c * tk, tk)
            for hh in range(hp):
                head_step(hh, c, k0, masked)
            return carry
        return body

    n_full = q0 // tk
    n_all = (q0 + tq + tk - 1) // tk
    lax.fori_loop(0, n_full, make_body(False), 0)
    lax.fori_loop(n_full, n_all, make_body(True), 0)
    for hh in range(hp):
        hcols = slice(hh * HEAD_DIM, (hh + 1) * HEAD_DIM)
        g = g_ref[:, hcols]
        ot = acc_ref[hh, 0:HEAD_DIM, :] / acc_ref[hh, HEAD_DIM:HEAD_DIM + 1, :]
        o_ref[:, hcols] = (ot.T * (g * jax.nn.sigmoid(g))).astype(o_ref.dtype)


def _fox(qb, kb, vt, u32, ccol, crow, batch, seq):
    t = qb.shape[0]
    tq, hp = FOX_TQ, FOX_HEADS_PER_STEP
    wide = hp * HEAD_DIM
    nq = seq // tq
    nkc = seq // FOX_TK
    cg = BIG["c_g"] * 1024 // wide
    once = pl.Buffered(1)
    return pl.pallas_call(
        _fox_kernel,
        out_shape=jax.ShapeDtypeStruct((t, BRANCH_WIDTH), BF16),
        grid=(batch, N_HEADS // hp, nq),
        in_specs=[pl.BlockSpec((tq, wide), lambda b, h, i: (b * nq + i, h)),
                  pl.BlockSpec((seq, wide), lambda b, h, i: (b, h), pipeline_mode=once),
                  pl.BlockSpec((nkc, hp, VT_ROWS, FOX_TK), lambda b, h, i: (b, h, 0, 0), pipeline_mode=once),
                  pl.BlockSpec((seq, LANES), lambda b, h, i: (b, 0), pipeline_mode=once),
                  pl.BlockSpec((None, N_HEADS, tq), lambda b, h, i: (b * nq + i, 0, 0)),
                  pl.BlockSpec((tq, wide), lambda b, h, i: (b * nq + i, cg + h))],
        out_specs=pl.BlockSpec((tq, wide), lambda b, h, i: (b * nq + i, h)),
        scratch_shapes=[pltpu.VMEM((hp, 1, tq), F32), pltpu.VMEM((hp, VT_ROWS, tq), F32)],
        compiler_params=_cparams(("parallel", "parallel", "arbitrary")),
        name="fox_attention",
    )(qb, kb, vt, ccol, crow, u32)


def _hgrn_kernel(q_ref, f_ref, i_ref, g_ref, lb_ref, nw_ref, o_ref, st_ref):
    cs, sub = HGRN_CHUNK, HGRN_SUB
    ts = q_ref.shape[0]

    @pl.when(pl.program_id(2) == 0)
    def _():
        st_ref[...] = jnp.zeros_like(st_ref)

    tri = (lax.broadcasted_iota(jnp.int32, (cs, cs), 1) <= lax.broadcasted_iota(jnp.int32, (cs, cs), 0)).astype(F32)
    row_c = lax.broadcasted_iota(jnp.int32, (cs, HEAD_DIM), 0)
    row_h = lax.broadcasted_iota(jnp.int32, (sub // 2, HEAD_DIM), 0)
    lane_h = lax.broadcasted_iota(jnp.int32, (sub // 2, cs), 1)

    def chunk(ci, carry):
        for hh in range(HGRN_HEADS_PER_STEP):
            head_chunk(pl.multiple_of(ci * cs, cs), hh)
        return carry

    def head_chunk(r0, hh):
        cols = slice(hh * HEAD_DIM, (hh + 1) * HEAD_DIM)
        lb = lb_ref[:, cols]
        nw = nw_ref[:, cols]
        f = lb + (1.0 - lb) * jax.nn.sigmoid(f_ref[pl.ds(r0, cs), cols])
        kk = 1.0 - f
        bc = jnp.dot(tri, jnp.log(f), precision=lax.Precision.HIGHEST, preferred_element_type=F32)
        bk = bc - jnp.log(kk)
        q = q_ref[pl.ds(r0, cs), cols]
        vb = i_ref[pl.ds(r0, cs), cols].astype(BF16)
        st = st_ref[hh]
        o = lax.dot_general((q * jnp.exp(bc)).astype(BF16), st.astype(BF16), NT_DIMS, preferred_element_type=F32)
        att_rows = []
        for si in range(cs // sub):
            lo = si * sub
            bi = bc[lo:lo + sub, :]
            qi = q[lo:lo + sub, :]
            att = jnp.zeros((sub, cs), F32)
            if si > 0:
                b0 = bc[lo - 1:lo, :]
                kt = kk * jnp.exp(jnp.where(row_c < lo, b0 - bc, -jnp.inf))
                qs = qi * jnp.exp(bi - b0)
                att = lax.dot_general(qs.astype(BF16), kt.astype(BF16), NT_DIMS, preferred_element_type=F32)
            half = sub // 2
            diag = [jnp.zeros((half, cs), F32), jnp.zeros((half, cs), F32)]
            for s in range(sub):
                bs = bk[lo + s:lo + s + 1, :]
                for part in range(s // half, 2):
                    rows = slice(part * half, (part + 1) * half)
                    arg = bi[rows, :] - bs
                    if part == s // half:
                        arg = jnp.where(row_h >= s - part * half, arg, -jnp.inf)
                    p = qi[rows, :] * jnp.exp(arg)
                    diag[part] = diag[part] + jnp.where(lane_h == lo + s, jnp.sum(p, axis=1, keepdims=True), 0.0)
            att_rows.append(att + jnp.concatenate(diag, axis=0))
        att = jnp.concatenate(att_rows, axis=0)
        o = o + jnp.dot(att.astype(BF16), vb, preferred_element_type=F32)
        bl = bc[cs - 1:cs, :]
        kd = kk * jnp.exp(bl - bc)
        st_ref[hh] = st * jnp.exp(bl) + lax.dot_general(vb, kd.astype(BF16), TN_DIMS, preferred_element_type=F32)
        y = o * lax.rsqrt(jnp.mean(o * o, axis=1, keepdims=True) + NORM_EPS) * nw
        g = g_ref[pl.ds(r0, cs), cols]
        o_ref[pl.ds(r0, cs), cols] = (y * (g * jax.nn.sigmoid(g))).astype(o_ref.dtype)

    lax.fori_loop(0, ts // cs, chunk, 0)


def _hgrn(u32, lb, norm_w, batch, seq, ts=512):
    t = u32.shape[0]
    ts = min(ts, seq)
    ns = seq // ts
    hp = HGRN_HEADS_PER_STEP
    wide = hp * HEAD_DIM
    dq, df, di, dg = (BIG[n] * 1024 // wide for n in ("d_q", "d_f", "d_i", "d_g"))

    def col(c0):
        return pl.BlockSpec((ts, wide), lambda b, h, j: (b * ns + j, c0 + h))

    vec = pl.BlockSpec((1, wide), lambda b, h, j: (0, h))
    return pl.pallas_call(
        _hgrn_kernel,
        out_shape=jax.ShapeDtypeStruct((t, BRANCH_WIDTH), BF16),
        grid=(batch, N_HEADS // hp, ns),
        in_specs=[col(dq), col(df), col(di), col(dg), vec, vec],
        out_specs=pl.BlockSpec((ts, wide), lambda b, h, j: (b * ns + j, h)),
        scratch_shapes=[pltpu.VMEM((hp, HEAD_DIM, HEAD_DIM), F32)],
        compiler_params=_cparams(("parallel", "parallel", "arbitrary")),
        name="hgrn2",
    )(u32, u32, u32, u32, lb.reshape(1, BRANCH_WIDTH), norm_w.reshape(1, BRANCH_WIDTH).astype(F32))


def _merge_kernel(ya_ref, yb_ref, yc_ref, yd_ref, h_ref, wb_ref, wm_ref, bm_ref, o_ref):
    tn = o_ref.shape[1]
    h = h_ref[...]
    merged = None
    for br, y_ref in enumerate((ya_ref, yb_ref, yc_ref, yd_ref)):
        proj = jnp.dot(y_ref[...], wb_ref[br], preferred_element_type=F32)
        gate = jnp.concatenate(
            [jnp.dot(h[:, n * MERGE_BLOCK_DIM:(n + 1) * MERGE_BLOCK_DIM], wm_ref[br, n], preferred_element_type=F32)
             for n in range(tn // MERGE_BLOCK_DIM)], axis=1)
        term = jax.nn.sigmoid(gate + bm_ref[br:br + 1, :]) * proj
        merged = term if merged is None else merged + term
    o_ref[...] = merged.astype(o_ref.dtype)


def _merge(ys, h, wb, wm, bm, layer, tm=1024, tn=512):
    t = h.shape[0]
    tm = min(tm, t)
    y_spec = pl.BlockSpec((tm, BRANCH_WIDTH), lambda i, j: (i, 0))
    nmb = tn // MERGE_BLOCK_DIM
    return pl.pallas_call(
        _merge_kernel,
        out_shape=jax.ShapeDtypeStruct((t, D_MODEL), BF16),
        grid=(t // tm, D_MODEL // tn),
        in_specs=[y_spec, y_spec, y_spec, y_spec,
                  pl.BlockSpec((tm, tn), lambda i, j: (i, j)),
                  pl.BlockSpec((None, 4, BRANCH_WIDTH, tn), lambda i, j: (layer, 0, 0, j)),
                  pl.BlockSpec((None, 4, nmb, MERGE_BLOCK_DIM, MERGE_BLOCK_DIM), lambda i, j: (layer, 0, j, 0, 0)),
                  pl.BlockSpec((None, 4, tn), lambda i, j: (layer, 0, j))],
        out_specs=pl.BlockSpec((tm, tn), lambda i, j: (i, j)),
        compiler_params=_cparams(("parallel", "parallel")),
        name="gated_merge",
    )(*ys, h, wb, wm, bm)


def kernel(x, norm_w, w_in, fox_f_bias, conv_w, hgrn_gamma, hgrn_norm_w, w_branch, w_merge, b_merge, w_out, final_norm_w):
    batch, seq, d = x.shape
    depth = w_in.shape[0]
    assert d == D_MODEL and w_in.shape[-1] == IN_WIDTH
    assert seq % FOX_TC == 0 and seq % (2 * DSA_TK) == 0 and seq % Q_BLOCK == 0, "unsupported sequence length"
    xf = x.reshape(batch * seq, d)
    w_in_p = _pad_in_proj(w_in)
    wb = w_branch.astype(BF16)
    wm = w_merge.astype(BF16)
    wo = w_out.astype(BF16)
    lower = _lower_bounds(hgrn_gamma)
    tabs = _rope_tables(seq)
    for layer in range(depth):
        h = _rmsnorm(xf, norm_w[layer], BF16)
        u32 = _in_proj(h, w_in_p, layer)
        qa, ka, vt, iq, ik, iwt = _dsa_prep(u32, tabs, seq)
        y_a = _dsa(iq, iwt, ik, qa, ka, vt, u32, batch, seq)
        y_b = _conv(u32, conv_w[layer].astype(F32), seq)
        ccol, crow = _fox_cum(u32, fox_f_bias[layer], batch, seq)
        fq, fk, fvt = _fox_prep(u32)
        y_c = _fox(fq, fk, fvt, u32, ccol, crow, batch, seq)
        y_d = _hgrn(u32, lower[layer], hgrn_norm_w[layer], batch, seq)
        merged = _merge((y_a, y_b, y_c, y_d), h, wb, wm, b_merge.astype(F32), layer)
        xf = _out_proj(merged, wo, layer, xf)
    return _rmsnorm(xf, final_norm_w, F32).reshape(batch, seq, d)
```

```python
import functools

import numpy as np
import jax
import jax.numpy as jnp
from jax import lax
from jax.experimental import pallas as pl
from jax.experimental.pallas import tpu as pltpu

F32 = jnp.float32
BF16 = jnp.bfloat16

D_MODEL = 4096
HEAD_DIM = 128
ROPE_THETA = 10000.0
NORM_EPS = 1e-6
Q_BLOCK = 256
BRANCH_WIDTH = D_MODEL // 4
N_HEADS = BRANCH_WIDTH // HEAD_DIM
IDX_HEADS = 16
IDX_DIM = 64
DSA_TOPK_MAX = 256
CONV_K = 3
HGRN_CHUNK = 64
HGRN_SUB = 16
HGRN_HEADS_PER_STEP = 8
MERGE_BLOCKS = 16
MERGE_BLOCK_DIM = D_MODEL // MERGE_BLOCKS

LANES = 128
SUBLANES = 8
VT_ROWS = HEAD_DIM + 16
VMEM_LIMIT = 56 * 1024 * 1024

_SRC_LAYOUT = (
    ("a_q", 1024), ("a_k", 128), ("a_v", 128), ("a_iq", 1024), ("a_ik", 64), ("a_iw", 16), ("a_g", 1024),
    ("b_b", 1024), ("b_c", 1024), ("b_x", 1024), ("b_g", 1024),
    ("c_q", 1024), ("c_k", 1024), ("c_v", 1024), ("c_f", 8), ("c_g", 1024),
    ("d_q", 1024), ("d_f", 1024), ("d_i", 1024), ("d_g", 1024),
)
_SRC_OFF = {}
_o = 0
for _n, _w in _SRC_LAYOUT:
    _SRC_OFF[_n] = (_o, _w)
    _o += _w
IN_WIDTH = _o

_BIG = ("a_q", "a_iq", "a_g", "b_b", "b_c", "b_x", "b_g", "c_q", "c_k", "c_v", "c_g", "d_q", "d_f", "d_i", "d_g")
BIG = {n: i for i, n in enumerate(_BIG)}
SMALL_BASE = len(_BIG) * 1024 // LANES
SLOT_AK, SLOT_AV, SLOT_AIKW, SLOT_CF = SMALL_BASE, SMALL_BASE + 1, SMALL_BASE + 2, SMALL_BASE + 3
PAD_WIDTH = (SMALL_BASE + 4) * LANES

INT_MIN = np.int32(-2 ** 31)
NT_DIMS = (((1,), (1,)), ((), ()))
TN_DIMS = (((0,), (0,)), ((), ()))


def _cparams(sem):
    return pltpu.CompilerParams(dimension_semantics=sem, vmem_limit_bytes=VMEM_LIMIT)


def _pad_in_proj_kernel(w_ref, o_ref):
    cols = w_ref.shape[1]

    def put(dst, name, width=None):
        o, w = _SRC_OFF[name]
        w = w if width is None else width
        o_ref[dst:dst + w, :] = w_ref[o:o + w, :].astype(BF16)
        return dst + w

    dst = 0
    for name in _BIG:
        dst = put(dst, name)
    dst = put(dst, "a_k")
    dst = put(dst, "a_v")
    dst = put(dst, "a_ik", IDX_DIM + IDX_HEADS)
    o_ref[dst:dst + LANES - IDX_DIM - IDX_HEADS, :] = jnp.zeros((LANES - IDX_DIM - IDX_HEADS, cols), BF16)
    dst = put(dst + LANES - IDX_DIM - IDX_HEADS, "c_f")
    o_ref[dst:dst + LANES - N_HEADS, :] = jnp.zeros((LANES - N_HEADS, cols), BF16)


def _pad_in_proj(w_in, tl=128):
    w_t = jnp.swapaxes(w_in, 1, 2)
    depth, n, d = w_t.shape
    return pl.pallas_call(
        _pad_in_proj_kernel,
        out_shape=jax.ShapeDtypeStruct((depth, PAD_WIDTH, d), BF16),
        grid=(depth, d // tl),
        in_specs=[pl.BlockSpec((None, n, tl), lambda l, i: (l, 0, i))],
        out_specs=pl.BlockSpec((None, PAD_WIDTH, tl), lambda l, i: (l, 0, i)),
        compiler_params=_cparams(("parallel", "parallel")),
        name="pad_in_proj",
    )(w_t)


def _lb_kernel(g_ref, o_ref):
    g = g_ref[...]
    e = jnp.exp(g - jnp.max(g, axis=0, keepdims=True))
    sm = e / jnp.sum(e, axis=0, keepdims=True)
    acc = jnp.zeros_like(sm[0:1])
    rows = []
    for layer in range(g.shape[0]):
        acc = acc + sm[layer:layer + 1]
        rows.append(acc - sm[0:1])
    o_ref[...] = jnp.concatenate(rows, axis=0)


def _lower_bounds(gamma):
    return pl.pallas_call(
        _lb_kernel, out_shape=jax.ShapeDtypeStruct(gamma.shape, F32), name="hgrn_lower_bounds",
    )(gamma.astype(F32))


def _rmsnorm_kernel(x_ref, w_ref, o_ref):
    x = x_ref[...]
    ms = jnp.mean(x * x, axis=-1, keepdims=True)
    o_ref[...] = (x * lax.rsqrt(ms + NORM_EPS) * w_ref[...]).astype(o_ref.dtype)


def _rmsnorm(x, w, out_dtype, tm=256):
    t, d = x.shape
    return pl.pallas_call(
        _rmsnorm_kernel,
        out_shape=jax.ShapeDtypeStruct((t, d), out_dtype),
        grid=(t // tm,),
        in_specs=[pl.BlockSpec((tm, d), lambda i: (i, 0)), pl.BlockSpec((1, d), lambda i: (0, 0))],
        out_specs=pl.BlockSpec((tm, d), lambda i: (i, 0)),
        compiler_params=_cparams(("parallel",)),
        name="rmsnorm",
    )(x, w.reshape(1, d))


def _in_proj_kernel(a_ref, b_ref, o_ref):
    o_ref[...] = lax.dot_general(a_ref[...], b_ref[...], NT_DIMS, preferred_element_type=F32)


def _in_proj(h, w, layer, tm=1024, tn=512):
    t, k = h.shape
    n = w.shape[1]
    tm = min(tm, t)
    return pl.pallas_call(
        _in_proj_kernel,
        out_shape=jax.ShapeDtypeStruct((t, n), F32),
        grid=(t // tm, n // tn),
        in_specs=[pl.BlockSpec((tm, k), lambda i, j: (i, 0)),
                  pl.BlockSpec((None, tn, k), lambda i, j: (layer, j, 0))],
        out_specs=pl.BlockSpec((tm, tn), lambda i, j: (i, j)),
        compiler_params=_cparams(("parallel", "parallel")),
        name="in_proj",
    )(h, w)


def _out_proj_kernel(a_ref, b_ref, r_ref, o_ref):
    o_ref[...] = r_ref[...] + jnp.dot(a_ref[...], b_ref[...], preferred_element_type=F32)


def _out_proj(a, w, layer, resid, tm=1024, tn=512):
    t, k = a.shape
    n = w.shape[2]
    tm = min(tm, t)
    return pl.pallas_call(
        _out_proj_kernel,
        out_shape=jax.ShapeDtypeStruct((t, n), F32),
        grid=(t // tm, n // tn),
        in_specs=[pl.BlockSpec((tm, k), lambda i, j: (i, 0)),
                  pl.BlockSpec((None, k, tn), lambda i, j: (layer, 0, j)),
                  pl.BlockSpec((tm, tn), lambda i, j: (i, j))],
        out_specs=pl.BlockSpec((tm, tn), lambda i, j: (i, j)),
        compiler_params=_cparams(("parallel", "parallel")),
        name="out_proj",
    )(a, w, resid)


def _rope_tables(seq):
    pos = jnp.arange(seq).astype(F32)

    def tables(dim):
        half = dim // 2
        inv = ROPE_THETA ** (-jnp.arange(half, dtype=F32) / half)
        ang = pos[:, None] * inv[None, :]
        cos, sin = jnp.cos(ang), jnp.sin(ang)
        reps = LANES // dim
        cos_t = jnp.tile(jnp.concatenate([cos, cos], axis=1), (1, reps))
        sin_t = jnp.tile(jnp.concatenate([-sin, sin], axis=1), (1, reps))
        return cos_t, sin_t

    return tables(HEAD_DIM) + tables(IDX_DIM)


def _dsa_prep_kernel(q_ref, iq_ref, k_ref, v_ref, ikw_ref, cos_ref, sin_ref, cosi_ref, sini_ref,
                     qo_ref, ko_ref, vto_ref, iqo_ref, iko_ref, iwo_ref):
    cos, sin = cos_ref[...], sin_ref[...]
    cosi, sini = cosi_ref[...], sini_ref[...]
    tm = cos.shape[0]

    def rot_head(x):
        return x * cos + pltpu.roll(x, HEAD_DIM // 2, axis=1) * sin

    lane = lax.broadcasted_iota(jnp.int32, (tm, LANES), 1)
    first_half = (lane % IDX_DIM) < (IDX_DIM // 2)

    def rot_idx(x):
        partner = jnp.where(first_half, pltpu.roll(x, LANES - IDX_DIM // 2, axis=1),
                            pltpu.roll(x, IDX_DIM // 2, axis=1))
        return x * cosi + partner * sini

    for h in range(N_HEADS):
        qo_ref[:, h * HEAD_DIM:(h + 1) * HEAD_DIM] = rot_head(q_ref[:, h * HEAD_DIM:(h + 1) * HEAD_DIM]).astype(BF16)
    ko_ref[...] = rot_head(k_ref[...]).astype(BF16)
    vto_ref[0:HEAD_DIM, :] = v_ref[...].T.astype(BF16)
    vto_ref[HEAD_DIM:VT_ROWS, :] = jnp.ones((VT_ROWS - HEAD_DIM, tm), BF16)
    for p in range(IDX_HEADS // 2):
        r = rot_idx(iq_ref[:, p * LANES:(p + 1) * LANES]).astype(BF16)
        iqo_ref[2 * p] = r[:, :IDX_DIM]
        iqo_ref[2 * p + 1] = r[:, IDX_DIM:]
    ikw = ikw_ref[...]
    iko_ref[...] = rot_idx(ikw)[:, :IDX_DIM].astype(BF16)
    iwo_ref[...] = ikw.T[IDX_DIM:IDX_DIM + IDX_HEADS, :] * (IDX_HEADS ** -0.5 * IDX_DIM ** -0.5)


def _dsa_prep(u32, tabs, seq):
    t = u32.shape[0]
    tm = DSA_TK
    nseq = seq // tm
    cos_a, sin_a, cos_i, sin_i = tabs
    tab_spec = pl.BlockSpec((tm, LANES), lambda i: (i % nseq, 0))
    return pl.pallas_call(
        _dsa_prep_kernel,
        out_shape=(jax.ShapeDtypeStruct((t, BRANCH_WIDTH), BF16),
                   jax.ShapeDtypeStruct((t, HEAD_DIM), BF16),
                   jax.ShapeDtypeStruct((t // tm, VT_ROWS, tm), BF16),
                   jax.ShapeDtypeStruct((IDX_HEADS, t, IDX_DIM), BF16),
                   jax.ShapeDtypeStruct((t, IDX_DIM), BF16),
                   jax.ShapeDtypeStruct((IDX_HEADS, t), F32)),
        grid=(t // tm,),
        in_specs=[pl.BlockSpec((tm, 1024), lambda i: (i, BIG["a_q"])),
                  pl.BlockSpec((tm, 1024), lambda i: (i, BIG["a_iq"])),
                  pl.BlockSpec((tm, LANES), lambda i: (i, SLOT_AK)),
                  pl.BlockSpec((tm, LANES), lambda i: (i, SLOT_AV)),
                  pl.BlockSpec((tm, LANES), lambda i: (i, SLOT_AIKW)),
                  tab_spec, tab_spec, tab_spec, tab_spec],
        out_specs=(pl.BlockSpec((tm, BRANCH_WIDTH), lambda i: (i, 0)),
                   pl.BlockSpec((tm, HEAD_DIM), lambda i: (i, 0)),
                   pl.BlockSpec((None, VT_ROWS, tm), lambda i: (i, 0, 0)),
                   pl.BlockSpec((IDX_HEADS, tm, IDX_DIM), lambda i: (0, i, 0)),
                   pl.BlockSpec((tm, IDX_DIM), lambda i: (i, 0)),
                   pl.BlockSpec((IDX_HEADS, tm), lambda i: (0, i))),
        compiler_params=_cparams(("parallel",)),
        name="dsa_prep",
    )(u32, u32, u32, u32, u32, cos_a, sin_a, cos_i, sin_i)


WORD_BITS = 32
DSA_TK = 256
CHUNK_ROWS = DSA_TK // WORD_BITS


def _bit_transpose32(words):
    a = list(words)
    j, m = WORD_BITS // 2, 0x0000FFFF
    while j:
        k = 0
        while k < WORD_BITS:
            t = (a[k] ^ lax.shift_right_logical(a[k + j], jnp.int32(j))) & jnp.int32(m)
            a[k] = a[k] ^ t
            a[k + j] = a[k + j] ^ (t << j)
            k = (k + j + 1) & ~j
        j >>= 1
        m = (m ^ (m << j)) & 0xFFFFFFFF if j else m
    return a


def _dsa_kernel(iq_ref, iw_ref, ik_ref, q_ref, k_ref, vt_ref, g_ref, o_ref,
                keys_ref, planes_ref, m_ref, acc_ref, *, topk):
    qb = Q_BLOCK
    tk = DSA_TK
    i = pl.program_id(1)
    q0 = i * qb
    nch = (q0 + qb + tk - 1) // tk
    prow = planes_ref.shape[1]
    index_bits = (keys_ref.shape[0] * tk).bit_length()

    @pl.when((pl.program_id(0) == 0) & (i == 0))
    def _():
        planes_ref[...] = jnp.zeros(planes_ref.shape, jnp.int32)

    iw = iw_ref[...]
    iq2 = iq_ref[...].reshape(IDX_HEADS * qb, IDX_DIM)
    krow = lax.broadcasted_iota(jnp.int32, (tk, qb), 0)
    qcol = q0 + lax.broadcasted_iota(jnp.int32, (tk, qb), 1)

    def score_chunk(c, masked):
        k0 = pl.multiple_of(c * tk, tk)
        logit = lax.dot_general(ik_ref[pl.ds(k0, tk), :], iq2, NT_DIMS, preferred_element_type=F32)
        s = jnp.zeros((tk, qb), F32)
        for h in range(IDX_HEADS):
            s = s + iw[h:h + 1, :] * jnp.maximum(logit[:, h * qb:(h + 1) * qb], 0.0)
        bits = lax.bitcast_convert_type(s, jnp.int32)
        key = bits ^ ((bits >> 31) & jnp.int32(0x7FFFFFFF))
        if masked:
            key = jnp.where(k0 + krow <= qcol, key, INT_MIN)
        keys_ref[c] = key
        ukey = key ^ INT_MIN
        planes = _bit_transpose32([ukey[CHUNK_ROWS * j:CHUNK_ROWS * (j + 1), :] for j in range(WORD_BITS)])
        r0 = pl.multiple_of(c * CHUNK_ROWS, CHUNK_ROWS)
        for b in range(WORD_BITS):
            planes_ref[b, pl.ds(r0, CHUNK_ROWS), :] = planes[b]

    npair = (nch + 1) // 2

    def score_pairs(masked):
        def body(cp, carry):
            score_chunk(2 * cp, masked)
            score_chunk(2 * cp + 1, masked)
            return carry
        return body

    npair_full = q0 // (2 * tk)
    lax.fori_loop(0, npair_full, score_pairs(False), 0)
    lax.fori_loop(npair_full, npair, score_pairs(True), 0)

    zero = jnp.zeros((1, qb), jnp.int32)

    def popcount_rows(x):
        cnt = lax.population_count(x)
        return jnp.sum(jnp.sum(cnt.reshape(x.shape[0] // SUBLANES, SUBLANES, LANES), axis=0), axis=0, keepdims=True)

    def radix_select(lanes, rows):
        def select_bit(b, carry):
            live, n_above, prefix = carry
            plane = planes_ref[b, 0:rows, lanes]
            ones = live & plane
            c1 = popcount_rows(ones)
            take = n_above + c1 >= topk
            n_above = jnp.where(take, n_above, n_above + c1)
            live = jnp.where(take, ones, live & ~plane)
            prefix = jnp.where(take, prefix | jnp.left_shift(jnp.int32(1), WORD_BITS - 1 - b), prefix)
            return live, n_above, prefix

        live0 = jnp.where(lax.broadcasted_iota(jnp.int32, (rows, LANES), 0) < pair_rows,
                          jnp.int32(-1), jnp.int32(0))
        z = jnp.zeros((1, LANES), jnp.int32)
        live, n_above, prefix = lax.fori_loop(0, WORD_BITS, select_bit, (live0, z, z))
        return prefix ^ INT_MIN, n_above, popcount_rows(live)

    pair_rows = 2 * CHUNK_ROWS * npair
    row_steps = sorted({r for r in (prow // 4, prow // 2, 3 * prow // 4, prow) if r % SUBLANES == 0 and r > 0})
    which = sum((pair_rows > r).astype(jnp.int32) for r in row_steps[:-1])

    def branch(lanes, rows):
        return lambda _: radix_select(lanes, rows)

    groups = []
    for g in range(qb // LANES):
        lanes = slice(g * LANES, (g + 1) * LANES)
        groups.append(lax.switch(which, [branch(lanes, r) for r in row_steps], 0))
    tau, n_gt, n_eq = (jnp.concatenate(parts, axis=1) for parts in zip(*groups))
    need = topk - n_gt

    def count(pred):
        def body(c, acc):
            hit = jnp.where(pred(keys_ref[c], c * tk + krow), 1, 0)
            return acc + jnp.sum(hit.reshape(tk // SUBLANES, SUBLANES, qb), axis=0)

        acc = lax.fori_loop(0, nch, body, jnp.zeros((SUBLANES, qb), jnp.int32))
        return jnp.sum(acc, axis=0, keepdims=True)

    def drop_surplus_ties(_):
        def body(it, p):
            cand = p + jnp.left_shift(jnp.int32(1), index_bits - 1 - it)
            below = count(lambda k, idx: (k == tau) & (idx < cand))
            return jnp.where(below < need, cand, p)
        bound = lax.fori_loop(0, index_bits, body, zero) + 1

        def drop(c, carry):
            k = keys_ref[c]
            keys_ref[c] = jnp.where((k == tau) & (c * tk + krow >= bound), INT_MIN, k)
            return carry

        lax.fori_loop(0, nch, drop, 0)
        return 0

    has_tie = jnp.max(jnp.where((n_eq > need) & (tau > INT_MIN), 1, 0)) > 0
    lax.cond(has_tie, drop_surplus_ties, lambda _: 0, 0)
    threshold = jnp.maximum(tau, INT_MIN + 1)

    q = q_ref[...]
    q2 = jnp.concatenate([q[:, h * HEAD_DIM:(h + 1) * HEAD_DIM] for h in range(N_HEADS)], axis=0)
    m_ref[...] = jnp.full(m_ref.shape, -jnp.inf, F32)
    acc_ref[...] = jnp.zeros(acc_ref.shape, F32)
    exp2_scale = HEAD_DIM ** -0.5 * np.log2(np.e)

    def attend(cp, carry):
        k0 = pl.multiple_of(cp * 2 * tk, 2 * tk)
        bias = []
        for c in (2 * cp, 2 * cp + 1):
            bias.append(jnp.where(keys_ref[c] >= threshold, 0.0, -jnp.inf))
        bias = jnp.concatenate(bias, axis=0)
        s = lax.dot_general(k_ref[pl.ds(k0, 2 * tk), :], q2, NT_DIMS, preferred_element_type=F32)
        vt = jnp.concatenate([vt_ref[2 * cp], vt_ref[2 * cp + 1]], axis=1)
        m_prev = m_ref[...]
        ps, ms, alphas = [], [], []
        for h in range(N_HEADS):
            cols = slice(h * qb, (h + 1) * qb)
            z = s[:, cols] + bias
            m_new = jnp.maximum(m_prev[:, cols], jnp.max(z, axis=0, keepdims=True))
            m_safe = jnp.where(m_new == -jnp.inf, 0.0, m_new)
            ps.append(jnp.exp2((z - m_safe) * exp2_scale).astype(BF16))
            ms.append(m_new)
            alphas.append(jnp.exp2((m_prev[:, cols] - m_safe) * exp2_scale))
        m_ref[...] = jnp.concatenate(ms, axis=1)
        acc_ref[...] = jnp.concatenate(alphas, axis=1) * acc_ref[...] + jnp.dot(
            vt, jnp.concatenate(ps, axis=1), preferred_element_type=F32)
        return carry

    lax.fori_loop(0, npair, attend, 0)
    ot = acc_ref[0:HEAD_DIM, :] / acc_ref[HEAD_DIM:HEAD_DIM + 1, :]
    o = jnp.concatenate([ot[:, h * qb:(h + 1) * qb].T for h in range(N_HEADS)], axis=1)
    g = g_ref[...]
    o_ref[...] = (o * (g * jax.nn.sigmoid(g))).astype(o_ref.dtype)


def _dsa(iq, iwt, ik, qa, ka, vt, u32, batch, seq):
    t = qa.shape[0]
    nb = seq // Q_BLOCK
    topk = min(DSA_TOPK_MAX, seq // 4)
    kern = functools.partial(_dsa_kernel, topk=topk)
    return pl.pallas_call(
        kern,
        out_shape=jax.ShapeDtypeStruct((t, BRANCH_WIDTH), BF16),
        grid=(batch, nb),
        in_specs=[pl.BlockSpec((IDX_HEADS, Q_BLOCK, IDX_DIM), lambda b, i: (0, b * nb + i, 0)),
                  pl.BlockSpec((IDX_HEADS, Q_BLOCK), lambda b, i: (0, b * nb + i)),
                  pl.BlockSpec((seq, IDX_DIM), lambda b, i: (b, 0)),
                  pl.BlockSpec((Q_BLOCK, BRANCH_WIDTH), lambda b, i: (b * nb + i, 0)),
                  pl.BlockSpec((seq, HEAD_DIM), lambda b, i: (b, 0)),
                  pl.BlockSpec((seq // DSA_TK, VT_ROWS, DSA_TK), lambda b, i: (b, 0, 0)),
                  pl.BlockSpec((Q_BLOCK, 1024), lambda b, i: (b * nb + i, BIG["a_g"]))],
        out_specs=pl.BlockSpec((Q_BLOCK, BRANCH_WIDTH), lambda b, i: (b * nb + i, 0)),
        scratch_shapes=[pltpu.VMEM((seq // DSA_TK, DSA_TK, Q_BLOCK), jnp.int32),
                        pltpu.VMEM((WORD_BITS, seq // DSA_TK * CHUNK_ROWS, Q_BLOCK), jnp.int32),
                        pltpu.VMEM((1, N_HEADS * Q_BLOCK), F32),
                        pltpu.VMEM((VT_ROWS, N_HEADS * Q_BLOCK), F32)],
        compiler_params=_cparams(("arbitrary", "arbitrary")),
        name="dsa_attention",
    )(iq, iwt, ik, qa, ka, vt, u32)


def _conv_kernel(b_ref, c_ref, x_ref, g_ref, cp_ref, xp_ref, w_ref, o_ref, *, tiles_per_seq):
    i = pl.program_id(0)
    u = c_ref[...] * x_ref[...]
    prev = cp_ref[...] * xp_ref[...]
    prev = jnp.where(i % tiles_per_seq == 0, 0.0, prev)
    row = lax.broadcasted_iota(jnp.int32, u.shape, 0)
    u1 = jnp.where(row == 0, prev[7:8, :], pltpu.roll(u, 1, axis=0))
    u2 = jnp.where(row == 0, prev[6:7, :], jnp.where(row == 1, prev[7:8, :], pltpu.roll(u, 2, axis=0)))
    w = w_ref[...]
    y = w[0:1, :] * u2 + w[1:2, :] * u1 + w[2:3, :] * u
    g = g_ref[...]
    o_ref[...] = (b_ref[...] * y * (g * jax.nn.sigmoid(g))).astype(o_ref.dtype)


def _conv(u32, conv_w, seq, tm=512):
    t = u32.shape[0]
    tps = seq // tm
    kern = functools.partial(_conv_kernel, tiles_per_seq=tps)

    def big(name):
        return pl.BlockSpec((tm, 1024), lambda i: (i, BIG[name]))

    def halo(name):
        return pl.BlockSpec((8, 1024), lambda i: (jnp.maximum(i * (tm // 8) - 1, 0), BIG[name]))

    return pl.pallas_call(
        kern,
        out_shape=jax.ShapeDtypeStruct((t, BRANCH_WIDTH), BF16),
        grid=(t // tm,),
        in_specs=[big("b_b"), big("b_c"), big("b_x"), big("b_g"), halo("b_c"), halo("b_x"),
                  pl.BlockSpec((CONV_K, BRANCH_WIDTH), lambda i: (0, 0))],
        out_specs=pl.BlockSpec((tm, BRANCH_WIDTH), lambda i: (i, 0)),
        compiler_params=_cparams(("parallel",)),
        name="short_conv",
    )(u32, u32, u32, u32, u32, u32, conv_w)


FOX_TK = 512
FOX_TQ = 1024
FOX_TC = 1024
FOX_HEADS_PER_STEP = 4


def _fox_cum_kernel(cf_ref, bias_ref, ccol_ref, crow_ref, carry_ref):
    j = pl.program_id(1)

    @pl.when(j == 0)
    def _():
        carry_ref[...] = jnp.zeros_like(carry_ref)

    x = cf_ref[...] + bias_ref[...]
    log_f = jnp.minimum(x, 0.0) - jnp.log1p(jnp.exp(-jnp.abs(x)))
    tc = x.shape[0]
    tri = (lax.broadcasted_iota(jnp.int32, (tc, tc), 1) <= lax.broadcasted_iota(jnp.int32, (tc, tc), 0)).astype(F32)
    cs = jnp.dot(tri, log_f, precision=lax.Precision.HIGHEST, preferred_element_type=F32) + carry_ref[...]
    ccol_ref[...] = cs
    carry_ref[...] = cs[tc - 1:tc, :]
    cst = cs.T
    for part in range(tc // FOX_TQ):
        crow_ref[part] = cst[0:N_HEADS, part * FOX_TQ:(part + 1) * FOX_TQ]


def _fox_cum(u32, bias, batch, seq):
    t = u32.shape[0]
    tc = FOX_TC
    ns = seq // tc
    per = tc // FOX_TQ
    bias_p = jnp.zeros((1, LANES), F32).at[0, :N_HEADS].set(bias.astype(F32))
    return pl.pallas_call(
        _fox_cum_kernel,
        out_shape=(jax.ShapeDtypeStruct((t, LANES), F32),
                   jax.ShapeDtypeStruct((t // FOX_TQ, N_HEADS, FOX_TQ), F32)),
        grid=(batch, ns),
        in_specs=[pl.BlockSpec((tc, LANES), lambda b, j: (b * ns + j, SLOT_CF)),
                  pl.BlockSpec((1, LANES), lambda b, j: (0, 0))],
        out_specs=(pl.BlockSpec((tc, LANES), lambda b, j: (b * ns + j, 0)),
                   pl.BlockSpec((per, N_HEADS, FOX_TQ), lambda b, j: (b * ns + j, 0, 0))),
        scratch_shapes=[pltpu.VMEM((1, LANES), F32)],
        compiler_params=_cparams(("parallel", "arbitrary")),
        name="fox_cumsum",
    )(u32, bias_p)


def _fox_prep_kernel(q_ref, k_ref, v_ref, qo_ref, ko_ref, vto_ref):
    tm = q_ref.shape[0]
    qo_ref[...] = q_ref[...].astype(BF16)
    ko_ref[...] = k_ref[...].astype(BF16)
    for h in range(N_HEADS):
        vto_ref[h, 0:HEAD_DIM, :] = v_ref[:, h * HEAD_DIM:(h + 1) * HEAD_DIM].T.astype(BF16)
        vto_ref[h, HEAD_DIM:VT_ROWS, :] = jnp.ones((VT_ROWS - HEAD_DIM, tm), BF16)


def _fox_prep(u32):
    t = u32.shape[0]
    tm = FOX_TK

    def big(name):
        return pl.BlockSpec((tm, 1024), lambda i: (i, BIG[name]))

    row_spec = pl.BlockSpec((tm, BRANCH_WIDTH), lambda i: (i, 0))
    return pl.pallas_call(
        _fox_prep_kernel,
        out_shape=(jax.ShapeDtypeStruct((t, BRANCH_WIDTH), BF16),
                   jax.ShapeDtypeStruct((t, BRANCH_WIDTH), BF16),
                   jax.ShapeDtypeStruct((t // tm, N_HEADS, VT_ROWS, tm), BF16)),
        grid=(t // tm,),
        in_specs=[big("c_q"), big("c_k"), big("c_v")],
        out_specs=(row_spec, row_spec, pl.BlockSpec((None, N_HEADS, VT_ROWS, tm), lambda i: (i, 0, 0, 0))),
        compiler_params=_cparams(("parallel",)),
        name="fox_prep",
    )(u32, u32, u32)


def _fox_kernel(q_ref, k_ref, vt_ref, ccol_ref, crow_ref, g_ref, o_ref, m_ref, acc_ref):
    tq, tk, hp = FOX_TQ, FOX_TK, FOX_HEADS_PER_STEP
    h0 = pl.program_id(1) * hp
    q0 = pl.program_id(2) * tq
    scale = HEAD_DIM ** -0.5
    log2e = np.log2(np.e)
    head_lane = lax.broadcasted_iota(jnp.int32, (tk, LANES), 1)

    sub = lax.broadcasted_iota(jnp.int32, (N_HEADS, tq), 0)
    crow = crow_ref[...]
    cqs = [jnp.sum(jnp.where(sub == h0 + hh, crow, 0.0), axis=0, keepdims=True) * log2e for hh in range(hp)]
    m_ref[...] = jnp.full(m_ref.shape, -jnp.inf, F32)
    acc_ref[...] = jnp.zeros(acc_ref.shape, F32)
    krow = lax.broadcasted_iota(jnp.int32, (tk, LANES), 0)
    qlane = lax.broadcasted_iota(jnp.int32, (tk, LANES), 1)

    def head_step(hh, c, k0, masked):
        hcols = slice(hh * HEAD_DIM, (hh + 1) * HEAD_DIM)
        s = lax.dot_general(k_ref[pl.ds(k0, tk), hcols], q_ref[:, hcols], NT_DIMS, preferred_element_type=F32)
        ckb = jnp.sum(jnp.where(head_lane == h0 + hh, ccol_ref[pl.ds(k0, tk), :], 0.0),
                      axis=1, keepdims=True) * log2e
        cq = cqs[hh]
        m_prev = m_ref[hh]
        ps, ms, alphas = [], [], []
        for j in range(tq // LANES):
            cols = slice(j * LANES, (j + 1) * LANES)
            y = s[:, cols] * (scale * log2e) - ckb
            if masked:
                y = jnp.where(k0 + krow <= q0 + j * LANES + qlane, y, -jnp.inf)
            m_new = jnp.maximum(m_prev[:, cols], jnp.max(y, axis=0, keepdims=True) + cq[:, cols])
            ps.append(jnp.exp2(y - (m_new - cq[:, cols])).astype(BF16))
            ms.append(m_new)
            alphas.append(jnp.exp2(m_prev[:, cols] - m_new))
        m_ref[hh] = jnp.concatenate(ms, axis=1)
        acc_ref[hh] = jnp.concatenate(alphas, axis=1) * acc_ref[hh] + jnp.dot(
            vt_ref[c, hh], jnp.concatenate(ps, axis=1), preferred_element_type=F32)

    def make_body(masked):
        def body(c, carry):
            k0 = pl.multiple_of(c * tk, tk)
            for hh in range(hp):
                head_step(hh, c, k0, masked)
            return carry
        return body

    n_full = q0 // tk
    n_all = (q0 + tq + tk - 1) // tk
    lax.fori_loop(0, n_full, make_body(False), 0)
    lax.fori_loop(n_full, n_all, make_body(True), 0)
    for hh in range(hp):
        hcols = slice(hh * HEAD_DIM, (hh + 1) * HEAD_DIM)
        g = g_ref[:, hcols]
        ot = acc_ref[hh, 0:HEAD_DIM, :] / acc_ref[hh, HEAD_DIM:HEAD_DIM + 1, :]
        o_ref[:, hcols] = (ot.T * (g * jax.nn.sigmoid(g))).astype(o_ref.dtype)


def _fox(qb, kb, vt, u32, ccol, crow, batch, seq):
    t = qb.shape[0]
    tq, hp = FOX_TQ, FOX_HEADS_PER_STEP
    wide = hp * HEAD_DIM
    nq = seq // tq
    nkc = seq // FOX_TK
    cg = BIG["c_g"] * 1024 // wide
    once = pl.Buffered(1)
    return pl.pallas_call(
        _fox_kernel,
        out_shape=jax.ShapeDtypeStruct((t, BRANCH_WIDTH), BF16),
        grid=(batch, N_HEADS // hp, nq),
        in_specs=[pl.BlockSpec((tq, wide), lambda b, h, i: (b * nq + i, h)),
                  pl.BlockSpec((seq, wide), lambda b, h, i: (b, h), pipeline_mode=once),
                  pl.BlockSpec((nkc, hp, VT_ROWS, FOX_TK), lambda b, h, i: (b, h, 0, 0), pipeline_mode=once),
                  pl.BlockSpec((seq, LANES), lambda b, h, i: (b, 0), pipeline_mode=once),
                  pl.BlockSpec((None, N_HEADS, tq), lambda b, h, i: (b * nq + i, 0, 0)),
                  pl.BlockSpec((tq, wide), lambda b, h, i: (b * nq + i, cg + h))],
        out_specs=pl.BlockSpec((tq, wide), lambda b, h, i: (b * nq + i, h)),
        scratch_shapes=[pltpu.VMEM((hp, 1, tq), F32), pltpu.VMEM((hp, VT_ROWS, tq), F32)],
        compiler_params=_cparams(("parallel", "parallel", "arbitrary")),
        name="fox_attention",
    )(qb, kb, vt, ccol, crow, u32)


def _hgrn_kernel(q_ref, f_ref, i_ref, g_ref, lb_ref, nw_ref, o_ref, st_ref):
    cs, sub = HGRN_CHUNK, HGRN_SUB
    ts = q_ref.shape[0]

    @pl.when(pl.program_id(2) == 0)
    def _():
        st_ref[...] = jnp.zeros_like(st_ref)

    tri = (lax.broadcasted_iota(jnp.int32, (cs, cs), 1) <= lax.broadcasted_iota(jnp.int32, (cs, cs), 0)).astype(F32)
    row_c = lax.broadcasted_iota(jnp.int32, (cs, HEAD_DIM), 0)
    row_h = lax.broadcasted_iota(jnp.int32, (sub // 2, HEAD_DIM), 0)
    lane_h = lax.broadcasted_iota(jnp.int32, (sub // 2, cs), 1)

    def chunk(ci, carry):
        for hh in range(HGRN_HEADS_PER_STEP):
            head_chunk(pl.multiple_of(ci * cs, cs), hh)
        return carry

    def head_chunk(r0, hh):
        cols = slice(hh * HEAD_DIM, (hh + 1) * HEAD_DIM)
        lb = lb_ref[:, cols]
        nw = nw_ref[:, cols]
        f = lb + (1.0 - lb) * jax.nn.sigmoid(f_ref[pl.ds(r0, cs), cols])
        kk = 1.0 - f
        bc = jnp.dot(tri, jnp.log(f), precision=lax.Precision.HIGHEST, preferred_element_type=F32)
        bk = bc - jnp.log(kk)
        q = q_ref[pl.ds(r0, cs), cols]
        vb = i_ref[pl.ds(r0, cs), cols].astype(BF16)
        st = st_ref[hh]
        o = lax.dot_general((q * jnp.exp(bc)).astype(BF16), st.astype(BF16), NT_DIMS, preferred_element_type=F32)
        att_rows = []
        for si in range(cs // sub):
            lo = si * sub
            bi = bc[lo:lo + sub, :]
            qi = q[lo:lo + sub, :]
            att = jnp.zeros((sub, cs), F32)
            if si > 0:
                b0 = bc[lo - 1:lo, :]
                kt = kk * jnp.exp(jnp.where(row_c < lo, b0 - bc, -jnp.inf))
                qs = qi * jnp.exp(bi - b0)
                att = lax.dot_general(qs.astype(BF16), kt.astype(BF16), NT_DIMS, preferred_element_type=F32)
            half = sub // 2
            diag = [jnp.zeros((half, cs), F32), jnp.zeros((half, cs), F32)]
            for s in range(sub):
                bs = bk[lo + s:lo + s + 1, :]
                for part in range(s // half, 2):
                    rows = slice(part * half, (part + 1) * half)
                    arg = bi[rows, :] - bs
                    if part == s // half:
                        arg = jnp.where(row_h >= s - part * half, arg, -jnp.inf)
                    p = qi[rows, :] * jnp.exp(arg)
                    diag[part] = diag[part] + jnp.where(lane_h == lo + s, jnp.sum(p, axis=1, keepdims=True), 0.0)
            att_rows.append(att + jnp.concatenate(diag, axis=0))
        att = jnp.concatenate(att_rows, axis=0)
        o = o + jnp.dot(att.astype(BF16), vb, preferred_element_type=F32)
        bl = bc[cs - 1:cs, :]
        kd = kk * jnp.exp(bl - bc)
        st_ref[hh] = st * jnp.exp(bl) + lax.dot_general(vb, kd.astype(BF16), TN_DIMS, preferred_element_type=F32)
        y = o * lax.rsqrt(jnp.mean(o * o, axis=1, keepdims=True) + NORM_EPS) * nw
        g = g_ref[pl.ds(r0, cs), cols]
        o_ref[pl.ds(r0, cs), cols] = (y * (g * jax.nn.sigmoid(g))).astype(o_ref.dtype)

    lax.fori_loop(0, ts // cs, chunk, 0)


def _hgrn(u32, lb, norm_w, batch, seq, ts=512):
    t = u32.shape[0]
    ts = min(ts, seq)
    ns = seq // ts
    hp = HGRN_HEADS_PER_STEP
    wide = hp * HEAD_DIM
    dq, df, di, dg = (BIG[n] * 1024 // wide for n in ("d_q", "d_f", "d_i", "d_g"))

    def col(c0):
        return pl.BlockSpec((ts, wide), lambda b, h, j: (b * ns + j, c0 + h))

    vec = pl.BlockSpec((1, wide), lambda b, h, j: (0, h))
    return pl.pallas_call(
        _hgrn_kernel,
        out_shape=jax.ShapeDtypeStruct((t, BRANCH_WIDTH), BF16),
        grid=(batch, N_HEADS // hp, ns),
        in_specs=[col(dq), col(df), col(di), col(dg), vec, vec],
        out_specs=pl.BlockSpec((ts, wide), lambda b, h, j: (b * ns + j, h)),
        scratch_shapes=[pltpu.VMEM((hp, HEAD_DIM, HEAD_DIM), F32)],
        compiler_params=_cparams(("parallel", "parallel", "arbitrary")),
        name="hgrn2",
    )(u32, u32, u32, u32, lb.reshape(1, BRANCH_WIDTH), norm_w.reshape(1, BRANCH_WIDTH).astype(F32))


def _merge_kernel(ya_ref, yb_ref, yc_ref, yd_ref, h_ref, wb_ref, wm_ref, bm_ref, o_ref):
    tn = o_ref.shape[1]
    h = h_ref[...]
    merged = None
    for br, y_ref in enumerate((ya_ref, yb_ref, yc_ref, yd_ref)):
        proj = jnp.dot(y_ref[...], wb_ref[br], preferred_element_type=F32)
        gate = jnp.concatenate(
            [jnp.dot(h[:, n * MERGE_BLOCK_DIM:(n + 1) * MERGE_BLOCK_DIM], wm_ref[br, n], preferred_element_type=F32)
             for n in range(tn // MERGE_BLOCK_DIM)], axis=1)
        term = jax.nn.sigmoid(gate + bm_ref[br:br + 1, :]) * proj
        merged = term if merged is None else merged + term
    o_ref[...] = merged.astype(o_ref.dtype)


def _merge(ys, h, wb, wm, bm, layer, tm=1024, tn=512):
    t = h.shape[0]
    tm = min(tm, t)
    y_spec = pl.BlockSpec((tm, BRANCH_WIDTH), lambda i, j: (i, 0))
    nmb = tn // MERGE_BLOCK_DIM
    return pl.pallas_call(
        _merge_kernel,
        out_shape=jax.ShapeDtypeStruct((t, D_MODEL), BF16),
        grid=(t // tm, D_MODEL // tn),
        in_specs=[y_spec, y_spec, y_spec, y_spec,
                  pl.BlockSpec((tm, tn), lambda i, j: (i, j)),
                  pl.BlockSpec((None, 4, BRANCH_WIDTH, tn), lambda i, j: (layer, 0, 0, j)),
                  pl.BlockSpec((None, 4, nmb, MERGE_BLOCK_DIM, MERGE_BLOCK_DIM), lambda i, j: (layer, 0, j, 0, 0)),
                  pl.BlockSpec((None, 4, tn), lambda i, j: (layer, 0, j))],
        out_specs=pl.BlockSpec((tm, tn), lambda i, j: (i, j)),
        compiler_params=_cparams(("parallel", "parallel")),
        name="gated_merge",
    )(*ys, h, wb, wm, bm)


def kernel(x, norm_w, w_in, fox_f_bias, conv_w, hgrn_gamma, hgrn_norm_w, w_branch, w_merge, b_merge, w_out, final_norm_w):
    batch, seq, d = x.shape
    depth = w_in.shape[0]
    assert d == D_MODEL and w_in.shape[-1] == IN_WIDTH
    assert seq % FOX_TC == 0 and seq % (2 * DSA_TK) == 0 and seq % Q_BLOCK == 0, "unsupported sequence length"
    xf = x.reshape(batch * seq, d)
    w_in_p = _pad_in_proj(w_in)
    wb = w_branch.astype(BF16)
    wm = w_merge.astype(BF16)
    wo = w_out.astype(BF16)
    lower = _lower_bounds(hgrn_gamma)
    tabs = _rope_tables(seq)
    for layer in range(depth):
        h = _rmsnorm(xf, norm_w[layer], BF16)
        u32 = _in_proj(h, w_in_p, layer)
        qa, ka, vt, iq, ik, iwt = _dsa_prep(u32, tabs, seq)
        y_a = _dsa(iq, iwt, ik, qa, ka, vt, u32, batch, seq)
        y_b = _conv(u32, conv_w[layer].astype(F32), seq)
        ccol, crow = _fox_cum(u32, fox_f_bias[layer], batch, seq)
        fq, fk, fvt = _fox_prep(u32)
        y_c = _fox(fq, fk, fvt, u32, ccol, crow, batch, seq)
        y_d = _hgrn(u32, lower[layer], hgrn_norm_w[layer], batch, seq)
        merged = _merge((y_a, y_b, y_c, y_d), h, wb, wm, b_merge.astype(F32), layer)
        xf = _out_proj(merged, wo, layer, xf)
    return _rmsnorm(xf, final_norm_w, F32).reshape(batch, seq, d)
```

```python
import functools

import numpy as np
import jax
import jax.numpy as jnp
from jax import lax
from jax.experimental import pallas as pl
from jax.experimental.pallas import tpu as pltpu

F32 = jnp.float32
BF16 = jnp.bfloat16

D_MODEL = 4096
HEAD_DIM = 128
ROPE_THETA = 10000.0
NORM_EPS = 1e-6
Q_BLOCK = 256
BRANCH_WIDTH = D_MODEL // 4
N_HEADS = BRANCH_WIDTH // HEAD_DIM
IDX_HEADS = 16
IDX_DIM = 64
DSA_TOPK_MAX = 256
CONV_K = 3
HGRN_CHUNK = 64
HGRN_SUB = 16
HGRN_HEADS_PER_STEP = 8
MERGE_BLOCKS = 16
MERGE_BLOCK_DIM = D_MODEL // MERGE_BLOCKS

LANES = 128
SUBLANES = 8
VT_ROWS = HEAD_DIM + 16
VMEM_LIMIT = 56 * 1024 * 1024

_SRC_LAYOUT = (
    ("a_q", 1024), ("a_k", 128), ("a_v", 128), ("a_iq", 1024), ("a_ik", 64), ("a_iw", 16), ("a_g", 1024),
    ("b_b", 1024), ("b_c", 1024), ("b_x", 1024), ("b_g", 1024),
    ("c_q", 1024), ("c_k", 1024), ("c_v", 1024), ("c_f", 8), ("c_g", 1024),
    ("d_q", 1024), ("d_f", 1024), ("d_i", 1024), ("d_g", 1024),
)
_SRC_OFF = {}
_o = 0
for _n, _w in _SRC_LAYOUT:
    _SRC_OFF[_n] = (_o, _w)
    _o += _w
IN_WIDTH = _o

_BIG = ("a_q", "a_iq", "a_g", "b_b", "b_c", "b_x", "b_g", "c_q", "c_k", "c_v", "c_g", "d_q", "d_f", "d_i", "d_g")
BIG = {n: i for i, n in enumerate(_BIG)}
SMALL_BASE = len(_BIG) * 1024 // LANES
SLOT_AK, SLOT_AV, SLOT_AIKW, SLOT_CF = SMALL_BASE, SMALL_BASE + 1, SMALL_BASE + 2, SMALL_BASE + 3
PAD_WIDTH = (SMALL_BASE + 4) * LANES

INT_MIN = np.int32(-2 ** 31)
NT_DIMS = (((1,), (1,)), ((), ()))
TN_DIMS = (((0,), (0,)), ((), ()))


def _cparams(sem):
    return pltpu.CompilerParams(dimension_semantics=sem, vmem_limit_bytes=VMEM_LIMIT)


def _pad_in_proj_kernel(w_ref, o_ref):
    cols = w_ref.shape[1]

    def put(dst, name, width=None):
        o, w = _SRC_OFF[name]
        w = w if width is None else width
        o_ref[dst:dst + w, :] = w_ref[o:o + w, :].astype(BF16)
        return dst + w

    dst = 0
    for name in _BIG:
        dst = put(dst, name)
    dst = put(dst, "a_k")
    dst = put(dst, "a_v")
    dst = put(dst, "a_ik", IDX_DIM + IDX_HEADS)
    o_ref[dst:dst + LANES - IDX_DIM - IDX_HEADS, :] = jnp.zeros((LANES - IDX_DIM - IDX_HEADS, cols), BF16)
    dst = put(dst + LANES - IDX_DIM - IDX_HEADS, "c_f")
    o_ref[dst:dst + LANES - N_HEADS, :] = jnp.zeros((LANES - N_HEADS, cols), BF16)


def _pad_in_proj(w_in, tl=128):
    w_t = jnp.swapaxes(w_in, 1, 2)
    depth, n, d = w_t.shape
    return pl.pallas_call(
        _pad_in_proj_kernel,
        out_shape=jax.ShapeDtypeStruct((depth, PAD_WIDTH, d), BF16),
        grid=(depth, d // tl),
        in_specs=[pl.BlockSpec((None, n, tl), lambda l, i: (l, 0, i))],
        out_specs=pl.BlockSpec((None, PAD_WIDTH, tl), lambda l, i: (l, 0, i)),
        compiler_params=_cparams(("parallel", "parallel")),
        name="pad_in_proj",
    )(w_t)


def _lb_kernel(g_ref, o_ref):
    g = g_ref[...]
    e = jnp.exp(g - jnp.max(g, axis=0, keepdims=True))
    sm = e / jnp.sum(e, axis=0, keepdims=True)
    acc = jnp.zeros_like(sm[0:1])
    rows = []
    for layer in range(g.shape[0]):
        acc = acc + sm[layer:layer + 1]
        rows.append(acc - sm[0:1])
    o_ref[...] = jnp.concatenate(rows, axis=0)


def _lower_bounds(gamma):
    return pl.pallas_call(
        _lb_kernel, out_shape=jax.ShapeDtypeStruct(gamma.shape, F32), name="hgrn_lower_bounds",
    )(gamma.astype(F32))


def _rmsnorm_kernel(x_ref, w_ref, o_ref):
    x = x_ref[...]
    ms = jnp.mean(x * x, axis=-1, keepdims=True)
    o_ref[...] = (x * lax.rsqrt(ms + NORM_EPS) * w_ref[...]).astype(o_ref.dtype)


def _rmsnorm(x, w, out_dtype, tm=256):
    t, d = x.shape
    return pl.pallas_call(
        _rmsnorm_kernel,
        out_shape=jax.ShapeDtypeStruct((t, d), out_dtype),
        grid=(t // tm,),
        in_specs=[pl.BlockSpec((tm, d), lambda i: (i, 0)), pl.BlockSpec((1, d), lambda i: (0, 0))],
        out_specs=pl.BlockSpec((tm, d), lambda i: (i, 0)),
        compiler_params=_cparams(("parallel",)),
        name="rmsnorm",
    )(x, w.reshape(1, d))


def _in_proj_kernel(a_ref, b_ref, o_ref):
    o_ref[...] = lax.dot_general(a_ref[...], b_ref[...], NT_DIMS, preferred_element_type=F32)


def _in_proj(h, w, layer, tm=1024, tn=512):
    t, k = h.shape
    n = w.shape[1]
    tm = min(tm, t)
    return pl.pallas_call(
        _in_proj_kernel,
        out_shape=jax.ShapeDtypeStruct((t, n), F32),
        grid=(t // tm, n // tn),
        in_specs=[pl.BlockSpec((tm, k), lambda i, j: (i, 0)),
                  pl.BlockSpec((None, tn, k), lambda i, j: (layer, j, 0))],
        out_specs=pl.BlockSpec((tm, tn), lambda i, j: (i, j)),
        compiler_params=_cparams(("parallel", "parallel")),
        name="in_proj",
    )(h, w)


def _out_proj_kernel(a_ref, b_ref, r_ref, o_ref):
    o_ref[...] = r_ref[...] + jnp.dot(a_ref[...], b_ref[...].astype(BF16), preferred_element_type=F32)


def _out_proj(a, w, layer, resid, tm=1024, tn=512):
    t, k = a.shape
    n = w.shape[2]
    tm = min(tm, t)
    return pl.pallas_call(
        _out_proj_kernel,
        out_shape=jax.ShapeDtypeStruct((t, n), F32),
        grid=(t // tm, n // tn),
        in_specs=[pl.BlockSpec((tm, k), lambda i, j: (i, 0)),
                  pl.BlockSpec((None, k, tn), lambda i, j: (layer, 0, j)),
                  pl.BlockSpec((tm, tn), lambda i, j: (i, j))],
        out_specs=pl.BlockSpec((tm, tn), lambda i, j: (i, j)),
        compiler_params=_cparams(("parallel", "parallel")),
        name="out_proj",
    )(a, w, resid)


def _rope_tables(seq):
    pos = jnp.arange(seq).astype(F32)

    def tables(dim):
        half = dim // 2
        inv = ROPE_THETA ** (-jnp.arange(half, dtype=F32) / half)
        ang = pos[:, None] * inv[None, :]
        cos, sin = jnp.cos(ang), jnp.sin(ang)
        reps = LANES // dim
        cos_t = jnp.tile(jnp.concatenate([cos, cos], axis=1), (1, reps))
        sin_t = jnp.tile(jnp.concatenate([-sin, sin], axis=1), (1, reps))
        return cos_t, sin_t

    return tables(HEAD_DIM) + tables(IDX_DIM)


def _dsa_prep_kernel(q_ref, iq_ref, k_ref, v_ref, ikw_ref, cos_ref, sin_ref, cosi_ref, sini_ref,
                     qo_ref, ko_ref, vto_ref, iqo_ref, iko_ref, iwo_ref):
    cos, sin = cos_ref[...], sin_ref[...]
    cosi, sini = cosi_ref[...], sini_ref[...]
    tm = cos.shape[0]

    def rot_head(x):
        return x * cos + pltpu.roll(x, HEAD_DIM // 2, axis=1) * sin

    lane = lax.broadcasted_iota(jnp.int32, (tm, LANES), 1)
    first_half = (lane % IDX_DIM) < (IDX_DIM // 2)

    def rot_idx(x):
        partner = jnp.where(first_half, pltpu.roll(x, LANES - IDX_DIM // 2, axis=1),
                            pltpu.roll(x, IDX_DIM // 2, axis=1))
        return x * cosi + partner * sini

    for h in range(N_HEADS):
        qo_ref[:, h * HEAD_DIM:(h + 1) * HEAD_DIM] = rot_head(q_ref[:, h * HEAD_DIM:(h + 1) * HEAD_DIM]).astype(BF16)
    ko_ref[...] = rot_head(k_ref[...]).astype(BF16)
    vto_ref[0:HEAD_DIM, :] = v_ref[...].T.astype(BF16)
    vto_ref[HEAD_DIM:VT_ROWS, :] = jnp.ones((VT_ROWS - HEAD_DIM, tm), BF16)
    for p in range(IDX_HEADS // 2):
        r = rot_idx(iq_ref[:, p * LANES:(p + 1) * LANES]).astype(BF16)
        iqo_ref[2 * p] = r[:, :IDX_DIM]
        iqo_ref[2 * p + 1] = r[:, IDX_DIM:]
    ikw = ikw_ref[...]
    iko_ref[...] = rot_idx(ikw)[:, :IDX_DIM].astype(BF16)
    iwo_ref[...] = ikw.T[IDX_DIM:IDX_DIM + IDX_HEADS, :] * (IDX_HEADS ** -0.5 * IDX_DIM ** -0.5)


def _dsa_prep(u32, tabs, seq):
    t = u32.shape[0]
    tm = DSA_TK
    nseq = seq // tm
    cos_a, sin_a, cos_i, sin_i = tabs
    tab_spec = pl.BlockSpec((tm, LANES), lambda i: (i % nseq, 0))
    return pl.pallas_call(
        _dsa_prep_kernel,
        out_shape=(jax.ShapeDtypeStruct((t, BRANCH_WIDTH), BF16),
                   jax.ShapeDtypeStruct((t, HEAD_DIM), BF16),
                   jax.ShapeDtypeStruct((t // tm, VT_ROWS, tm), BF16),
                   jax.ShapeDtypeStruct((IDX_HEADS, t, IDX_DIM), BF16),
                   jax.ShapeDtypeStruct((t, IDX_DIM), BF16),
                   jax.ShapeDtypeStruct((IDX_HEADS, t), F32)),
        grid=(t // tm,),
        in_specs=[pl.BlockSpec((tm, 1024), lambda i: (i, BIG["a_q"])),
                  pl.BlockSpec((tm, 1024), lambda i: (i, BIG["a_iq"])),
                  pl.BlockSpec((tm, LANES), lambda i: (i, SLOT_AK)),
                  pl.BlockSpec((tm, LANES), lambda i: (i, SLOT_AV)),
                  pl.BlockSpec((tm, LANES), lambda i: (i, SLOT_AIKW)),
                  tab_spec, tab_spec, tab_spec, tab_spec],
        out_specs=(pl.BlockSpec((tm, BRANCH_WIDTH), lambda i: (i, 0)),
                   pl.BlockSpec((tm, HEAD_DIM), lambda i: (i, 0)),
                   pl.BlockSpec((None, VT_ROWS, tm), lambda i: (i, 0, 0)),
                   pl.BlockSpec((IDX_HEADS, tm, IDX_DIM), lambda i: (0, i, 0)),
                   pl.BlockSpec((tm, IDX_DIM), lambda i: (i, 0)),
                   pl.BlockSpec((IDX_HEADS, tm), lambda i: (0, i))),
        compiler_params=_cparams(("parallel",)),
        name="dsa_prep",
    )(u32, u32, u32, u32, u32, cos_a, sin_a, cos_i, sin_i)


WORD_BITS = 32
DSA_TK = 256
CHUNK_ROWS = DSA_TK // WORD_BITS


def _bit_transpose32(words):
    a = list(words)
    j, m = WORD_BITS // 2, 0x0000FFFF
    while j:
        k = 0
        while k < WORD_BITS:
            t = (a[k] ^ lax.shift_right_logical(a[k + j], jnp.int32(j))) & jnp.int32(m)
            a[k] = a[k] ^ t
            a[k + j] = a[k + j] ^ (t << j)
            k = (k + j + 1) & ~j
        j >>= 1
        m = (m ^ (m << j)) & 0xFFFFFFFF if j else m
    return a


def _dsa_kernel(iq_ref, iw_ref, ik_ref, q_ref, k_ref, vt_ref, g_ref, o_ref,
                keys_ref, planes_ref, m_ref, acc_ref, *, topk):
    qb = Q_BLOCK
    tk = DSA_TK
    i = pl.program_id(1)
    q0 = i * qb
    nch = (q0 + qb + tk - 1) // tk
    prow = planes_ref.shape[1]
    index_bits = (keys_ref.shape[0] * tk).bit_length()

    @pl.when((pl.program_id(0) == 0) & (i == 0))
    def _():
        planes_ref[...] = jnp.zeros(planes_ref.shape, jnp.int32)

    iw = iw_ref[...]
    iq2 = iq_ref[...].reshape(IDX_HEADS * qb, IDX_DIM)
    krow = lax.broadcasted_iota(jnp.int32, (tk, qb), 0)
    qcol = q0 + lax.broadcasted_iota(jnp.int32, (tk, qb), 1)

    def score_chunk(c, masked):
        k0 = pl.multiple_of(c * tk, tk)
        logit = lax.dot_general(ik_ref[pl.ds(k0, tk), :], iq2, NT_DIMS, preferred_element_type=F32)
        s = jnp.zeros((tk, qb), F32)
        for h in range(IDX_HEADS):
            s = s + iw[h:h + 1, :] * jnp.maximum(logit[:, h * qb:(h + 1) * qb], 0.0)
        bits = lax.bitcast_convert_type(s, jnp.int32)
        key = bits ^ ((bits >> 31) & jnp.int32(0x7FFFFFFF))
        if masked:
            key = jnp.where(k0 + krow <= qcol, key, INT_MIN)
        keys_ref[c] = key
        ukey = key ^ INT_MIN
        planes = _bit_transpose32([ukey[CHUNK_ROWS * j:CHUNK_ROWS * (j + 1), :] for j in range(WORD_BITS)])
        r0 = pl.multiple_of(c * CHUNK_ROWS, CHUNK_ROWS)
        for b in range(WORD_BITS):
            planes_ref[b, pl.ds(r0, CHUNK_ROWS), :] = planes[b]

    npair = (nch + 1) // 2

    def score_pairs(masked):
        def body(cp, carry):
            score_chunk(2 * cp, masked)
            score_chunk(2 * cp + 1, masked)
            return carry
        return body

    npair_full = q0 // (2 * tk)
    lax.fori_loop(0, npair_full, score_pairs(False), 0)
    lax.fori_loop(npair_full, npair, score_pairs(True), 0)

    zero = jnp.zeros((1, qb), jnp.int32)

    def popcount_rows(x):
        cnt = lax.population_count(x)
        return jnp.sum(jnp.sum(cnt.reshape(x.shape[0] // SUBLANES, SUBLANES, LANES), axis=0), axis=0, keepdims=True)

    def radix_select(lanes, rows):
        def select_bit(b, carry):
            live, n_above, prefix = carry
            plane = planes_ref[b, 0:rows, lanes]
            ones = live & plane
            c1 = popcount_rows(ones)
            take = n_above + c1 >= topk
            n_above = jnp.where(take, n_above, n_above + c1)
            live = jnp.where(take, ones, live & ~plane)
            prefix = jnp.where(take, prefix | jnp.left_shift(jnp.int32(1), WORD_BITS - 1 - b), prefix)
            return live, n_above, prefix

        live0 = jnp.where(lax.broadcasted_iota(jnp.int32, (rows, LANES), 0) < pair_rows,
                          jnp.int32(-1), jnp.int32(0))
        z = jnp.zeros((1, LANES), jnp.int32)
        live, n_above, prefix = lax.fori_loop(0, WORD_BITS, select_bit, (live0, z, z))
        return prefix ^ INT_MIN, n_above, popcount_rows(live)

    pair_rows = 2 * CHUNK_ROWS * npair
    row_steps = sorted({r for r in (prow // 4, prow // 2, 3 * prow // 4, prow) if r % SUBLANES == 0 and r > 0})
    which = sum((pair_rows > r).astype(jnp.int32) for r in row_steps[:-1])

    def branch(lanes, rows):
        return lambda _: radix_select(lanes, rows)

    groups = []
    for g in range(qb // LANES):
        lanes = slice(g * LANES, (g + 1) * LANES)
        groups.append(lax.switch(which, [branch(lanes, r) for r in row_steps], 0))
    tau, n_gt, n_eq = (jnp.concatenate(parts, axis=1) for parts in zip(*groups))
    need = topk - n_gt

    def count(pred):
        def body(c, acc):
            hit = jnp.where(pred(keys_ref[c], c * tk + krow), 1, 0)
            return acc + jnp.sum(hit.reshape(tk // SUBLANES, SUBLANES, qb), axis=0)

        acc = lax.fori_loop(0, nch, body, jnp.zeros((SUBLANES, qb), jnp.int32))
        return jnp.sum(acc, axis=0, keepdims=True)

    def drop_surplus_ties(_):
        def body(it, p):
            cand = p + jnp.left_shift(jnp.int32(1), index_bits - 1 - it)
            below = count(lambda k, idx: (k == tau) & (idx < cand))
            return jnp.where(below < need, cand, p)
        bound = lax.fori_loop(0, index_bits, body, zero) + 1

        def drop(c, carry):
            k = keys_ref[c]
            keys_ref[c] = jnp.where((k == tau) & (c * tk + krow >= bound), INT_MIN, k)
            return carry

        lax.fori_loop(0, nch, drop, 0)
        return 0

    has_tie = jnp.max(jnp.where((n_eq > need) & (tau > INT_MIN), 1, 0)) > 0
    lax.cond(has_tie, drop_surplus_ties, lambda _: 0, 0)
    threshold = jnp.maximum(tau, INT_MIN + 1)

    q = q_ref[...]
    q2 = jnp.concatenate([q[:, h * HEAD_DIM:(h + 1) * HEAD_DIM] for h in range(N_HEADS)], axis=0)
    m_ref[...] = jnp.full(m_ref.shape, -jnp.inf, F32)
    acc_ref[...] = jnp.zeros(acc_ref.shape, F32)
    exp2_scale = HEAD_DIM ** -0.5 * np.log2(np.e)

    def attend(cp, carry):
        k0 = pl.multiple_of(cp * 2 * tk, 2 * tk)
        bias = []
        for c in (2 * cp, 2 * cp + 1):
            bias.append(jnp.where(keys_ref[c] >= threshold, 0.0, -jnp.inf))
        bias = jnp.concatenate(bias, axis=0)
        s = lax.dot_general(k_ref[pl.ds(k0, 2 * tk), :], q2, NT_DIMS, preferred_element_type=F32)
        vt = jnp.concatenate([vt_ref[2 * cp], vt_ref[2 * cp + 1]], axis=1)
        m_prev = m_ref[...]
        ps, ms, alphas = [], [], []
        for h in range(N_HEADS):
            cols = slice(h * qb, (h + 1) * qb)
            z = s[:, cols] + bias
            m_new = jnp.maximum(m_prev[:, cols], jnp.max(z, axis=0, keepdims=True))
            m_safe = jnp.where(m_new == -jnp.inf, 0.0, m_new)
            ps.append(jnp.exp2((z - m_safe) * exp2_scale).astype(BF16))
            ms.append(m_new)
            alphas.append(jnp.exp2((m_prev[:, cols] - m_safe) * exp2_scale))
        m_ref[...] = jnp.concatenate(ms, axis=1)
        acc_ref[...] = jnp.concatenate(alphas, axis=1) * acc_ref[...] + jnp.dot(
            vt, jnp.concatenate(ps, axis=1), preferred_element_type=F32)
        return carry

    lax.fori_loop(0, npair, attend, 0)
    ot = acc_ref[0:HEAD_DIM, :] / acc_ref[HEAD_DIM:HEAD_DIM + 1, :]
    o = jnp.concatenate([ot[:, h * qb:(h + 1) * qb].T for h in range(N_HEADS)], axis=1)
    g = g_ref[...]
    o_ref[...] = (o * (g * jax.nn.sigmoid(g))).astype(o_ref.dtype)


def _dsa(iq, iwt, ik, qa, ka, vt, u32, batch, seq):
    t = qa.shape[0]
    nb = seq // Q_BLOCK
    topk = min(DSA_TOPK_MAX, seq // 4)
    kern = functools.partial(_dsa_kernel, topk=topk)
    return pl.pallas_call(
        kern,
        out_shape=jax.ShapeDtypeStruct((t, BRANCH_WIDTH), BF16),
        grid=(batch, nb),
        in_specs=[pl.BlockSpec((IDX_HEADS, Q_BLOCK, IDX_DIM), lambda b, i: (0, b * nb + i, 0)),
                  pl.BlockSpec((IDX_HEADS, Q_BLOCK), lambda b, i: (0, b * nb + i)),
                  pl.BlockSpec((seq, IDX_DIM), lambda b, i: (b, 0)),
                  pl.BlockSpec((Q_BLOCK, BRANCH_WIDTH), lambda b, i: (b * nb + i, 0)),
                  pl.BlockSpec((seq, HEAD_DIM), lambda b, i: (b, 0)),
                  pl.BlockSpec((seq // DSA_TK, VT_ROWS, DSA_TK), lambda b, i: (b, 0, 0)),
                  pl.BlockSpec((Q_BLOCK, 1024), lambda b, i: (b * nb + i, BIG["a_g"]))],
        out_specs=pl.BlockSpec((Q_BLOCK, BRANCH_WIDTH), lambda b, i: (b * nb + i, 0)),
        scratch_shapes=[pltpu.VMEM((seq // DSA_TK, DSA_TK, Q_BLOCK), jnp.int32),
                        pltpu.VMEM((WORD_BITS, seq // DSA_TK * CHUNK_ROWS, Q_BLOCK), jnp.int32),
                        pltpu.VMEM((1, N_HEADS * Q_BLOCK), F32),
                        pltpu.VMEM((VT_ROWS, N_HEADS * Q_BLOCK), F32)],
        compiler_params=_cparams(("arbitrary", "arbitrary")),
        name="dsa_attention",
    )(iq, iwt, ik, qa, ka, vt, u32)


def _conv_kernel(b_ref, c_ref, x_ref, g_ref, cp_ref, xp_ref, w_ref, o_ref, *, tiles_per_seq):
    i = pl.program_id(0)
    u = c_ref[...] * x_ref[...]
    prev = cp_ref[...] * xp_ref[...]
    prev = jnp.where(i % tiles_per_seq == 0, 0.0, prev)
    row = lax.broadcasted_iota(jnp.int32, u.shape, 0)
    u1 = jnp.where(row == 0, prev[7:8, :], pltpu.roll(u, 1, axis=0))
    u2 = jnp.where(row == 0, prev[6:7, :], jnp.where(row == 1, prev[7:8, :], pltpu.roll(u, 2, axis=0)))
    w = w_ref[...]
    y = w[0:1, :] * u2 + w[1:2, :] * u1 + w[2:3, :] * u
    g = g_ref[...]
    o_ref[...] = (b_ref[...] * y * (g * jax.nn.sigmoid(g))).astype(o_ref.dtype)


def _conv(u32, conv_w, seq, tm=512):
    t = u32.shape[0]
    tps = seq // tm
    kern = functools.partial(_conv_kernel, tiles_per_seq=tps)

    def big(name):
        return pl.BlockSpec((tm, 1024), lambda i: (i, BIG[name]))

    def halo(name):
        return pl.BlockSpec((8, 1024), lambda i: (jnp.maximum(i * (tm // 8) - 1, 0), BIG[name]))

    return pl.pallas_call(
        kern,
        out_shape=jax.ShapeDtypeStruct((t, BRANCH_WIDTH), BF16),
        grid=(t // tm,),
        in_specs=[big("b_b"), big("b_c"), big("b_x"), big("b_g"), halo("b_c"), halo("b_x"),
                  pl.BlockSpec((CONV_K, BRANCH_WIDTH), lambda i: (0, 0))],
        out_specs=pl.BlockSpec((tm, BRANCH_WIDTH), lambda i: (i, 0)),
        compiler_params=_cparams(("parallel",)),
        name="short_conv",
    )(u32, u32, u32, u32, u32, u32, conv_w)


FOX_TK = 512
FOX_TQ = 1024
FOX_TC = 1024
FOX_HEADS_PER_STEP = 4


def _fox_cum_kernel(cf_ref, bias_ref, ccol_ref, crow_ref, carry_ref):
    j = pl.program_id(1)

    @pl.when(j == 0)
    def _():
        carry_ref[...] = jnp.zeros_like(carry_ref)

    x = cf_ref[...] + bias_ref[...]
    log_f = jnp.minimum(x, 0.0) - jnp.log1p(jnp.exp(-jnp.abs(x)))
    tc = x.shape[0]
    tri = (lax.broadcasted_iota(jnp.int32, (tc, tc), 1) <= lax.broadcasted_iota(jnp.int32, (tc, tc), 0)).astype(F32)
    cs = jnp.dot(tri, log_f, precision=lax.Precision.HIGHEST, preferred_element_type=F32) + carry_ref[...]
    ccol_ref[...] = cs
    carry_ref[...] = cs[tc - 1:tc, :]
    cst = cs.T
    for part in range(tc // FOX_TQ):
        crow_ref[part] = cst[0:N_HEADS, part * FOX_TQ:(part + 1) * FOX_TQ]


def _fox_cum(u32, bias, batch, seq):
    t = u32.shape[0]
    tc = FOX_TC
    ns = seq // tc
    per = tc // FOX_TQ
    bias_p = jnp.zeros((1, LANES), F32).at[0, :N_HEADS].set(bias.astype(F32))
    return pl.pallas_call(
        _fox_cum_kernel,
        out_shape=(jax.ShapeDtypeStruct((t, LANES), F32),
                   jax.ShapeDtypeStruct((t // FOX_TQ, N_HEADS, FOX_TQ), F32)),
        grid=(batch, ns),
        in_specs=[pl.BlockSpec((tc, LANES), lambda b, j: (b * ns + j, SLOT_CF)),
                  pl.BlockSpec((1, LANES), lambda b, j: (0, 0))],
        out_specs=(pl.BlockSpec((tc, LANES), lambda b, j: (b * ns + j, 0)),
                   pl.BlockSpec((per, N_HEADS, FOX_TQ), lambda b, j: (b * ns + j, 0, 0))),
        scratch_shapes=[pltpu.VMEM((1, LANES), F32)],
        compiler_params=_cparams(("parallel", "arbitrary")),
        name="fox_cumsum",
    )(u32, bias_p)


def _fox_prep_kernel(q_ref, k_ref, v_ref, qo_ref, ko_ref, vto_ref):
    tm = q_ref.shape[0]
    qo_ref[...] = q_ref[...].astype(BF16)
    ko_ref[...] = k_ref[...].astype(BF16)
    for h in range(N_HEADS):
        vto_ref[h, 0:HEAD_DIM, :] = v_ref[:, h * HEAD_DIM:(h + 1) * HEAD_DIM].T.astype(BF16)
        vto_ref[h, HEAD_DIM:VT_ROWS, :] = jnp.ones((VT_ROWS - HEAD_DIM, tm), BF16)


def _fox_prep(u32):
    t = u32.shape[0]
    tm = FOX_TK

    def big(name):
        return pl.BlockSpec((tm, 1024), lambda i: (i, BIG[name]))

    row_spec = pl.BlockSpec((tm, BRANCH_WIDTH), lambda i: (i, 0))
    return pl.pallas_call(
        _fox_prep_kernel,
        out_shape=(jax.ShapeDtypeStruct((t, BRANCH_WIDTH), BF16),
                   jax.ShapeDtypeStruct((t, BRANCH_WIDTH), BF16),
                   jax.ShapeDtypeStruct((t // tm, N_HEADS, VT_ROWS, tm), BF16)),
        grid=(t // tm,),
        in_specs=[big("c_q"), big("c_k"), big("c_v")],
        out_specs=(row_spec, row_spec, pl.BlockSpec((None, N_HEADS, VT_ROWS, tm), lambda i: (i, 0, 0, 0))),
        compiler_params=_cparams(("parallel",)),
        name="fox_prep",
    )(u32, u32, u32)


def _fox_kernel(q_ref, k_ref, vt_ref, ccol_ref, crow_ref, g_ref, o_ref, m_ref, acc_ref):
    tq, tk, hp = FOX_TQ, FOX_TK, FOX_HEADS_PER_STEP
    h0 = pl.program_id(1) * hp
    q0 = pl.program_id(2) * tq
    scale = HEAD_DIM ** -0.5
    log2e = np.log2(np.e)
    head_lane = lax.broadcasted_iota(jnp.int32, (tk, LANES), 1)

    sub = lax.broadcasted_iota(jnp.int32, (N_HEADS, tq), 0)
    crow = crow_ref[...]
    cqs = [jnp.sum(jnp.where(sub == h0 + hh, crow, 0.0), axis=0, keepdims=True) * log2e for hh in range(hp)]
    m_ref[...] = jnp.full(m_ref.shape, -jnp.inf, F32)
    acc_ref[...] = jnp.zeros(acc_ref.shape, F32)
    krow = lax.broadcasted_iota(jnp.int32, (tk, LANES), 0)
    qlane = lax.broadcasted_iota(jnp.int32, (tk, LANES), 1)

    def head_step(hh, c, k0, masked):
        hcols = slice(hh * HEAD_DIM, (hh + 1) * HEAD_DIM)
        s = lax.dot_general(k_ref[pl.ds(k0, tk), hcols], q_ref[:, hcols], NT_DIMS, preferred_element_type=F32)
        ckb = jnp.sum(jnp.where(head_lane == h0 + hh, ccol_ref[pl.ds(k0, tk), :], 0.0),
                      axis=1, keepdims=True) * log2e
        cq = cqs[hh]
        m_prev = m_ref[hh]
        ps, ms, alphas = [], [], []
        for j in range(tq // LANES):
            cols = slice(j * LANES, (j + 1) * LANES)
            y = s[:, cols] * (scale * log2e) - ckb
            if masked:
                y = jnp.where(k0 + krow <= q0 + j * LANES + qlane, y, -jnp.inf)
            m_new = jnp.maximum(m_prev[:, cols], jnp.max(y, axis=0, keepdims=True) + cq[:, cols])
            ps.append(jnp.exp2(y - (m_new - cq[:, cols])).astype(BF16))
            ms.append(m_new)
            alphas.append(jnp.exp2(m_prev[:, cols] - m_new))
        m_ref[hh] = jnp.concatenate(ms, axis=1)
        acc_ref[hh] = jnp.concatenate(alphas, axis=1) * acc_ref[hh] + jnp.dot(
            vt_ref[c, hh], jnp.concatenate(ps, axis=1), preferred_element_type=F32)

    def make_body(masked):
        def body(c, carry):
            k0 = pl.multiple_of(c * tk, tk)
            for hh in range(hp):
                head_step(hh, c, k0, masked)
            return carry
        return body

    n_full = q0 // tk
    n_all = (q0 + tq + tk - 1) // tk
    lax.fori_loop(0, n_full, make_body(False), 0)
    lax.fori_loop(n_full, n_all, make_body(True), 0)
    for hh in range(hp):
        hcols = slice(hh * HEAD_DIM, (hh + 1) * HEAD_DIM)
        g = g_ref[:, hcols]
        ot = acc_ref[hh, 0:HEAD_DIM, :] / acc_ref[hh, HEAD_DIM:HEAD_DIM + 1, :]
        o_ref[:, hcols] = (ot.T * (g * jax.nn.sigmoid(g))).astype(o_ref.dtype)


def _fox(qb, kb, vt, u32, ccol, crow, batch, seq):
    t = qb.shape[0]
    tq, hp = FOX_TQ, FOX_HEADS_PER_STEP
    wide = hp * HEAD_DIM
    nq = seq // tq
    nkc = seq // FOX_TK
    cg = BIG["c_g"] * 1024 // wide
    once = pl.Buffered(1)
    return pl.pallas_call(
        _fox_kernel,
        out_shape=jax.ShapeDtypeStruct((t, BRANCH_WIDTH), BF16),
        grid=(batch, N_HEADS // hp, nq),
        in_specs=[pl.BlockSpec((tq, wide), lambda b, h, i: (b * nq + i, h)),
                  pl.BlockSpec((seq, wide), lambda b, h, i: (b, h), pipeline_mode=once),
                  pl.BlockSpec((nkc, hp, VT_ROWS, FOX_TK), lambda b, h, i: (b, h, 0, 0), pipeline_mode=once),
                  pl.BlockSpec((seq, LANES), lambda b, h, i: (b, 0), pipeline_mode=once),
                  pl.BlockSpec((None, N_HEADS, tq), lambda b, h, i: (b * nq + i, 0, 0)),
                  pl.BlockSpec((tq, wide), lambda b, h, i: (b * nq + i, cg + h))],
        out_specs=pl.BlockSpec((tq, wide), lambda b, h, i: (b * nq + i, h)),
        scratch_shapes=[pltpu.VMEM((hp, 1, tq), F32), pltpu.VMEM((hp, VT_ROWS, tq), F32)],
        compiler_params=_cparams(("parallel", "parallel", "arbitrary")),
        name="fox_attention",
    )(qb, kb, vt, ccol, crow, u32)


def _hgrn_kernel(q_ref, f_ref, i_ref, g_ref, lb_ref, nw_ref, o_ref, st_ref):
    cs, sub = HGRN_CHUNK, HGRN_SUB
    ts = q_ref.shape[0]

    @pl.when(pl.program_id(2) == 0)
    def _():
        st_ref[...] = jnp.zeros_like(st_ref)

    tri = (lax.broadcasted_iota(jnp.int32, (cs, cs), 1) <= lax.broadcasted_iota(jnp.int32, (cs, cs), 0)).astype(F32)
    row_c = lax.broadcasted_iota(jnp.int32, (cs, HEAD_DIM), 0)
    row_h = lax.broadcasted_iota(jnp.int32, (sub // 2, HEAD_DIM), 0)
    lane_h = lax.broadcasted_iota(jnp.int32, (sub // 2, cs), 1)

    def chunk(ci, carry):
        for hh in range(HGRN_HEADS_PER_STEP):
            head_chunk(pl.multiple_of(ci * cs, cs), hh)
        return carry

    def head_chunk(r0, hh):
        cols = slice(hh * HEAD_DIM, (hh + 1) * HEAD_DIM)
        lb = lb_ref[:, cols]
        nw = nw_ref[:, cols]
        f = lb + (1.0 - lb) * jax.nn.sigmoid(f_ref[pl.ds(r0, cs), cols])
        kk = 1.0 - f
        bc = jnp.dot(tri, jnp.log(f), precision=lax.Precision.HIGHEST, preferred_element_type=F32)
        bk = bc - jnp.log(kk)
        q = q_ref[pl.ds(r0, cs), cols]
        vb = i_ref[pl.ds(r0, cs), cols].astype(BF16)
        st = st_ref[hh]
        o = lax.dot_general((q * jnp.exp(bc)).astype(BF16), st.astype(BF16), NT_DIMS, preferred_element_type=F32)
        att_rows = []
        for si in range(cs // sub):
            lo = si * sub
            bi = bc[lo:lo + sub, :]
            qi = q[lo:lo + sub, :]
            att = jnp.zeros((sub, cs), F32)
            if si > 0:
                b0 = bc[lo - 1:lo, :]
                kt = kk * jnp.exp(jnp.where(row_c < lo, b0 - bc, -jnp.inf))
                qs = qi * jnp.exp(bi - b0)
                att = lax.dot_general(qs.astype(BF16), kt.astype(BF16), NT_DIMS, preferred_element_type=F32)
            half = sub // 2
            diag = [jnp.zeros((half, cs), F32), jnp.zeros((half, cs), F32)]
            for s in range(sub):
                bs = bk[lo + s:lo + s + 1, :]
                for part in range(s // half, 2):
                    rows = slice(part * half, (part + 1) * half)
                    arg = bi[rows, :] - bs
                    if part == s // half:
                        arg = jnp.where(row_h >= s - part * half, arg, -jnp.inf)
                    p = qi[rows, :] * jnp.exp(arg)
                    diag[part] = diag[part] + jnp.where(lane_h == lo + s, jnp.sum(p, axis=1, keepdims=True), 0.0)
            att_rows.append(att + jnp.concatenate(diag, axis=0))
        att = jnp.concatenate(att_rows, axis=0)
        o = o + jnp.dot(att.astype(BF16), vb, preferred_element_type=F32)
        bl = bc[cs - 1:cs, :]
        kd = kk * jnp.exp(bl - bc)
        st_ref[hh] = st * jnp.exp(bl) + lax.dot_general(vb, kd.astype(BF16), TN_DIMS, preferred_element_type=F32)
        y = o * lax.rsqrt(jnp.mean(o * o, axis=1, keepdims=True) + NORM_EPS) * nw
        g = g_ref[pl.ds(r0, cs), cols]
        o_ref[pl.ds(r0, cs), cols] = (y * (g * jax.nn.sigmoid(g))).astype(o_ref.dtype)

    lax.fori_loop(0, ts // cs, chunk, 0)


def _hgrn(u32, lb, norm_w, batch, seq, ts=512):
    t = u32.shape[0]
    ts = min(ts, seq)
    ns = seq // ts
    hp = HGRN_HEADS_PER_STEP
    wide = hp * HEAD_DIM
    dq, df, di, dg = (BIG[n] * 1024 // wide for n in ("d_q", "d_f", "d_i", "d_g"))

    def col(c0):
        return pl.BlockSpec((ts, wide), lambda b, h, j: (b * ns + j, c0 + h))

    vec = pl.BlockSpec((1, wide), lambda b, h, j: (0, h))
    return pl.pallas_call(
        _hgrn_kernel,
        out_shape=jax.ShapeDtypeStruct((t, BRANCH_WIDTH), BF16),
        grid=(batch, N_HEADS // hp, ns),
        in_specs=[col(dq), col(df), col(di), col(dg), vec, vec],
        out_specs=pl.BlockSpec((ts, wide), lambda b, h, j: (b * ns + j, h)),
        scratch_shapes=[pltpu.VMEM((hp, HEAD_DIM, HEAD_DIM), F32)],
        compiler_params=_cparams(("parallel", "parallel", "arbitrary")),
        name="hgrn2",
    )(u32, u32, u32, u32, lb.reshape(1, BRANCH_WIDTH), norm_w.reshape(1, BRANCH_WIDTH).astype(F32))


def _merge_kernel(ya_ref, yb_ref, yc_ref, yd_ref, h_ref, wb_ref, wm_ref, bm_ref, o_ref):
    tn = o_ref.shape[1]
    h = h_ref[...]
    merged = None
    for br, y_ref in enumerate((ya_ref, yb_ref, yc_ref, yd_ref)):
        proj = jnp.dot(y_ref[...], wb_ref[br], preferred_element_type=F32)
        gate = jnp.concatenate(
            [jnp.dot(h[:, n * MERGE_BLOCK_DIM:(n + 1) * MERGE_BLOCK_DIM], wm_ref[br, n], preferred_element_type=F32)
             for n in range(tn // MERGE_BLOCK_DIM)], axis=1)
        term = jax.nn.sigmoid(gate + bm_ref[br:br + 1, :]) * proj
        merged = term if merged is None else merged + term
    o_ref[...] = merged.astype(o_ref.dtype)


def _merge(ys, h, wb, wm, bm, layer, tm=1024, tn=512):
    t = h.shape[0]
    tm = min(tm, t)
    y_spec = pl.BlockSpec((tm, BRANCH_WIDTH), lambda i, j: (i, 0))
    nmb = tn // MERGE_BLOCK_DIM
    return pl.pallas_call(
        _merge_kernel,
        out_shape=jax.ShapeDtypeStruct((t, D_MODEL), BF16),
        grid=(t // tm, D_MODEL // tn),
        in_specs=[y_spec, y_spec, y_spec, y_spec,
                  pl.BlockSpec((tm, tn), lambda i, j: (i, j)),
                  pl.BlockSpec((None, 4, BRANCH_WIDTH, tn), lambda i, j: (layer, 0, 0, j)),
                  pl.BlockSpec((None, 4, nmb, MERGE_BLOCK_DIM, MERGE_BLOCK_DIM), lambda i, j: (layer, 0, j, 0, 0)),
                  pl.BlockSpec((None, 4, tn), lambda i, j: (layer, 0, j))],
        out_specs=pl.BlockSpec((tm, tn), lambda i, j: (i, j)),
        compiler_params=_cparams(("parallel", "parallel")),
        name="gated_merge",
    )(*ys, h, wb, wm, bm)


def kernel(x, norm_w, w_in, fox_f_bias, conv_w, hgrn_gamma, hgrn_norm_w, w_branch, w_merge, b_merge, w_out, final_norm_w):
    batch, seq, d = x.shape
    depth = w_in.shape[0]
    assert d == D_MODEL and w_in.shape[-1] == IN_WIDTH
    assert seq % FOX_TC == 0 and seq % (2 * DSA_TK) == 0 and seq % Q_BLOCK == 0, "unsupported sequence length"
    xf = x.reshape(batch * seq, d)
    w_in_p = _pad_in_proj(w_in)
    wb = w_branch.astype(BF16)
    wm = w_merge.astype(BF16)
    wo = w_out.astype(F32)
    lower = _lower_bounds(hgrn_gamma)
    tabs = _rope_tables(seq)
    for layer in range(depth):
        h = _rmsnorm(xf, norm_w[layer], BF16)
        u32 = _in_proj(h, w_in_p, layer)
        qa, ka, vt, iq, ik, iwt = _dsa_prep(u32, tabs, seq)
        y_a = _dsa(iq, iwt, ik, qa, ka, vt, u32, batch, seq)
        y_b = _conv(u32, conv_w[layer].astype(F32), seq)
        ccol, crow = _fox_cum(u32, fox_f_bias[layer], batch, seq)
        fq, fk, fvt = _fox_prep(u32)
        y_c = _fox(fq, fk, fvt, u32, ccol, crow, batch, seq)
        y_d = _hgrn(u32, lower[layer], hgrn_norm_w[layer], batch, seq)
        merged = _merge((y_a, y_b, y_c, y_d), h, wb, wm, b_merge.astype(F32), layer)
        xf = _out_proj(merged, wo, layer, xf)
    return _rmsnorm(xf, final_norm_w, F32).reshape(batch, seq, d)
```
